```python
import math
import jax, jax.numpy as jnp
from jax import lax
import numpy as np

D_MODEL = 1024
BATCH = 4
SEQ = 4096
DEPTH = 1
DEC_BATCH = 16
DEC_SEQ = 32
PAST_LEN = 2048

CHUNK = 64
RET_HEADS = 4
RET_DK = 128
RET_DV = 128
SB_HEADS = 8
SB_DH = 64
SB_BLOCK = 128
MIX_WIDTH = RET_HEADS * RET_DV + SB_HEADS * SB_DH
N_MEM = 256
MEM_HEADS = 4
MEM_DH = D_MODEL // MEM_HEADS
D_FF = 2816
CONV_W = 3
ROPE_BASE = 10000.0
LN_EPS = 1e-5
RMS_EPS = 1e-6
DN_ALPHA = (2.0 * DEPTH) ** 0.25
DN_BETA = (8.0 * DEPTH) ** -0.25
IN_SIZES = (RET_HEADS * RET_DK, RET_HEADS * RET_DK, RET_HEADS * RET_DV, RET_HEADS * RET_DV,
            SB_HEADS * SB_DH, SB_HEADS * SB_DH, SB_HEADS * SB_DH)
IN_WIDTH = sum(IN_SIZES)
IN_SPLITS = tuple(int(s) for s in np.cumsum(IN_SIZES)[:-1])

kernel_name = "retention_stickbreaking_hybrid_stream_step"


def layer_norm(x, g, b):
    xf = x.astype(jnp.float32)
    mu = jnp.mean(xf, -1, keepdims=True)
    var = jnp.mean(jnp.square(xf - mu), -1, keepdims=True)
    y = (xf - mu) * lax.rsqrt(var + LN_EPS) * g.astype(jnp.float32) + b.astype(jnp.float32)
    return y.astype(x.dtype)


def split_heads(t, n):
    b, s, w = t.shape
    return t.reshape(b, s, n, w // n).transpose(0, 2, 1, 3)


def merge_heads(t):
    b, h, s, d = t.shape
    return t.transpose(0, 2, 1, 3).reshape(b, s, h * d)


def rope(x, pos):
    half = x.shape[-1] // 2
    inv = 1.0 / (ROPE_BASE ** (jnp.arange(half, dtype=jnp.float32) / half))
    ang = pos.astype(jnp.float32)[:, None] * inv[None, :]
    cos, sin = jnp.cos(ang).astype(x.dtype), jnp.sin(ang).astype(x.dtype)
    x1, x2 = x[..., :half], x[..., half:]
    return jnp.concatenate([x1 * cos - x2 * sin, x1 * sin + x2 * cos], axis=-1)


def retention(q, k, v, s0, chunk):
    b, h, t, dk = q.shape
    dv = v.shape[-1]
    n = t // chunk
    log_g = jnp.log1p(-(2.0 ** (-5.0 - jnp.arange(h, dtype=jnp.float32))))
    idx = jnp.arange(chunk, dtype=jnp.float32)
    diff = idx[:, None] - idx[None, :]
    decay_mask = jnp.where(diff >= 0, jnp.exp(log_g[:, None, None] * jnp.maximum(diff, 0.0)), 0.0)
    q_dec = jnp.exp(log_g[:, None] * (idx + 1.0))
    k_dec = jnp.exp(log_g[:, None] * (chunk - 1.0 - idx))
    chunk_dec = jnp.exp(log_g * chunk)
    qc = q.astype(jnp.float32).reshape(b, h, n, chunk, dk)
    kc = k.astype(jnp.float32).reshape(b, h, n, chunk, dk)
    vc = v.astype(jnp.float32).reshape(b, h, n, chunk, dv)
    scores = jnp.einsum('bhnid,bhnjd->bhnij', qc, kc) * decay_mask[None, :, None]
    o_intra = jnp.einsum('bhnij,bhnje->bhnie', scores, vc)
    u = jnp.einsum('bhnjd,bhnje->nbhde', kc * k_dec[None, :, None, :, None], vc)

    def step(s, u_n):
        return chunk_dec[None, :, None, None] * s + u_n, s

    s_final, s_prev = lax.scan(step, s0.astype(jnp.float32), u)
    o_cross = jnp.einsum('bhnid,nbhde->bhnie', qc * q_dec[None, :, None, :, None], s_prev)
    return (o_intra + o_cross).reshape(b, h, t, dv), s_final


def stick_breaking_block(q, k, v, q_pos, k_pos):
    z = jnp.einsum('bhqd,bhkd->bhqk', q, k).astype(jnp.float32) * (SB_DH ** -0.5)
    valid = k_pos[None, :] < q_pos[:, None]
    log_fail = jnp.where(valid, jax.nn.log_sigmoid(-z), 0.0)
    after = lax.cumsum(log_fail, axis=3, reverse=True) - log_fail
    a = jnp.where(valid, jnp.exp(jax.nn.log_sigmoid(z) + after), 0.0)
    return jnp.einsum('bhqk,bhkd->bhqd', a, v.astype(jnp.float32)).astype(v.dtype)


def stick_breaking_attention(q, k_all, v_all, q_pos):
    b, h, t, d = q.shape
    qb = min(SB_BLOCK, t)
    nb = t // qb
    k_pos = jnp.arange(k_all.shape[2], dtype=jnp.int32)
    q_blocks = q.reshape(b, h, nb, qb, d).transpose(2, 0, 1, 3, 4)
    pos_blocks = q_pos.reshape(nb, qb)
    out = lax.map(lambda a: stick_breaking_block(a[0], k_all, v_all, a[1], k_pos), (q_blocks, pos_blocks))
    return out.transpose(1, 2, 0, 3, 4).reshape(b, h, t, d)


def memory_attention(x, mem_k, mem_v, w_q, w_o):
    q = split_heads(x @ w_q, MEM_HEADS)
    s = jnp.einsum('bhtd,bhmd->bhtm', q, mem_k.astype(q.dtype)).astype(jnp.float32) * (MEM_DH ** -0.5)
    p = jax.nn.softmax(s, axis=-1).astype(x.dtype)
    o = jnp.einsum('bhtm,bhmd->bhtd', p, mem_v.astype(x.dtype))
    return merge_heads(o) @ w_o


def conv_ffn(x, conv_past, w_up, conv_w, conv_b, w_down):
    t = x.shape[1]
    u = x @ w_up
    u_pad = jnp.concatenate([conv_past.astype(u.dtype), u], axis=1)
    c = conv_b
    for j in range(CONV_W):
        c = c + conv_w[j] * u_pad[:, j:j + t]
    a, g = jnp.split(c, 2, axis=-1)
    return (jax.nn.silu(a) * g) @ w_down, u_pad[:, -(CONV_W - 1):]


def trunk_layer(x, mem_k, mem_v, ret_s0, sb_k_past, sb_v_past, conv_past, past_len,
                w_in, w_o, ln1_g, ln1_b, w_q_mem, w_o_mem, ln2_g, ln2_b,
                w_up, conv_w, conv_b, w_down, ln3_g, ln3_b):
    t = x.shape[1]
    pos = past_len + jnp.arange(t, dtype=jnp.int32)
    rq, rk, rv, rg, sq, sk, sv = jnp.split(x @ w_in, IN_SPLITS, axis=-1)
    rq = rope(split_heads(rq, RET_HEADS), pos)
    rk = rope(split_heads(rk, RET_HEADS), pos) * (RET_DK ** -0.5)
    rv = split_heads(rv, RET_HEADS)
    ro, ret_state = retention(rq, rk, rv, ret_s0, min(CHUNK, t))
    ro = ro * lax.rsqrt(jnp.mean(jnp.square(ro), -1, keepdims=True) + RMS_EPS)
    ro = merge_heads(ro.astype(x.dtype)) * jax.nn.silu(rg)
    sq, sk, sv = split_heads(sq, SB_HEADS), split_heads(sk, SB_HEADS), split_heads(sv, SB_HEADS)
    k_all = jnp.concatenate([sb_k_past.astype(sk.dtype), sk], axis=2)
    v_all = jnp.concatenate([sb_v_past.astype(sv.dtype), sv], axis=2)
    so = merge_heads(stick_breaking_attention(sq, k_all, v_all, pos))
    mix = jnp.concatenate([ro, so], axis=-1) @ w_o
    x = layer_norm(DN_ALPHA * x + mix, ln1_g, ln1_b)
    x = layer_norm(DN_ALPHA * x + memory_attention(x, mem_k, mem_v, w_q_mem, w_o_mem), ln2_g, ln2_b)
    f, conv_state = conv_ffn(x, conv_past, w_up, conv_w, conv_b, w_down)
    x = layer_norm(DN_ALPHA * x + f, ln3_g, ln3_b)
    return x, sk, sv, ret_state.astype(ret_s0.dtype), conv_state


def setup_inputs(seed: int = 0) -> dict:
    key = jax.random.key(seed)
    ks = jax.random.split(key, 26)
    f32 = jnp.float32

    def nrm(k, shape, scale):
        return jax.random.normal(k, shape, f32) * scale

    col_scale = jnp.asarray(np.concatenate([
        np.full((s,), DN_BETA if i in (2, 6) else 1.0, np.float32) for i, s in enumerate(IN_SIZES)]))
    return {
        "x_prompt": nrm(ks[0], (BATCH, SEQ, D_MODEL), 1.0),
        "x_sample": nrm(ks[1], (DEC_BATCH, DEC_SEQ, D_MODEL), 1.0),
        "cache_sb_k": nrm(ks[2], (DEPTH, DEC_BATCH, SB_HEADS, PAST_LEN, SB_DH), 1.0),
        "cache_sb_v": nrm(ks[3], (DEPTH, DEC_BATCH, SB_HEADS, PAST_LEN, SB_DH), DN_BETA),
        "state_ret": nrm(ks[4], (DEPTH, DEC_BATCH, RET_HEADS, RET_DK, RET_DV), 0.1),
        "state_ffn_conv": nrm(ks[5], (DEPTH, DEC_BATCH, CONV_W - 1, 2 * D_FF), 1.0),
        "cache_mem_k": nrm(ks[6], (DEPTH, DEC_BATCH, MEM_HEADS, N_MEM, MEM_DH), 1.0),
        "cache_mem_v": nrm(ks[7], (DEPTH, DEC_BATCH, MEM_HEADS, N_MEM, MEM_DH), DN_BETA),
        "mem_prompt": nrm(ks[8], (BATCH, N_MEM, D_MODEL), 1.0),
        "w_in": nrm(ks[9], (DEPTH, D_MODEL, IN_WIDTH), D_MODEL ** -0.5) * col_scale,
        "w_o": nrm(ks[10], (DEPTH, MIX_WIDTH, D_MODEL), MIX_WIDTH ** -0.5 * DN_BETA),
        "ln1_g": 1.0 + nrm(ks[11], (DEPTH, D_MODEL), 0.01),
        "ln1_b": nrm(ks[12], (DEPTH, D_MODEL), 0.01),
        "w_q_mem": nrm(ks[13], (DEPTH, D_MODEL, D_MODEL), D_MODEL ** -0.5),
        "w_k_mem": nrm(ks[14], (DEPTH, D_MODEL, D_MODEL), D_MODEL ** -0.5),
        "w_v_mem": nrm(ks[15], (DEPTH, D_MODEL, D_MODEL), D_MODEL ** -0.5 * DN_BETA),
        "w_o_mem": nrm(ks[16], (DEPTH, D_MODEL, D_MODEL), D_MODEL ** -0.5 * DN_BETA),
        "ln2_g": 1.0 + nrm(ks[17], (DEPTH, D_MODEL), 0.01),
        "ln2_b": nrm(ks[18], (DEPTH, D_MODEL), 0.01),
        "w_up": nrm(ks[19], (DEPTH, D_MODEL, 2 * D_FF), D_MODEL ** -0.5),
        "conv_w": nrm(ks[20], (DEPTH, CONV_W, 2 * D_FF), CONV_W ** -0.5),
        "conv_b": nrm(ks[21], (DEPTH, 2 * D_FF), 0.01),
        "w_down": nrm(ks[22], (DEPTH, D_FF, D_MODEL), D_FF ** -0.5 * DN_BETA),
        "ln3_g": 1.0 + nrm(ks[23], (DEPTH, D_MODEL), 0.01),
        "ln3_b": nrm(ks[24], (DEPTH, D_MODEL), 0.01),
    }


def reference(x_prompt, x_sample, cache_sb_k, cache_sb_v, state_ret, state_ffn_conv,
              cache_mem_k, cache_mem_v, mem_prompt,
              w_in, w_o, ln1_g, ln1_b, w_q_mem, w_k_mem, w_v_mem, w_o_mem, ln2_g, ln2_b,
              w_up, conv_w, conv_b, w_down, ln3_g, ln3_b):
    xp, xs = x_prompt, x_sample
    bp = xp.shape[0]
    past_len = cache_sb_k.shape[3]
    p_k, p_v, p_s, p_c, p_mk, p_mv = [], [], [], [], [], []
    s_k, s_v, s_s, s_c = [], [], [], []
    for l in range(DEPTH):
        lw = (w_in[l], w_o[l], ln1_g[l], ln1_b[l], w_q_mem[l], w_o_mem[l], ln2_g[l], ln2_b[l],
              w_up[l], conv_w[l], conv_b[l], w_down[l], ln3_g[l], ln3_b[l])
        mk = split_heads(mem_prompt @ w_k_mem[l], MEM_HEADS)
        mv = split_heads(mem_prompt @ w_v_mem[l], MEM_HEADS)
        zero_s = jnp.zeros((bp, RET_HEADS, RET_DK, RET_DV), xp.dtype)
        zero_kv = jnp.zeros((bp, SB_HEADS, 0, SB_DH), xp.dtype)
        zero_c = jnp.zeros((bp, CONV_W - 1, 2 * D_FF), xp.dtype)
        xp, pk, pv, ps, pc = trunk_layer(xp, mk, mv, zero_s, zero_kv, zero_kv, zero_c, 0, *lw)
        p_k.append(pk); p_v.append(pv); p_s.append(ps); p_c.append(pc); p_mk.append(mk); p_mv.append(mv)
        xs, sk, sv, ss, sc = trunk_layer(xs, cache_mem_k[l], cache_mem_v[l], state_ret[l],
                                         cache_sb_k[l], cache_sb_v[l], state_ffn_conv[l], past_len, *lw)
        s_k.append(sk); s_v.append(sv); s_s.append(ss); s_c.append(sc)
    new_sb_k_prompt = jnp.stack(p_k)
    new_sb_v_prompt = jnp.stack(p_v)
    new_state_ret_prompt = jnp.stack(p_s)
    new_ffn_conv_prompt = jnp.stack(p_c)
    new_mem_k_prompt = jnp.stack(p_mk)
    new_mem_v_prompt = jnp.stack(p_mv)
    new_sb_k_sample = jnp.stack(s_k)
    new_sb_v_sample = jnp.stack(s_v)
    new_state_ret_sample = jnp.stack(s_s)
    new_ffn_conv_sample = jnp.stack(s_c)
    return (xp, xs, new_sb_k_prompt, new_sb_v_prompt, new_state_ret_prompt, new_ffn_conv_prompt,
            new_mem_k_prompt, new_mem_v_prompt, new_sb_k_sample, new_sb_v_sample,
            new_state_ret_sample, new_ffn_conv_sample)
```

```python
import functools
import math

import jax
import jax.numpy as jnp
from jax import lax
from jax.experimental import pallas as pl
from jax.experimental.pallas import tpu as pltpu

F32 = jnp.float32
BF16 = jnp.bfloat16

RET_HEADS = 4
RET_D = 128
SB_HEADS = 8
SB_DH = 64
MEM_HEADS = 4
CONV_W = 3
ROPE_BASE = 10000.0
LN_EPS = 1e-5
RMS_EPS = 1e-6
REF_CHUNK = 64

VMEM_LIMIT = 56 * 1024 * 1024


def _const_spec(shape):
    nd = len(shape)
    return pl.BlockSpec(shape, lambda *_: (0,) * nd, pipeline_mode=pl.Buffered(1))


def _ln(x, g, b):
    mu = jnp.mean(x, -1, keepdims=True)
    xc = x - mu
    var = jnp.mean(xc * xc, -1, keepdims=True)
    return xc * lax.rsqrt(var + LN_EPS) * g + b


def _silu(x):
    return x * (1.0 / (1.0 + jnp.exp(-x)))


def _dot(a, b):
    return jnp.dot(a, b, preferred_element_type=F32)


def _dot_nt(a, b):
    return lax.dot_general(a, b, (((1,), (1,)), ((), ())), preferred_element_type=F32)


def _proj_kernel(x_ref, w_ref, cos_ref, sin_ref, *out_refs, nb, tl, d_model, head_split_q):
    if head_split_q:
        ret_ref, sqh_ref, sk_ref, sv_ref = out_refs
    else:
        ret_ref, sq_ref, skb_ref, svb_ref, sk_ref, sv_ref = out_refs
    m = nb * tl
    xb = x_ref[...].reshape(m, d_model).astype(BF16)
    cos = jnp.broadcast_to(cos_ref[...][None], (nb, tl, RET_D)).reshape(m, RET_D)
    sin = jnp.broadcast_to(sin_ref[...][None], (nb, tl, RET_D)).reshape(m, RET_D)
    hw = RET_HEADS * RET_D

    def seg(i):
        return _dot(xb, w_ref[:, i * hw:(i + 1) * hw])

    def rope(r, scale):
        outs = []
        for h in range(RET_HEADS):
            xh = r[:, h * RET_D:(h + 1) * RET_D]
            o = xh * cos + pltpu.roll(xh, RET_D // 2, axis=1) * sin
            outs.append(o * scale if scale != 1.0 else o)
        return jnp.concatenate(outs, axis=-1)

    def put_ret(i, val):
        ret_ref[:, :, i * hw:(i + 1) * hw] = val.astype(BF16).reshape(nb, tl, hw)

    put_ret(0, rope(seg(0), 1.0))
    put_ret(1, rope(seg(1), RET_D ** -0.5))
    put_ret(2, seg(2))
    put_ret(3, seg(3))

    def put_heads(ref, val):
        for h in range(SB_HEADS):
            ref[:, h, :, :] = val[:, h * SB_DH:(h + 1) * SB_DH].astype(ref.dtype).reshape(nb, tl, SB_DH)

    sq = seg(4) * (SB_DH ** -0.5)
    if head_split_q:
        put_heads(sqh_ref, sq)
    else:
        sq_ref[...] = sq.astype(BF16).reshape(nb, tl, hw)
    sk = seg(5)
    put_heads(sk_ref, sk)
    if not head_split_q:
        skb_ref[...] = sk.astype(BF16).reshape(nb, tl, hw)
    sv = seg(6)
    put_heads(sv_ref, sv)
    if not head_split_q:
        svb_ref[...] = sv.astype(BF16).reshape(nb, tl, hw)


def _proj(x, w_in_b, cos, sin, *, nb, tl, head_split_q):
    b, t, d = x.shape
    hw = RET_HEADS * RET_D
    grid = (b // nb, t // tl)
    hs = lambda dt: jax.ShapeDtypeStruct((b, SB_HEADS, t, SB_DH), dt)
    hs_spec = pl.BlockSpec((nb, SB_HEADS, tl, SB_DH), lambda i, j: (i, 0, j, 0))
    row_spec = lambda w: pl.BlockSpec((nb, tl, w), lambda i, j: (i, j, 0))
    out_shape = [jax.ShapeDtypeStruct((b, t, 4 * hw), BF16)]
    out_specs = [row_spec(4 * hw)]
    if head_split_q:
        out_shape += [hs(BF16)]
        out_specs += [hs_spec]
    else:
        out_shape += [jax.ShapeDtypeStruct((b, t, hw), BF16)] * 3
        out_specs += [row_spec(hw)] * 3
    out_shape += [hs(F32), hs(F32)]
    out_specs += [hs_spec, hs_spec]
    return pl.pallas_call(
        functools.partial(_proj_kernel, nb=nb, tl=tl, d_model=d, head_split_q=head_split_q),
        grid=grid,
        in_specs=[row_spec(d), _const_spec(w_in_b.shape),
                  pl.BlockSpec((tl, RET_D), lambda i, j: (j, 0)),
                  pl.BlockSpec((tl, RET_D), lambda i, j: (j, 0))],
        out_specs=out_specs, out_shape=out_shape,
        compiler_params=pltpu.CompilerParams(
            dimension_semantics=("parallel", "parallel"), vmem_limit_bytes=VMEM_LIMIT),
        name="proj",
    )(x, w_in_b, cos, sin)


def _memkv_kernel(m_ref, wk_ref, wv_ref, mk_ref, mv_ref, mkb_ref, mvb_ref, *, dh):
    mb = m_ref[0].astype(BF16)
    for w_ref, o_ref, ob_ref in ((wk_ref, mk_ref, mkb_ref), (wv_ref, mv_ref, mvb_ref)):
        r = _dot(mb, w_ref[...])
        for h in range(MEM_HEADS):
            rh = r[:, h * dh:(h + 1) * dh]
            o_ref[0, h] = rh
            ob_ref[0, h] = rh.astype(BF16)


def _memkv(mem, wk_b, wv_b):
    b, n, d = mem.shape
    dh = d // MEM_HEADS
    spec = pl.BlockSpec((1, MEM_HEADS, n, dh), lambda i: (i, 0, 0, 0))
    return pl.pallas_call(
        functools.partial(_memkv_kernel, dh=dh),
        grid=(b,),
        in_specs=[pl.BlockSpec((1, n, d), lambda i: (i, 0, 0)), _const_spec(wk_b.shape), _const_spec(wv_b.shape)],
        out_specs=[spec] * 4,
        out_shape=[jax.ShapeDtypeStruct((b, MEM_HEADS, n, dh), F32)] * 2
        + [jax.ShapeDtypeStruct((b, MEM_HEADS, n, dh), BF16)] * 2,
        compiler_params=pltpu.CompilerParams(dimension_semantics=("parallel",), vmem_limit_bytes=VMEM_LIMIT),
        name="memkv",
    )(mem, wk_b, wv_b)


def _ret_kernel(*refs, c, has_s0):
    if has_s0:
        ret_ref, s0_ref, ro_ref, sf_ref, s_scr = refs
    else:
        ret_ref, ro_ref, sf_ref, s_scr = refs
    t = pl.program_id(1)

    @pl.when(t == 0)
    def _():
        s_scr[...] = s0_ref[0] if has_s0 else jnp.zeros_like(s_scr)

    hw = RET_HEADS * RET_D
    row = lax.broadcasted_iota(jnp.int32, (c, c), 0)
    col = lax.broadcasted_iota(jnp.int32, (c, c), 1)
    diff = (row - col).astype(F32)
    idx = lax.broadcasted_iota(jnp.int32, (c, 1), 0).astype(F32)
    for h in range(RET_HEADS):
        log_g = math.log1p(-(2.0 ** (-5.0 - h)))
        sl = lambda i: ret_ref[0, :, i * hw + h * RET_D:i * hw + (h + 1) * RET_D]
        q, k, v, g = sl(0), sl(1), sl(2), sl(3)
        dmask = jnp.where(diff >= 0, jnp.exp(log_g * jnp.maximum(diff, 0.0)), 0.0)
        scores = _dot_nt(q, k) * dmask
        o = _dot(scores.astype(BF16), v)
        s_prev = s_scr[h]
        qd = (q.astype(F32) * jnp.exp(log_g * (idx + 1.0))).astype(BF16)
        o = o + _dot(qd, s_prev.astype(BF16))
        kd = (k.astype(F32) * jnp.exp(log_g * (c - 1.0 - idx))).astype(BF16)
        s_scr[h] = math.exp(log_g * c) * s_prev + _dot(kd.T, v)
        o = o * lax.rsqrt(jnp.mean(o * o, -1, keepdims=True) + RMS_EPS)
        ro_ref[0, :, h * RET_D:(h + 1) * RET_D] = (o * _silu(g.astype(F32))).astype(BF16)

    @pl.when(t == pl.num_programs(1) - 1)
    def _():
        sf_ref[0] = s_scr[...]


def _retention(ret, s0, *, c):
    b, t, w = ret.shape
    hw = RET_HEADS * RET_D
    st_spec = pl.BlockSpec((1, RET_HEADS, RET_D, RET_D), lambda i, j: (i, 0, 0, 0))
    in_specs = [pl.BlockSpec((1, c, w), lambda i, j: (i, j, 0))]
    args = [ret]
    if s0 is not None:
        in_specs.append(st_spec)
        args.append(s0)
    return pl.pallas_call(
        functools.partial(_ret_kernel, c=c, has_s0=s0 is not None),
        grid=(b, t // c),
        in_specs=in_specs,
        out_specs=[pl.BlockSpec((1, c, hw), lambda i, j: (i, j, 0)), st_spec],
        out_shape=[jax.ShapeDtypeStruct((b, t, hw), BF16),
                   jax.ShapeDtypeStruct((b, RET_HEADS, RET_D, RET_D), F32)],
        scratch_shapes=[pltpu.VMEM((RET_HEADS, RET_D, RET_D), F32)],
        compiler_params=pltpu.CompilerParams(
            dimension_semantics=("parallel", "arbitrary"), vmem_limit_bytes=VMEM_LIMIT),
        name="retention",
    )(*args)


def _suffix_ones(n):
    j = lax.broadcasted_iota(jnp.int32, (n, n), 0)
    s = lax.broadcasted_iota(jnp.int32, (n, n), 1)
    return jnp.where(j >= s, 1.0, 0.0).astype(BF16)


def _sb_tile(q, k, v, acc, carry, valid, u):
    z = _dot_nt(q, k)
    lf = -(jnp.maximum(z, 0.0) + jnp.log(1.0 + jnp.exp(-jnp.abs(z))))
    if valid is not None:
        lf = jnp.where(valid, lf, 0.0)
    hi = lf.astype(BF16)
    lo = (lf - hi.astype(F32)).astype(BF16)
    cs = _dot(hi, u) + _dot(lo, u)
    a = jnp.exp(z + cs + carry)
    if valid is not None:
        a = jnp.where(valid, a, 0.0)
    acc = acc + _dot(a.astype(BF16), v)
    carry = carry + cs[:, 0:1]
    return acc, carry


def _sb_prompt_kernel(q_ref, k_ref, v_ref, o_ref, *, tq):
    qi = pl.program_id(2)
    q = q_ref[0]
    lanes = 2 * SB_DH
    lo_half = lax.broadcasted_iota(jnp.int32, (1, lanes), 1) < SB_DH
    zero = jnp.zeros_like(q)
    q2 = jnp.concatenate([jnp.where(lo_half, q, zero), jnp.where(lo_half, zero, q)], axis=0)
    row = lax.broadcasted_iota(jnp.int32, (2 * tq, tq), 0) & (tq - 1)
    col = lax.broadcasted_iota(jnp.int32, (2 * tq, tq), 1)
    u = _suffix_ones(tq)

    def tile(kt, acc, carry, valid):
        ks = pl.multiple_of(kt * tq, tq)
        return _sb_tile(q2, k_ref[0, pl.ds(ks, tq), :], v_ref[0, pl.ds(ks, tq), :], acc, carry, valid, u)

    acc = jnp.zeros((2 * tq, lanes), F32)
    carry = jnp.zeros((2 * tq, 1), F32)
    acc, carry = tile(qi, acc, carry, col < row)
    acc, carry = lax.fori_loop(0, qi, lambda j, c: tile(qi - 1 - j, c[0], c[1], None), (acc, carry))
    o_ref[0] = jnp.where(lo_half, acc[:tq], acc[tq:]).astype(BF16)


def _sb_prompt(sq, sk, sv, *, tq):
    b, t, w = sq.shape
    lanes = 2 * SB_DH
    q_spec = pl.BlockSpec((1, tq, lanes), lambda i, p, j: (i, j, p))
    kv_spec = pl.BlockSpec((1, t, lanes), lambda i, p, j: (i, 0, p))
    return pl.pallas_call(
        functools.partial(_sb_prompt_kernel, tq=tq),
        grid=(b, w // lanes, t // tq),
        in_specs=[q_spec, kv_spec, kv_spec],
        out_specs=q_spec,
        out_shape=jax.ShapeDtypeStruct((b, t, w), BF16),
        compiler_params=pltpu.CompilerParams(
            dimension_semantics=("parallel", "parallel", "arbitrary"), vmem_limit_bytes=VMEM_LIMIT),
        name="sb_prompt",
    )(sq, sk, sv)


def _sb_sample_kernel(q_ref, kn_ref, vn_ref, kc_ref, vc_ref, o_ref, *, tl, tk, hg):
    n_tiles = kc_ref.shape[2] // tk
    row = lax.broadcasted_iota(jnp.int32, (tl, tl), 0)
    col = lax.broadcasted_iota(jnp.int32, (tl, tl), 1)
    u_new = _suffix_ones(tl)
    u_past = _suffix_ones(tk)
    outs = []
    for h in range(hg):
        q = q_ref[0, h]
        acc = jnp.zeros((tl, SB_DH), F32)
        carry = jnp.zeros((tl, 1), F32)
        acc, carry = _sb_tile(q, kn_ref[0, h].astype(BF16), vn_ref[0, h].astype(BF16), acc, carry, col < row, u_new)

        def body(j, c, h=h, q=q):
            ks = pl.multiple_of((n_tiles - 1 - j) * tk, tk)
            k = kc_ref[0, h, pl.ds(ks, tk), :].astype(BF16)
            v = vc_ref[0, h, pl.ds(ks, tk), :].astype(BF16)
            return _sb_tile(q, k, v, c[0], c[1], None, u_past)

        acc, carry = lax.fori_loop(0, n_tiles, body, (acc, carry))
        outs.append(acc)
    o_ref[0] = jnp.concatenate(outs, axis=-1).astype(BF16)


def _sb_sample(sqh, sk_new, sv_new, k_cache, v_cache, *, tk, hg):
    b, nh, tl, dh = sqh.shape
    past = k_cache.shape[2]
    new_spec = pl.BlockSpec((1, hg, tl, dh), lambda i, g: (i, g, 0, 0))
    cache_spec = pl.BlockSpec((1, hg, past, dh), lambda i, g: (i, g, 0, 0))
    return pl.pallas_call(
        functools.partial(_sb_sample_kernel, tl=tl, tk=tk, hg=hg),
        grid=(b, nh // hg),
        in_specs=[new_spec, new_spec, new_spec, cache_spec, cache_spec],
        out_specs=pl.BlockSpec((1, tl, hg * dh), lambda i, g: (i, 0, g)),
        out_shape=jax.ShapeDtypeStruct((b, tl, nh * dh), BF16),
        compiler_params=pltpu.CompilerParams(
            dimension_semantics=("parallel", "parallel"), vmem_limit_bytes=VMEM_LIMIT),
        name="sb_sample",
    )(sqh, sk_new, sv_new, k_cache, v_cache)


def _post_kernel(x_ref, ro_ref, so_ref, mk_ref, mv_ref, cpast_ref,
                 wo_ref, ln1g_ref, ln1b_ref, wq_ref, wom_ref, ln2g_ref, ln2b_ref,
                 wup_ref, cw_ref, cb_ref, wdn_ref, ln3g_ref, ln3b_ref,
                 y_ref, cst_ref, prev_scr, *, tm, alpha, fb):
    t = pl.program_id(1)

    @pl.when(t == 0)
    def _():
        prev_scr[...] = cpast_ref[0]

    x = x_ref[0]
    mix = _dot(jnp.concatenate([ro_ref[0], so_ref[0]], axis=-1), wo_ref[...])
    x1 = _ln(alpha * x + mix, ln1g_ref[...], ln1b_ref[...])

    q = _dot(x1.astype(BF16), wq_ref[...])
    dh = mk_ref.shape[3]
    heads = []
    for h in range(MEM_HEADS):
        s = _dot_nt(q[:, h * dh:(h + 1) * dh].astype(BF16), mk_ref[0, h]) * (dh ** -0.5)
        e = jnp.exp(s - jnp.max(s, -1, keepdims=True))
        p = e * (1.0 / jnp.sum(e, -1, keepdims=True))
        heads.append(_dot(p.astype(BF16), mv_ref[0, h]).astype(BF16))
    att = _dot(jnp.concatenate(heads, axis=-1), wom_ref[...])
    x2 = _ln(alpha * x1 + att, ln2g_ref[...], ln2b_ref[...])

    x2b = x2.astype(BF16)
    d_ff = wdn_ref.shape[0]
    rows = lax.broadcasted_iota(jnp.int32, (tm, fb), 0)

    def conv(u, c0):
        p0 = prev_scr[0:1, c0:c0 + fb]
        p1 = prev_scr[1:2, c0:c0 + fb]
        u1 = jnp.where(rows == 0, p1, pltpu.roll(u, 1, axis=0))
        u2 = jnp.where(rows == 0, p0, jnp.where(rows == 1, p1, pltpu.roll(u, 2, axis=0)))
        prev_scr[:, c0:c0 + fb] = u[tm - 2:tm, :]
        return (cb_ref[:, c0:c0 + fb] + cw_ref[0:1, c0:c0 + fb] * u2
                + cw_ref[1:2, c0:c0 + fb] * u1 + cw_ref[2:3, c0:c0 + fb] * u)

    f = jnp.zeros((tm, x.shape[-1]), F32)
    for blk in range(d_ff // fb):
        ca, cg = blk * fb, d_ff + blk * fb
        a = conv(_dot(x2b, wup_ref[:, ca:ca + fb]), ca)
        g = conv(_dot(x2b, wup_ref[:, cg:cg + fb]), cg)
        f = f + _dot((_silu(a) * g).astype(BF16), wdn_ref[ca:ca + fb, :])
    y_ref[0] = _ln(alpha * x2 + f, ln3g_ref[...], ln3b_ref[...])
    cst_ref[0] = prev_scr[...]


def _post(x, ro, so, mk_b, mv_b, conv_past, lw, *, tm, alpha):
    b, t, d = x.shape
    hw = ro.shape[-1]
    two_ff = lw["w_up"].shape[1]
    row = lambda w: pl.BlockSpec((1, tm, w), lambda i, j: (i, j, 0))
    mem_spec = pl.BlockSpec((1,) + mk_b.shape[1:], lambda i, j: (i, 0, 0, 0))
    cst_spec = pl.BlockSpec((1, CONV_W - 1, two_ff), lambda i, j: (i, 0, 0))
    names = ("w_o", "ln1_g", "ln1_b", "w_q_mem", "w_o_mem", "ln2_g", "ln2_b",
             "w_up", "conv_w", "conv_b", "w_down", "ln3_g", "ln3_b")
    weights = [lw[n] for n in names]
    return pl.pallas_call(
        functools.partial(_post_kernel, tm=tm, alpha=alpha, fb=256),
        grid=(b, t // tm),
        in_specs=[row(d), row(hw), row(hw), mem_spec, mem_spec, cst_spec] + [_const_spec(w.shape) for w in weights],
        out_specs=[row(d), cst_spec],
        out_shape=[jax.ShapeDtypeStruct((b, t, d), F32), jax.ShapeDtypeStruct((b, CONV_W - 1, two_ff), F32)],
        scratch_shapes=[pltpu.VMEM((CONV_W - 1, two_ff), F32)],
        compiler_params=pltpu.CompilerParams(
            dimension_semantics=("parallel", "arbitrary"), vmem_limit_bytes=VMEM_LIMIT),
        name="post",
    )(x, ro, so, mk_b, mv_b, conv_past, *weights)


def _rope_tables(pos):
    half = RET_D // 2
    inv = 1.0 / (ROPE_BASE ** (jnp.arange(half, dtype=F32) / half))
    ang = pos.astype(F32)[:, None] * inv[None, :]
    c, s = jnp.cos(ang), jnp.sin(ang)
    return jnp.concatenate([c, c], axis=-1), jnp.concatenate([-s, s], axis=-1)


def kernel(x_prompt, x_sample, cache_sb_k, cache_sb_v, state_ret, state_ffn_conv, cache_mem_k, cache_mem_v, mem_prompt, w_in, w_o, ln1_g, ln1_b, w_q_mem, w_k_mem, w_v_mem, w_o_mem, ln2_g, ln2_b, w_up, conv_w, conv_b, w_down, ln3_g, ln3_b):
    depth = w_in.shape[0]
    alpha = (2.0 * depth) ** 0.25
    xp, xs = x_prompt, x_sample
    bp, tp, _ = xp.shape
    bs, ts, _ = xs.shape
    past_len = cache_sb_k.shape[3]
    two_ff = w_up.shape[2]
    cos_p, sin_p = _rope_tables(jnp.arange(tp, dtype=jnp.int32))
    cos_s, sin_s = _rope_tables(past_len + jnp.arange(ts, dtype=jnp.int32))
    row2 = lambda a: a.reshape(1, -1)
    outs = [[] for _ in range(10)]
    for l in range(depth):
        lw = {"w_o": w_o[l].astype(BF16), "ln1_g": row2(ln1_g[l]), "ln1_b": row2(ln1_b[l]),
              "w_q_mem": w_q_mem[l].astype(BF16), "w_o_mem": w_o_mem[l].astype(BF16),
              "ln2_g": row2(ln2_g[l]), "ln2_b": row2(ln2_b[l]),
              "w_up": w_up[l].astype(BF16), "conv_w": conv_w[l], "conv_b": row2(conv_b[l]),
              "w_down": w_down[l].astype(BF16), "ln3_g": row2(ln3_g[l]), "ln3_b": row2(ln3_b[l])}
        w_in_b = w_in[l].astype(BF16)

        mk, mv, mk_b, mv_b = _memkv(mem_prompt, w_k_mem[l].astype(BF16), w_v_mem[l].astype(BF16))
        ret, sq, skb, svb, pk, pv = _proj(xp, w_in_b, cos_p, sin_p, nb=1, tl=512, head_split_q=False)
        ro, ps = _retention(ret, None, c=256)
        so = _sb_prompt(sq, skb, svb, tq=256)
        xp, pc = _post(xp, ro, so, mk_b, mv_b, jnp.zeros((bp, CONV_W - 1, two_ff), F32), lw, tm=256, alpha=alpha)

        ret, sqh, sk, sv = _proj(xs, w_in_b, cos_s, sin_s, nb=bs, tl=ts, head_split_q=True)
        ro, ss = _retention(ret, state_ret[l], c=min(REF_CHUNK, ts))
        so = _sb_sample(sqh, sk, sv, cache_sb_k[l], cache_sb_v[l], tk=256, hg=4)
        xs, sc = _post(xs, ro, so, cache_mem_k[l].astype(BF16), cache_mem_v[l].astype(BF16),
                       state_ffn_conv[l], lw, tm=ts, alpha=alpha)
        for lst, val in zip(outs, (pk, pv, ps, pc, mk, mv, sk, sv, ss, sc)):
            lst.append(val)
    return (xp, xs) + tuple(jnp.stack(o) for o in outs)
```

```python
import functools
import math

import jax
import jax.numpy as jnp
from jax import lax
from jax.experimental import pallas as pl
from jax.experimental.pallas import tpu as pltpu

F32 = jnp.float32
BF16 = jnp.bfloat16

RET_HEADS = 4
RET_D = 128
SB_HEADS = 8
SB_DH = 64
MEM_HEADS = 4
CONV_W = 3
ROPE_BASE = 10000.0
LN_EPS = 1e-5
RMS_EPS = 1e-6
REF_CHUNK = 64

VMEM_LIMIT = 56 * 1024 * 1024


def _const_spec(shape):
    nd = len(shape)
    return pl.BlockSpec(shape, lambda *_: (0,) * nd, pipeline_mode=pl.Buffered(1))


def _ln(x, g, b):
    mu = jnp.mean(x, -1, keepdims=True)
    xc = x - mu
    var = jnp.mean(xc * xc, -1, keepdims=True)
    return xc * lax.rsqrt(var + LN_EPS) * g + b


def _silu(x):
    return x * (1.0 / (1.0 + jnp.exp(-x)))


def _dot(a, b):
    return jnp.dot(a, b, preferred_element_type=F32)


def _dot_nt(a, b):
    return lax.dot_general(a, b, (((1,), (1,)), ((), ())), preferred_element_type=F32)


def _proj_kernel(x_ref, w_ref, cos_ref, sin_ref, *out_refs, nb, tl, d_model, head_split_q):
    if head_split_q:
        ret_ref, sqh_ref, sk_ref, sv_ref = out_refs
    else:
        ret_ref, sq_ref, skb_ref, svb_ref, sk_ref, sv_ref = out_refs
    m = nb * tl
    xb = x_ref[...].reshape(m, d_model).astype(BF16)
    cos = jnp.broadcast_to(cos_ref[...][None], (nb, tl, RET_D)).reshape(m, RET_D)
    sin = jnp.broadcast_to(sin_ref[...][None], (nb, tl, RET_D)).reshape(m, RET_D)
    hw = RET_HEADS * RET_D

    def seg(i):
        return _dot(xb, w_ref[:, i * hw:(i + 1) * hw])

    def rope(r, scale):
        outs = []
        for h in range(RET_HEADS):
            xh = r[:, h * RET_D:(h + 1) * RET_D]
            o = xh * cos + pltpu.roll(xh, RET_D // 2, axis=1) * sin
            outs.append(o * scale if scale != 1.0 else o)
        return jnp.concatenate(outs, axis=-1)

    def put_ret(i, val):
        ret_ref[:, :, i * hw:(i + 1) * hw] = val.astype(BF16).reshape(nb, tl, hw)

    put_ret(0, rope(seg(0), 1.0))
    put_ret(1, rope(seg(1), RET_D ** -0.5))
    put_ret(2, seg(2))
    put_ret(3, seg(3))

    def put_heads(ref, val):
        for h in range(SB_HEADS):
            ref[:, h, :, :] = val[:, h * SB_DH:(h + 1) * SB_DH].astype(ref.dtype).reshape(nb, tl, SB_DH)

    sq = seg(4) * (SB_DH ** -0.5)
    sk = seg(5)
    sv = seg(6)
    if head_split_q:
        put_heads(sqh_ref, sq)
        put_heads(sk_ref, sk)
        put_heads(sv_ref, sv)
    else:
        sq_ref[...] = sq.astype(BF16).reshape(nb, tl, hw)
        skb_ref[...] = sk.astype(BF16).reshape(nb, tl, hw)
        svb_ref[...] = sv.astype(BF16).reshape(nb, tl, hw)
        sk_ref[0] = sk.T.reshape(SB_HEADS, SB_DH, tl)
        sv_ref[0] = sv.T.reshape(SB_HEADS, SB_DH, tl)


def _proj(x, w_in_b, cos, sin, *, nb, tl, head_split_q):
    b, t, d = x.shape
    hw = RET_HEADS * RET_D
    grid = (b // nb, t // tl)
    hs = lambda dt: jax.ShapeDtypeStruct((b, SB_HEADS, t, SB_DH), dt)
    hs_spec = pl.BlockSpec((nb, SB_HEADS, tl, SB_DH), lambda i, j: (i, 0, j, 0))
    row_spec = lambda w: pl.BlockSpec((nb, tl, w), lambda i, j: (i, j, 0))
    out_shape = [jax.ShapeDtypeStruct((b, t, 4 * hw), BF16)]
    out_specs = [row_spec(4 * hw)]
    if head_split_q:
        out_shape += [hs(BF16), hs(F32), hs(F32)]
        out_specs += [hs_spec] * 3
    else:
        assert nb == 1
        out_shape += [jax.ShapeDtypeStruct((b, t, hw), BF16)] * 3
        out_specs += [row_spec(hw)] * 3
        out_shape += [jax.ShapeDtypeStruct((b, SB_HEADS, SB_DH, t), F32)] * 2
        out_specs += [pl.BlockSpec((1, SB_HEADS, SB_DH, tl), lambda i, j: (i, 0, 0, j))] * 2
    return pl.pallas_call(
        functools.partial(_proj_kernel, nb=nb, tl=tl, d_model=d, head_split_q=head_split_q),
        grid=grid,
        in_specs=[row_spec(d), _const_spec(w_in_b.shape),
                  pl.BlockSpec((tl, RET_D), lambda i, j: (j, 0)),
                  pl.BlockSpec((tl, RET_D), lambda i, j: (j, 0))],
        out_specs=out_specs, out_shape=out_shape,
        compiler_params=pltpu.CompilerParams(
            dimension_semantics=("parallel", "parallel"), vmem_limit_bytes=VMEM_LIMIT),
        name="proj",
    )(x, w_in_b, cos, sin)


def _memkv_kernel(m_ref, wk_ref, wv_ref, mk_ref, mv_ref, mkb_ref, mvb_ref, *, dh):
    mb = m_ref[0].astype(BF16)
    for w_ref, o_ref, ob_ref in ((wk_ref, mk_ref, mkb_ref), (wv_ref, mv_ref, mvb_ref)):
        r = _dot(mb, w_ref[...])
        for h in range(MEM_HEADS):
            rh = r[:, h * dh:(h + 1) * dh]
            o_ref[0, h] = rh
            ob_ref[0, h] = rh.astype(BF16)


def _memkv(mem, wk_b, wv_b):
    b, n, d = mem.shape
    dh = d // MEM_HEADS
    spec = pl.BlockSpec((1, MEM_HEADS, n, dh), lambda i: (i, 0, 0, 0))
    return pl.pallas_call(
        functools.partial(_memkv_kernel, dh=dh),
        grid=(b,),
        in_specs=[pl.BlockSpec((1, n, d), lambda i: (i, 0, 0)), _const_spec(wk_b.shape), _const_spec(wv_b.shape)],
        out_specs=[spec] * 4,
        out_shape=[jax.ShapeDtypeStruct((b, MEM_HEADS, n, dh), F32)] * 2
        + [jax.ShapeDtypeStruct((b, MEM_HEADS, n, dh), BF16)] * 2,
        compiler_params=pltpu.CompilerParams(dimension_semantics=("parallel",), vmem_limit_bytes=VMEM_LIMIT),
        name="memkv",
    )(mem, wk_b, wv_b)


def _ret_kernel(*refs, c, has_s0):
    if has_s0:
        ret_ref, s0_ref, ro_ref, sf_ref, s_scr = refs
    else:
        ret_ref, ro_ref, sf_ref, s_scr = refs
    t = pl.program_id(1)

    @pl.when(t == 0)
    def _():
        s_scr[...] = s0_ref[0] if has_s0 else jnp.zeros_like(s_scr)

    hw = RET_HEADS * RET_D
    row = lax.broadcasted_iota(jnp.int32, (c, c), 0)
    col = lax.broadcasted_iota(jnp.int32, (c, c), 1)
    diff = (row - col).astype(F32)
    idx = lax.broadcasted_iota(jnp.int32, (c, 1), 0).astype(F32)
    for h in range(RET_HEADS):
        log_g = math.log1p(-(2.0 ** (-5.0 - h)))
        sl = lambda i: ret_ref[0, :, i * hw + h * RET_D:i * hw + (h + 1) * RET_D]
        q, k, v, g = sl(0), sl(1), sl(2), sl(3)
        dmask = jnp.where(diff >= 0, jnp.exp(log_g * jnp.maximum(diff, 0.0)), 0.0)
        scores = _dot_nt(q, k) * dmask
        o = _dot(scores.astype(BF16), v)
        s_prev = s_scr[h]
        qd = (q.astype(F32) * jnp.exp(log_g * (idx + 1.0))).astype(BF16)
        o = o + _dot(qd, s_prev.astype(BF16))
        kd = (k.astype(F32) * jnp.exp(log_g * (c - 1.0 - idx))).astype(BF16)
        s_scr[h] = math.exp(log_g * c) * s_prev + _dot(kd.T, v)
        o = o * lax.rsqrt(jnp.mean(o * o, -1, keepdims=True) + RMS_EPS)
        ro_ref[0, :, h * RET_D:(h + 1) * RET_D] = (o * _silu(g.astype(F32))).astype(BF16)

    @pl.when(t == pl.num_programs(1) - 1)
    def _():
        sf_ref[0] = s_scr[...]


def _retention(ret, s0, *, c):
    b, t, w = ret.shape
    hw = RET_HEADS * RET_D
    st_spec = pl.BlockSpec((1, RET_HEADS, RET_D, RET_D), lambda i, j: (i, 0, 0, 0))
    in_specs = [pl.BlockSpec((1, c, w), lambda i, j: (i, j, 0))]
    args = [ret]
    if s0 is not None:
        in_specs.append(st_spec)
        args.append(s0)
    return pl.pallas_call(
        functools.partial(_ret_kernel, c=c, has_s0=s0 is not None),
        grid=(b, t // c),
        in_specs=in_specs,
        out_specs=[pl.BlockSpec((1, c, hw), lambda i, j: (i, j, 0)), st_spec],
        out_shape=[jax.ShapeDtypeStruct((b, t, hw), BF16),
                   jax.ShapeDtypeStruct((b, RET_HEADS, RET_D, RET_D), F32)],
        scratch_shapes=[pltpu.VMEM((RET_HEADS, RET_D, RET_D), F32)],
        compiler_params=pltpu.CompilerParams(
            dimension_semantics=("parallel", "arbitrary"), vmem_limit_bytes=VMEM_LIMIT),
        name="retention",
    )(*args)


SB_SKIP_BELOW = -105.0


def _suffix_ones(n):
    j = lax.broadcasted_iota(jnp.int32, (n, n), 0)
    s = lax.broadcasted_iota(jnp.int32, (n, n), 1)
    return jnp.where(j >= s, 1.0, 0.0).astype(BF16)


def _sb_weights(z, carry, valid, u):
    lf = -(jnp.maximum(z, 0.0) + jnp.log(1.0 + jnp.exp(-jnp.abs(z))))
    if valid is not None:
        lf = jnp.where(valid, lf, 0.0)
    hi = lf.astype(BF16)
    lo = (lf - hi.astype(F32)).astype(BF16)
    cs = _dot(hi, u) + _dot(lo, u)
    a = jnp.exp(z + cs + carry)
    if valid is not None:
        a = jnp.where(valid, a, 0.0)
    return a.astype(BF16), carry + cs[:, 0:1]


def _sb_live(carry):
    return (jnp.max(carry) > SB_SKIP_BELOW).astype(jnp.int32)


def _sb_prompt_kernel(q_ref, k_ref, v_ref, o_ref, *, tq):
    qi = pl.program_id(2)
    q = q_ref[0]
    lanes = 2 * SB_DH
    lo_half = lax.broadcasted_iota(jnp.int32, (1, lanes), 1) < SB_DH
    zero = jnp.zeros_like(q)
    q2 = jnp.concatenate([jnp.where(lo_half, q, zero), jnp.where(lo_half, zero, q)], axis=0)
    row = lax.broadcasted_iota(jnp.int32, (2 * tq, tq), 0) & (tq - 1)
    col = lax.broadcasted_iota(jnp.int32, (2 * tq, tq), 1)
    u = _suffix_ones(tq)

    def tile(kt, acc, carry, valid):
        ks = pl.multiple_of(kt * tq, tq)
        a, carry = _sb_weights(_dot_nt(q2, k_ref[0, pl.ds(ks, tq), :]), carry, valid, u)
        return acc + _dot(a, v_ref[0, pl.ds(ks, tq), :]), carry

    acc = jnp.zeros((2 * tq, lanes), F32)
    carry = jnp.zeros((2 * tq, 1), F32)
    acc, carry = tile(qi, acc, carry, col < row)

    def cond(c):
        return jnp.logical_and(c[0] < qi, c[3] > 0)

    def body(c):
        acc, carry = tile(qi - 1 - c[0], c[1], c[2], None)
        return c[0] + 1, acc, carry, _sb_live(carry)

    _, acc, _, _ = lax.while_loop(cond, body, (jnp.int32(0), acc, carry, _sb_live(carry)))
    o_ref[0] = jnp.where(lo_half, acc[:tq], acc[tq:]).astype(BF16)


def _sb_prompt(sq, sk, sv, *, tq):
    b, t, w = sq.shape
    lanes = 2 * SB_DH
    q_spec = pl.BlockSpec((1, tq, lanes), lambda i, p, j: (i, j, p))
    kv_spec = pl.BlockSpec((1, t, lanes), lambda i, p, j: (i, 0, p))
    return pl.pallas_call(
        functools.partial(_sb_prompt_kernel, tq=tq),
        grid=(b, w // lanes, t // tq),
        in_specs=[q_spec, kv_spec, kv_spec],
        out_specs=q_spec,
        out_shape=jax.ShapeDtypeStruct((b, t, w), BF16),
        compiler_params=pltpu.CompilerParams(
            dimension_semantics=("parallel", "parallel", "arbitrary"), vmem_limit_bytes=VMEM_LIMIT),
        name="sb_prompt",
    )(sq, sk, sv)


def _sb_sample_kernel(q_ref, kn_ref, vn_ref, kct_ref, vct_ref, o_ref, acc_scr, carry_scr, *, tl, tk):
    nh = q_ref.shape[1]
    n_tiles = kct_ref.shape[3] // tk
    m = nh * tl

    def per_head(fn):
        return jnp.concatenate([fn(h) for h in range(nh)], axis=0)

    row = lax.broadcasted_iota(jnp.int32, (m, tl), 0) & (tl - 1)
    col = lax.broadcasted_iota(jnp.int32, (m, tl), 1)
    z = per_head(lambda h: _dot_nt(q_ref[0, h], kn_ref[0, h].astype(BF16)))
    a, carry = _sb_weights(z, jnp.zeros((m, 1), F32), col < row, _suffix_ones(tl))
    acc_scr[...] = per_head(lambda h: _dot(a[h * tl:(h + 1) * tl], vn_ref[0, h].astype(BF16)))
    carry_scr[...] = carry
    u = _suffix_ones(tk)
    for j in reversed(range(n_tiles)):
        @pl.when(jnp.max(carry_scr[...]) > SB_SKIP_BELOW)
        def _(j=j):
            cols = slice(j * tk, (j + 1) * tk)
            z = per_head(lambda h: _dot(q_ref[0, h], kct_ref[0, h, :, cols].astype(BF16)))
            a, carry = _sb_weights(z, carry_scr[...], None, u)
            acc_scr[...] += per_head(
                lambda h: _dot_nt(a[h * tl:(h + 1) * tl], vct_ref[0, h, :, cols].astype(BF16)))
            carry_scr[...] = carry

    acc = acc_scr[...]
    o_ref[0] = jnp.concatenate([acc[h * tl:(h + 1) * tl] for h in range(nh)], axis=-1).astype(BF16)


def _sb_sample(sqh, sk_new, sv_new, k_cache_t, v_cache_t, *, tk):
    b, nh, tl, dh = sqh.shape
    past = k_cache_t.shape[3]
    new_spec = pl.BlockSpec((1, nh, tl, dh), lambda i: (i, 0, 0, 0))
    cache_spec = pl.BlockSpec((1, nh, dh, past), lambda i: (i, 0, 0, 0))
    return pl.pallas_call(
        functools.partial(_sb_sample_kernel, tl=tl, tk=tk),
        grid=(b,),
        in_specs=[new_spec, new_spec, new_spec, cache_spec, cache_spec],
        out_specs=pl.BlockSpec((1, tl, nh * dh), lambda i: (i, 0, 0)),
        out_shape=jax.ShapeDtypeStruct((b, tl, nh * dh), BF16),
        scratch_shapes=[pltpu.VMEM((nh * tl, dh), F32), pltpu.VMEM((nh * tl, 1), F32)],
        compiler_params=pltpu.CompilerParams(dimension_semantics=("parallel",), vmem_limit_bytes=VMEM_LIMIT),
        name="sb_sample",
    )(sqh, sk_new, sv_new, k_cache_t, v_cache_t)


def _post_kernel(x_ref, ro_ref, so_ref, mk_ref, mv_ref, cpast_ref,
                 wo_ref, ln1g_ref, ln1b_ref, wq_ref, wom_ref, ln2g_ref, ln2b_ref,
                 wup_ref, cw_ref, cb_ref, wdn_ref, ln3g_ref, ln3b_ref,
                 y_ref, cst_ref, prev_scr, *, nb, tl, alpha, fb):
    t = pl.program_id(1)
    m = nb * tl

    @pl.when(t == 0)
    def _():
        prev_scr[...] = cpast_ref[...]

    d = x_ref.shape[-1]
    hw = ro_ref.shape[-1]
    x = x_ref[...].reshape(m, d)
    mix_in = jnp.concatenate([ro_ref[...].reshape(m, hw), so_ref[...].reshape(m, hw)], axis=-1)
    x1 = _ln(alpha * x + _dot(mix_in, wo_ref[...]), ln1g_ref[...], ln1b_ref[...])

    q = _dot(x1.astype(BF16), wq_ref[...])
    dh = mk_ref.shape[3]
    streams = []
    for b in range(nb):
        heads = []
        for h in range(MEM_HEADS):
            qh = q[b * tl:(b + 1) * tl, h * dh:(h + 1) * dh].astype(BF16)
            s = _dot_nt(qh, mk_ref[b, h]) * (dh ** -0.5)
            e = jnp.exp(s - jnp.max(s, -1, keepdims=True))
            p = e * (1.0 / jnp.sum(e, -1, keepdims=True))
            heads.append(_dot(p.astype(BF16), mv_ref[b, h]).astype(BF16))
        streams.append(jnp.concatenate(heads, axis=-1))
    att_in = streams[0] if nb == 1 else jnp.concatenate(streams, axis=0)
    x2 = _ln(alpha * x1 + _dot(att_in, wom_ref[...]), ln2g_ref[...], ln2b_ref[...])

    x2b = x2.astype(BF16)
    d_ff = wdn_ref.shape[0]
    rows = lax.broadcasted_iota(jnp.int32, (m, fb), 0) & (tl - 1)

    def conv(u, c0):
        prev = prev_scr[:, :, c0:c0 + fb]
        p0 = jnp.broadcast_to(prev[:, 0:1, :], (nb, tl, fb)).reshape(m, fb)
        p1 = jnp.broadcast_to(prev[:, 1:2, :], (nb, tl, fb)).reshape(m, fb)
        u1 = jnp.where(rows == 0, p1, pltpu.roll(u, 1, axis=0))
        u2 = jnp.where(rows == 0, p0, jnp.where(rows == 1, p1, pltpu.roll(u, 2, axis=0)))
        prev_scr[:, :, c0:c0 + fb] = u.reshape(nb, tl, fb)[:, tl - 2:tl, :]
        return (cb_ref[:, c0:c0 + fb] + cw_ref[0:1, c0:c0 + fb] * u2
                + cw_ref[1:2, c0:c0 + fb] * u1 + cw_ref[2:3, c0:c0 + fb] * u)

    f = jnp.zeros((m, d), F32)
    for blk in range(d_ff // fb):
        ca, cg = blk * fb, d_ff + blk * fb
        a = conv(_dot(x2b, wup_ref[:, ca:ca + fb]), ca)
        g = conv(_dot(x2b, wup_ref[:, cg:cg + fb]), cg)
        f = f + _dot((_silu(a) * g).astype(BF16), wdn_ref[ca:ca + fb, :])
    y_ref[...] = _ln(alpha * x2 + f, ln3g_ref[...], ln3b_ref[...]).reshape(nb, tl, d)
    cst_ref[...] = prev_scr[...]


def _post(x, ro, so, mk_b, mv_b, conv_past, lw, *, nb, tl, alpha):
    b, t, d = x.shape
    assert tl & (tl - 1) == 0 and (nb == 1 or tl == t)
    hw = ro.shape[-1]
    two_ff = lw["w_up"].shape[1]
    row = lambda w: pl.BlockSpec((nb, tl, w), lambda i, j: (i, j, 0))
    mem_spec = pl.BlockSpec((nb,) + mk_b.shape[1:], lambda i, j: (i, 0, 0, 0))
    cst_spec = pl.BlockSpec((nb, CONV_W - 1, two_ff), lambda i, j: (i, 0, 0))
    names = ("w_o", "ln1_g", "ln1_b", "w_q_mem", "w_o_mem", "ln2_g", "ln2_b",
             "w_up", "conv_w", "conv_b", "w_down", "ln3_g", "ln3_b")
    weights = [lw[n] for n in names]
    return pl.pallas_call(
        functools.partial(_post_kernel, nb=nb, tl=tl, alpha=alpha, fb=256),
        grid=(b // nb, t // tl),
        in_specs=[row(d), row(hw), row(hw), mem_spec, mem_spec, cst_spec] + [_const_spec(w.shape) for w in weights],
        out_specs=[row(d), cst_spec],
        out_shape=[jax.ShapeDtypeStruct((b, t, d), F32), jax.ShapeDtypeStruct((b, CONV_W - 1, two_ff), F32)],
        scratch_shapes=[pltpu.VMEM((nb, CONV_W - 1, two_ff), F32)],
        compiler_params=pltpu.CompilerParams(
            dimension_semantics=("parallel", "arbitrary"), vmem_limit_bytes=VMEM_LIMIT),
        name="post",
    )(x, ro, so, mk_b, mv_b, conv_past, *weights)


def _rope_tables(pos):
    half = RET_D // 2
    inv = 1.0 / (ROPE_BASE ** (jnp.arange(half, dtype=F32) / half))
    ang = pos.astype(F32)[:, None] * inv[None, :]
    c, s = jnp.cos(ang), jnp.sin(ang)
    return jnp.concatenate([c, c], axis=-1), jnp.concatenate([-s, s], axis=-1)


def kernel(x_prompt, x_sample, cache_sb_k, cache_sb_v, state_ret, state_ffn_conv, cache_mem_k, cache_mem_v, mem_prompt, w_in, w_o, ln1_g, ln1_b, w_q_mem, w_k_mem, w_v_mem, w_o_mem, ln2_g, ln2_b, w_up, conv_w, conv_b, w_down, ln3_g, ln3_b):
    depth = w_in.shape[0]
    alpha = (2.0 * depth) ** 0.25
    xp, xs = x_prompt, x_sample
    bp, tp, _ = xp.shape
    bs, ts, _ = xs.shape
    past_len = cache_sb_k.shape[3]
    two_ff = w_up.shape[2]
    cos_p, sin_p = _rope_tables(jnp.arange(tp, dtype=jnp.int32))
    cos_s, sin_s = _rope_tables(past_len + jnp.arange(ts, dtype=jnp.int32))
    row2 = lambda a: a.reshape(1, -1)
    swap = lambda a: jnp.swapaxes(a, -1, -2)
    outs = [[] for _ in range(10)]
    for l in range(depth):
        lw = {"w_o": w_o[l].astype(BF16), "ln1_g": row2(ln1_g[l]), "ln1_b": row2(ln1_b[l]),
              "w_q_mem": w_q_mem[l].astype(BF16), "w_o_mem": w_o_mem[l].astype(BF16),
              "ln2_g": row2(ln2_g[l]), "ln2_b": row2(ln2_b[l]),
              "w_up": w_up[l].astype(BF16), "conv_w": conv_w[l], "conv_b": row2(conv_b[l]),
              "w_down": w_down[l].astype(BF16), "ln3_g": row2(ln3_g[l]), "ln3_b": row2(ln3_b[l])}
        w_in_b = w_in[l].astype(BF16)

        mk, mv, mk_b, mv_b = _memkv(mem_prompt, w_k_mem[l].astype(BF16), w_v_mem[l].astype(BF16))
        ret, sq, skb, svb, pk_t, pv_t = _proj(xp, w_in_b, cos_p, sin_p, nb=1, tl=512, head_split_q=False)
        ro, ps = _retention(ret, None, c=256)
        so = _sb_prompt(sq, skb, svb, tq=256)
        xp, pc = _post(xp, ro, so, mk_b, mv_b, jnp.zeros((bp, CONV_W - 1, two_ff), F32), lw,
                       nb=1, tl=512, alpha=alpha)

        ret, sqh, sk, sv = _proj(xs, w_in_b, cos_s, sin_s, nb=bs, tl=ts, head_split_q=True)
        ro, ss = _retention(ret, state_ret[l], c=min(REF_CHUNK, ts))
        so = _sb_sample(sqh, sk, sv, swap(cache_sb_k[l]), swap(cache_sb_v[l]), tk=256)
        xs, sc = _post(xs, ro, so, cache_mem_k[l].astype(BF16), cache_mem_v[l].astype(BF16),
                       state_ffn_conv[l], lw, nb=min(bs, 8), tl=ts, alpha=alpha)
        for lst, val in zip(outs, (swap(pk_t), swap(pv_t), ps, pc, mk, mv, sk, sv, ss, sc)):
            lst.append(val)
    return (xp, xs) + tuple(jnp.stack(o) for o in outs)
```

```python
import functools
import math

import jax
import jax.numpy as jnp
from jax import lax
from jax.experimental import pallas as pl
from jax.experimental.pallas import tpu as pltpu

F32 = jnp.float32
BF16 = jnp.bfloat16

RET_HEADS = 4
RET_D = 128
SB_HEADS = 8
SB_DH = 64
MEM_HEADS = 4
CONV_W = 3
ROPE_BASE = 10000.0
LN_EPS = 1e-5
RMS_EPS = 1e-6
REF_CHUNK = 64

VMEM_LIMIT = 56 * 1024 * 1024


def _const_spec(shape):
    nd = len(shape)
    return pl.BlockSpec(shape, lambda *_: (0,) * nd, pipeline_mode=pl.Buffered(1))


def _ln(x, g, b):
    mu = jnp.mean(x, -1, keepdims=True)
    xc = x - mu
    var = jnp.mean(xc * xc, -1, keepdims=True)
    return xc * lax.rsqrt(var + LN_EPS) * g + b


def _silu(x):
    return x * (1.0 / (1.0 + jnp.exp(-x)))


def _dot(a, b):
    return jnp.dot(a, b, preferred_element_type=F32)


def _dot_nt(a, b):
    return lax.dot_general(a, b, (((1,), (1,)), ((), ())), preferred_element_type=F32)


def _proj_kernel(x_ref, w_ref, cos_ref, sin_ref, *out_refs, nb, tl, d_model, head_split_q):
    if head_split_q:
        ret_ref, sqh_ref, sk_ref, sv_ref = out_refs
    else:
        ret_ref, sq_ref, skb_ref, svb_ref, sk_ref, sv_ref = out_refs
    m = nb * tl
    xb = x_ref[...].reshape(m, d_model).astype(BF16)
    cos = jnp.broadcast_to(cos_ref[...][None], (nb, tl, RET_D)).reshape(m, RET_D)
    sin = jnp.broadcast_to(sin_ref[...][None], (nb, tl, RET_D)).reshape(m, RET_D)
    hw = RET_HEADS * RET_D

    def seg(i):
        return _dot(xb, w_ref[:, i * hw:(i + 1) * hw])

    def rope(r, scale):
        outs = []
        for h in range(RET_HEADS):
            xh = r[:, h * RET_D:(h + 1) * RET_D]
            o = xh * cos + pltpu.roll(xh, RET_D // 2, axis=1) * sin
            outs.append(o * scale if scale != 1.0 else o)
        return jnp.concatenate(outs, axis=-1)

    def put_ret(i, val):
        ret_ref[:, :, i * hw:(i + 1) * hw] = val.astype(BF16).reshape(nb, tl, hw)

    put_ret(0, rope(seg(0), 1.0))
    put_ret(1, rope(seg(1), RET_D ** -0.5))
    put_ret(2, seg(2))
    put_ret(3, seg(3))

    def put_heads(ref, val):
        for h in range(SB_HEADS):
            ref[:, h, :, :] = val[:, h * SB_DH:(h + 1) * SB_DH].astype(ref.dtype).reshape(nb, tl, SB_DH)

    sq = seg(4) * (SB_DH ** -0.5)
    sk = seg(5)
    sv = seg(6)
    if head_split_q:
        put_heads(sqh_ref, sq)
        put_heads(sk_ref, sk)
        put_heads(sv_ref, sv)
    else:
        sq_ref[...] = sq.astype(BF16).reshape(nb, tl, hw)
        skb_ref[...] = sk.astype(BF16).reshape(nb, tl, hw)
        svb_ref[...] = sv.astype(BF16).reshape(nb, tl, hw)
        sk_ref[0] = sk.T.reshape(SB_HEADS, SB_DH, tl)
        sv_ref[0] = sv.T.reshape(SB_HEADS, SB_DH, tl)


def _proj(x, w_in_b, cos, sin, *, nb, tl, head_split_q):
    b, t, d = x.shape
    hw = RET_HEADS * RET_D
    grid = (b // nb, t // tl)
    hs = lambda dt: jax.ShapeDtypeStruct((b, SB_HEADS, t, SB_DH), dt)
    hs_spec = pl.BlockSpec((nb, SB_HEADS, tl, SB_DH), lambda i, j: (i, 0, j, 0))
    row_spec = lambda w: pl.BlockSpec((nb, tl, w), lambda i, j: (i, j, 0))
    out_shape = [jax.ShapeDtypeStruct((b, t, 4 * hw), BF16)]
    out_specs = [row_spec(4 * hw)]
    if head_split_q:
        out_shape += [hs(BF16), hs(F32), hs(F32)]
        out_specs += [hs_spec] * 3
    else:
        assert nb == 1
        out_shape += [jax.ShapeDtypeStruct((b, t, hw), BF16)] * 3
        out_specs += [row_spec(hw)] * 3
        out_shape += [jax.ShapeDtypeStruct((b, SB_HEADS, SB_DH, t), F32)] * 2
        out_specs += [pl.BlockSpec((1, SB_HEADS, SB_DH, tl), lambda i, j: (i, 0, 0, j))] * 2
    return pl.pallas_call(
        functools.partial(_proj_kernel, nb=nb, tl=tl, d_model=d, head_split_q=head_split_q),
        grid=grid,
        in_specs=[row_spec(d), _const_spec(w_in_b.shape),
                  pl.BlockSpec((tl, RET_D), lambda i, j: (j, 0)),
                  pl.BlockSpec((tl, RET_D), lambda i, j: (j, 0))],
        out_specs=out_specs, out_shape=out_shape,
        compiler_params=pltpu.CompilerParams(
            dimension_semantics=("parallel", "parallel"), vmem_limit_bytes=VMEM_LIMIT),
        name="proj",
    )(x, w_in_b, cos, sin)


def _memkv_kernel(m_ref, wk_ref, wv_ref, mk_ref, mv_ref, mkb_ref, mvb_ref, *, dh):
    mb = m_ref[0].astype(BF16)
    for w_ref, o_ref, ob_ref in ((wk_ref, mk_ref, mkb_ref), (wv_ref, mv_ref, mvb_ref)):
        r = _dot(mb, w_ref[...])
        for h in range(MEM_HEADS):
            rh = r[:, h * dh:(h + 1) * dh]
            o_ref[0, h] = rh
            ob_ref[0, h] = rh.astype(BF16)


def _memkv(mem, wk_b, wv_b):
    b, n, d = mem.shape
    dh = d // MEM_HEADS
    spec = pl.BlockSpec((1, MEM_HEADS, n, dh), lambda i: (i, 0, 0, 0))
    return pl.pallas_call(
        functools.partial(_memkv_kernel, dh=dh),
        grid=(b,),
        in_specs=[pl.BlockSpec((1, n, d), lambda i: (i, 0, 0)), _const_spec(wk_b.shape), _const_spec(wv_b.shape)],
        out_specs=[spec] * 4,
        out_shape=[jax.ShapeDtypeStruct((b, MEM_HEADS, n, dh), F32)] * 2
        + [jax.ShapeDtypeStruct((b, MEM_HEADS, n, dh), BF16)] * 2,
        compiler_params=pltpu.CompilerParams(dimension_semantics=("parallel",), vmem_limit_bytes=VMEM_LIMIT),
        name="memkv",
    )(mem, wk_b, wv_b)


def _ret_kernel(*refs, c, has_s0):
    if has_s0:
        ret_ref, s0_ref, ro_ref, sf_ref, s_scr = refs
    else:
        ret_ref, ro_ref, sf_ref, s_scr = refs
    t = pl.program_id(1)

    @pl.when(t == 0)
    def _():
        s_scr[...] = s0_ref[0] if has_s0 else jnp.zeros_like(s_scr)

    hw = RET_HEADS * RET_D
    row = lax.broadcasted_iota(jnp.int32, (c, c), 0)
    col = lax.broadcasted_iota(jnp.int32, (c, c), 1)
    diff = (row - col).astype(F32)
    idx = lax.broadcasted_iota(jnp.int32, (c, 1), 0).astype(F32)
    for h in range(RET_HEADS):
        log_g = math.log1p(-(2.0 ** (-5.0 - h)))
        sl = lambda i: ret_ref[0, :, i * hw + h * RET_D:i * hw + (h + 1) * RET_D]
        q, k, v, g = sl(0), sl(1), sl(2), sl(3)
        dmask = jnp.where(diff >= 0, jnp.exp(log_g * jnp.maximum(diff, 0.0)), 0.0)
        scores = _dot_nt(q, k) * dmask
        o = _dot(scores.astype(BF16), v)
        s_prev = s_scr[h]
        qd = (q.astype(F32) * jnp.exp(log_g * (idx + 1.0))).astype(BF16)
        o = o + _dot(qd, s_prev.astype(BF16))
        kd = (k.astype(F32) * jnp.exp(log_g * (c - 1.0 - idx))).astype(BF16)
        s_scr[h] = math.exp(log_g * c) * s_prev + _dot(kd.T, v)
        o = o * lax.rsqrt(jnp.mean(o * o, -1, keepdims=True) + RMS_EPS)
        ro_ref[0, :, h * RET_D:(h + 1) * RET_D] = (o * _silu(g.astype(F32))).astype(BF16)

    @pl.when(t == pl.num_programs(1) - 1)
    def _():
        sf_ref[0] = s_scr[...]


def _retention(ret, s0, *, c):
    b, t, w = ret.shape
    hw = RET_HEADS * RET_D
    st_spec = pl.BlockSpec((1, RET_HEADS, RET_D, RET_D), lambda i, j: (i, 0, 0, 0))
    in_specs = [pl.BlockSpec((1, c, w), lambda i, j: (i, j, 0))]
    args = [ret]
    if s0 is not None:
        in_specs.append(st_spec)
        args.append(s0)
    return pl.pallas_call(
        functools.partial(_ret_kernel, c=c, has_s0=s0 is not None),
        grid=(b, t // c),
        in_specs=in_specs,
        out_specs=[pl.BlockSpec((1, c, hw), lambda i, j: (i, j, 0)), st_spec],
        out_shape=[jax.ShapeDtypeStruct((b, t, hw), BF16),
                   jax.ShapeDtypeStruct((b, RET_HEADS, RET_D, RET_D), F32)],
        scratch_shapes=[pltpu.VMEM((RET_HEADS, RET_D, RET_D), F32)],
        compiler_params=pltpu.CompilerParams(
            dimension_semantics=("parallel", "arbitrary"), vmem_limit_bytes=VMEM_LIMIT),
        name="retention",
    )(*args)


SB_SKIP_BELOW = -105.0


def _suffix_ones(n):
    j = lax.broadcasted_iota(jnp.int32, (n, n), 0)
    s = lax.broadcasted_iota(jnp.int32, (n, n), 1)
    return jnp.where(j >= s, 1.0, 0.0).astype(BF16)


def _sb_weights(z, carry, valid, u):
    lf = -(jnp.maximum(z, 0.0) + jnp.log(1.0 + jnp.exp(-jnp.abs(z))))
    if valid is not None:
        lf = jnp.where(valid, lf, 0.0)
    hi = lf.astype(BF16)
    lo = (lf - hi.astype(F32)).astype(BF16)
    cs = _dot(hi, u) + _dot(lo, u)
    a = jnp.exp(z + cs + carry)
    if valid is not None:
        a = jnp.where(valid, a, 0.0)
    return a.astype(BF16), carry + cs[:, 0:1]


def _sb_live(carry):
    return (jnp.max(carry) > SB_SKIP_BELOW).astype(jnp.int32)


def _sb_prompt_kernel(q_ref, k_ref, v_ref, o_ref, *, tq):
    qi = pl.program_id(1)
    lanes = 2 * SB_DH
    n_pairs = q_ref.shape[-1] // lanes
    pair = lambda p: slice(p * lanes, (p + 1) * lanes)
    lo_half = lax.broadcasted_iota(jnp.int32, (1, lanes), 1) < SB_DH
    q2 = []
    for p in range(n_pairs):
        q = q_ref[0, :, pair(p)]
        zero = jnp.zeros_like(q)
        q2.append(jnp.concatenate([jnp.where(lo_half, q, zero), jnp.where(lo_half, zero, q)], axis=0))
    m = 2 * n_pairs * tq
    row = lax.broadcasted_iota(jnp.int32, (m, tq), 0) & (tq - 1)
    col = lax.broadcasted_iota(jnp.int32, (m, tq), 1)
    u = _suffix_ones(tq)

    def tile(kt, acc, carry, valid):
        ks = pl.multiple_of(kt * tq, tq)
        z = jnp.concatenate([_dot_nt(q2[p], k_ref[0, pl.ds(ks, tq), pair(p)]) for p in range(n_pairs)], axis=0)
        a, carry = _sb_weights(z, carry, valid, u)
        pv = jnp.concatenate([_dot(a[p * 2 * tq:(p + 1) * 2 * tq], v_ref[0, pl.ds(ks, tq), pair(p)])
                              for p in range(n_pairs)], axis=0)
        return acc + pv, carry

    acc = jnp.zeros((m, lanes), F32)
    carry = jnp.zeros((m, 1), F32)
    acc, carry = tile(qi, acc, carry, col < row)

    def cond(c):
        return jnp.logical_and(c[0] < qi, c[3] > 0)

    def body(c):
        acc, carry = tile(qi - 1 - c[0], c[1], c[2], None)
        return c[0] + 1, acc, carry, _sb_live(carry)

    _, acc, _, _ = lax.while_loop(cond, body, (jnp.int32(0), acc, carry, _sb_live(carry)))
    o_ref[0] = jnp.concatenate(
        [jnp.where(lo_half, acc[2 * p * tq:(2 * p + 1) * tq], acc[(2 * p + 1) * tq:(2 * p + 2) * tq])
         for p in range(n_pairs)], axis=-1).astype(BF16)


def _sb_prompt(sq, sk, sv, *, tq):
    b, t, w = sq.shape
    q_spec = pl.BlockSpec((1, tq, w), lambda i, j: (i, j, 0))
    kv_spec = pl.BlockSpec((1, t, w), lambda i, j: (i, 0, 0))
    return pl.pallas_call(
        functools.partial(_sb_prompt_kernel, tq=tq),
        grid=(b, t // tq),
        in_specs=[q_spec, kv_spec, kv_spec],
        out_specs=q_spec,
        out_shape=jax.ShapeDtypeStruct((b, t, w), BF16),
        compiler_params=pltpu.CompilerParams(
            dimension_semantics=("parallel", "arbitrary"), vmem_limit_bytes=VMEM_LIMIT),
        name="sb_prompt",
    )(sq, sk, sv)


def _sb_sample_kernel(q_ref, kn_ref, vn_ref, kct_ref, vct_ref, o_ref, acc_scr, carry_scr, *, tl, tk):
    nh = q_ref.shape[1]
    n_tiles = kct_ref.shape[3] // tk
    m = nh * tl

    def per_head(fn):
        return jnp.concatenate([fn(h) for h in range(nh)], axis=0)

    row = lax.broadcasted_iota(jnp.int32, (m, tl), 0) & (tl - 1)
    col = lax.broadcasted_iota(jnp.int32, (m, tl), 1)
    z = per_head(lambda h: _dot_nt(q_ref[0, h], kn_ref[0, h].astype(BF16)))
    a, carry = _sb_weights(z, jnp.zeros((m, 1), F32), col < row, _suffix_ones(tl))
    acc_scr[...] = per_head(lambda h: _dot(a[h * tl:(h + 1) * tl], vn_ref[0, h].astype(BF16)))
    carry_scr[...] = carry
    u = _suffix_ones(tk)
    for j in reversed(range(n_tiles)):
        @pl.when(jnp.max(carry_scr[...]) > SB_SKIP_BELOW)
        def _(j=j):
            cols = slice(j * tk, (j + 1) * tk)
            z = per_head(lambda h: _dot(q_ref[0, h], kct_ref[0, h, :, cols].astype(BF16)))
            a, carry = _sb_weights(z, carry_scr[...], None, u)
            acc_scr[...] += per_head(
                lambda h: _dot_nt(a[h * tl:(h + 1) * tl], vct_ref[0, h, :, cols].astype(BF16)))
            carry_scr[...] = carry

    acc = acc_scr[...]
    o_ref[0] = jnp.concatenate([acc[h * tl:(h + 1) * tl] for h in range(nh)], axis=-1).astype(BF16)


def _sb_sample(sqh, sk_new, sv_new, k_cache_t, v_cache_t, *, tk):
    b, nh, tl, dh = sqh.shape
    past = k_cache_t.shape[3]
    new_spec = pl.BlockSpec((1, nh, tl, dh), lambda i: (i, 0, 0, 0))
    cache_spec = pl.BlockSpec((1, nh, dh, past), lambda i: (i, 0, 0, 0))
    return pl.pallas_call(
        functools.partial(_sb_sample_kernel, tl=tl, tk=tk),
        grid=(b,),
        in_specs=[new_spec, new_spec, new_spec, cache_spec, cache_spec],
        out_specs=pl.BlockSpec((1, tl, nh * dh), lambda i: (i, 0, 0)),
        out_shape=jax.ShapeDtypeStruct((b, tl, nh * dh), BF16),
        scratch_shapes=[pltpu.VMEM((nh * tl, dh), F32), pltpu.VMEM((nh * tl, 1), F32)],
        compiler_params=pltpu.CompilerParams(dimension_semantics=("parallel",), vmem_limit_bytes=VMEM_LIMIT),
        name="sb_sample",
    )(sqh, sk_new, sv_new, k_cache_t, v_cache_t)


def _post_kernel(x_ref, ro_ref, so_ref, mk_ref, mv_ref, cpast_ref,
                 wo_ref, ln1g_ref, ln1b_ref, wq_ref, wom_ref, ln2g_ref, ln2b_ref,
                 wup_ref, cw_ref, cb_ref, wdn_ref, ln3g_ref, ln3b_ref,
                 y_ref, cst_ref, prev_scr, act_scr, *, nb, tl, alpha, fb, n_groups):
    t = pl.program_id(1)
    m = nb * tl
    sub = CONV_W - 1

    @pl.when(t == 0)
    def _():
        prev_scr[...] = jnp.zeros_like(prev_scr)
        prev_scr[:, 8 - sub:8, :] = cpast_ref[...]

    d = x_ref.shape[-1]
    hw = ro_ref.shape[-1]
    dh = mk_ref.shape[3]
    d_ff = wdn_ref.shape[0]
    if nb == 1:
        r = tl // n_groups
        groups = [[(0, i * r, r)] for i in range(n_groups)]
    else:
        per = nb // n_groups
        groups = [[(b, b * tl, tl) for b in range(i * per, (i + 1) * per)] for i in range(n_groups)]
    x_all = x_ref[...].reshape(m, d)
    ro_all = ro_ref[...].reshape(m, hw)
    so_all = so_ref[...].reshape(m, hw)

    def mixer_and_memory(segs):
        g0, g1 = segs[0][1], segs[-1][1] + segs[-1][2]
        mix_in = jnp.concatenate([ro_all[g0:g1], so_all[g0:g1]], axis=-1)
        x1 = _ln(alpha * x_all[g0:g1] + _dot(mix_in, wo_ref[...]), ln1g_ref[...], ln1b_ref[...])
        q = _dot(x1.astype(BF16), wq_ref[...])
        outs = []
        for b, s0, n in segs:
            heads = []
            for h in range(MEM_HEADS):
                qh = q[s0 - g0:s0 - g0 + n, h * dh:(h + 1) * dh].astype(BF16)
                s = _dot_nt(qh, mk_ref[b, h]) * (dh ** -0.5)
                e = jnp.exp(s - jnp.max(s, -1, keepdims=True))
                p = e * (1.0 / jnp.sum(e, -1, keepdims=True))
                heads.append(_dot(p.astype(BF16), mv_ref[b, h]).astype(BF16))
            outs.append(jnp.concatenate(heads, axis=-1))
        att_in = outs[0] if len(outs) == 1 else jnp.concatenate(outs, axis=0)
        return _ln(alpha * x1 + _dot(att_in, wom_ref[...]), ln2g_ref[...], ln2b_ref[...])

    x2 = [mixer_and_memory(segs) for segs in groups]
    x2b = [v.astype(BF16) for v in x2]

    sub_idx = lax.broadcasted_iota(jnp.int32, (1, 8, fb), 1)

    def conv(u, prev8, c0):
        g = u.shape[0] // 8
        u3 = u.reshape(g, 8, fb)
        ext = jnp.concatenate([prev8[None], u3], axis=0)
        r1 = pltpu.roll(ext, 1, axis=1)
        r2 = pltpu.roll(ext, 2, axis=1)
        u1 = jnp.where(sub_idx < 1, r1[:-1], r1[1:])
        u2 = jnp.where(sub_idx < 2, r2[:-1], r2[1:])
        cols = slice(c0, c0 + fb)
        c = (cb_ref[:, cols][None] + cw_ref[0:1, cols][None] * u2
             + cw_ref[1:2, cols][None] * u1 + cw_ref[2:3, cols][None] * u3)
        return c.reshape(u.shape), u3[g - 1]

    def conv_group(u, segs, g0, c0, chain):
        parts = []
        for b, s0, n in segs:
            prev8 = chain.get(b)
            if prev8 is None:
                prev8 = prev_scr[b, :, c0:c0 + fb]
            c, chain[b] = conv(u[s0 - g0:s0 - g0 + n], prev8, c0)
            parts.append(c)
        return parts[0] if len(parts) == 1 else jnp.concatenate(parts, axis=0)

    for blk in range(d_ff // fb):
        ca, cg = blk * fb, d_ff + blk * fb
        chain_a, chain_g = {}, {}
        for segs, xg in zip(groups, x2b):
            g0, g1 = segs[0][1], segs[-1][1] + segs[-1][2]
            a = conv_group(_dot(xg, wup_ref[:, ca:ca + fb]), segs, g0, ca, chain_a)
            g = conv_group(_dot(xg, wup_ref[:, cg:cg + fb]), segs, g0, cg, chain_g)
            act_scr[g0:g1, ca:ca + fb] = (_silu(a) * g).astype(BF16)
        for b in range(nb):
            prev_scr[b, :, ca:ca + fb] = chain_a[b]
            prev_scr[b, :, cg:cg + fb] = chain_g[b]

    for segs, x2g in zip(groups, x2):
        g0, g1 = segs[0][1], segs[-1][1] + segs[-1][2]
        f = _dot(act_scr[g0:g1, :], wdn_ref[...])
        y = _ln(alpha * x2g + f, ln3g_ref[...], ln3b_ref[...])
        for b, s0, n in segs:
            y_ref[b, s0 - b * tl:s0 - b * tl + n, :] = y[s0 - g0:s0 - g0 + n]
    cst_ref[...] = prev_scr[:, 8 - sub:8, :]


def _post(x, ro, so, mk_b, mv_b, conv_past, lw, *, nb, tl, alpha):
    b, t, d = x.shape
    assert nb == 1 or tl == t
    hw = ro.shape[-1]
    two_ff = lw["w_up"].shape[1]
    row = lambda w: pl.BlockSpec((nb, tl, w), lambda i, j: (i, j, 0))
    mem_spec = pl.BlockSpec((nb,) + mk_b.shape[1:], lambda i, j: (i, 0, 0, 0))
    cst_spec = pl.BlockSpec((nb, CONV_W - 1, two_ff), lambda i, j: (i, 0, 0))
    names = ("w_o", "ln1_g", "ln1_b", "w_q_mem", "w_o_mem", "ln2_g", "ln2_b",
             "w_up", "conv_w", "conv_b", "w_down", "ln3_g", "ln3_b")
    weights = [lw[n] for n in names]
    n_groups = 2
    assert (tl if nb == 1 else nb) % n_groups == 0 and tl % 8 == 0
    return pl.pallas_call(
        functools.partial(_post_kernel, nb=nb, tl=tl, alpha=alpha, fb=256, n_groups=n_groups),
        grid=(b // nb, t // tl),
        in_specs=[row(d), row(hw), row(hw), mem_spec, mem_spec, cst_spec] + [_const_spec(w.shape) for w in weights],
        out_specs=[row(d), cst_spec],
        out_shape=[jax.ShapeDtypeStruct((b, t, d), F32), jax.ShapeDtypeStruct((b, CONV_W - 1, two_ff), F32)],
        scratch_shapes=[pltpu.VMEM((nb, 8, two_ff), F32), pltpu.VMEM((nb * tl, two_ff // 2), BF16)],
        compiler_params=pltpu.CompilerParams(
            dimension_semantics=("parallel", "arbitrary"), vmem_limit_bytes=VMEM_LIMIT),
        name="post",
    )(x, ro, so, mk_b, mv_b, conv_past, *weights)


def _rope_tables(pos):
    half = RET_D // 2
    inv = 1.0 / (ROPE_BASE ** (jnp.arange(half, dtype=F32) / half))
    ang = pos.astype(F32)[:, None] * inv[None, :]
    c, s = jnp.cos(ang), jnp.sin(ang)
    return jnp.concatenate([c, c], axis=-1), jnp.concatenate([-s, s], axis=-1)


def kernel(x_prompt, x_sample, cache_sb_k, cache_sb_v, state_ret, state_ffn_conv, cache_mem_k, cache_mem_v, mem_prompt, w_in, w_o, ln1_g, ln1_b, w_q_mem, w_k_mem, w_v_mem, w_o_mem, ln2_g, ln2_b, w_up, conv_w, conv_b, w_down, ln3_g, ln3_b):
    depth = w_in.shape[0]
    alpha = (2.0 * depth) ** 0.25
    xp, xs = x_prompt, x_sample
    bp, tp, _ = xp.shape
    bs, ts, _ = xs.shape
    past_len = cache_sb_k.shape[3]
    two_ff = w_up.shape[2]
    cos_p, sin_p = _rope_tables(jnp.arange(tp, dtype=jnp.int32))
    cos_s, sin_s = _rope_tables(past_len + jnp.arange(ts, dtype=jnp.int32))
    row2 = lambda a: a.reshape(1, -1)
    swap = lambda a: jnp.swapaxes(a, -1, -2)
    outs = [[] for _ in range(10)]
    for l in range(depth):
        lw = {"w_o": w_o[l].astype(BF16), "ln1_g": row2(ln1_g[l]), "ln1_b": row2(ln1_b[l]),
              "w_q_mem": w_q_mem[l].astype(BF16), "w_o_mem": w_o_mem[l].astype(BF16),
              "ln2_g": row2(ln2_g[l]), "ln2_b": row2(ln2_b[l]),
              "w_up": w_up[l].astype(BF16), "conv_w": conv_w[l], "conv_b": row2(conv_b[l]),
              "w_down": w_down[l].astype(BF16), "ln3_g": row2(ln3_g[l]), "ln3_b": row2(ln3_b[l])}
        w_in_b = w_in[l].astype(BF16)

        mk, mv, mk_b, mv_b = _memkv(mem_prompt, w_k_mem[l].astype(BF16), w_v_mem[l].astype(BF16))
        ret, sq, skb, svb, pk_t, pv_t = _proj(xp, w_in_b, cos_p, sin_p, nb=1, tl=512, head_split_q=False)
        ro, ps = _retention(ret, None, c=256)
        so = _sb_prompt(sq, skb, svb, tq=256)
        xp, pc = _post(xp, ro, so, mk_b, mv_b, jnp.zeros((bp, CONV_W - 1, two_ff), F32), lw,
                       nb=1, tl=512, alpha=alpha)

        ret, sqh, sk, sv = _proj(xs, w_in_b, cos_s, sin_s, nb=bs, tl=ts, head_split_q=True)
        ro, ss = _retention(ret, state_ret[l], c=min(REF_CHUNK, ts))
        so = _sb_sample(sqh, sk, sv, swap(cache_sb_k[l]), swap(cache_sb_v[l]), tk=256)
        xs, sc = _post(xs, ro, so, cache_mem_k[l].astype(BF16), cache_mem_v[l].astype(BF16),
                       state_ffn_conv[l], lw, nb=min(bs, 8), tl=ts, alpha=alpha)
        for lst, val in zip(outs, (swap(pk_t), swap(pv_t), ps, pc, mk, mv, sk, sv, ss, sc)):
            lst.append(val)
    return (xp, xs) + tuple(jnp.stack(o) for o in outs)
```

```python
import functools
import math

import jax
import jax.numpy as jnp
import numpy as np
from jax import lax
from jax.experimental import pallas as pl
from jax.experimental.pallas import tpu as pltpu

F32 = jnp.float32
BF16 = jnp.bfloat16

RET_HEADS = 4
RET_D = 128
SB_HEADS = 8
SB_DH = 64
MEM_HEADS = 4
CONV_W = 3
ROPE_BASE = 10000.0
LN_EPS = 1e-5
RMS_EPS = 1e-6
LOG2E = math.log2(math.e)
REF_CHUNK = 64

VMEM_LIMIT = 56 * 1024 * 1024


def _const_spec(shape):
    nd = len(shape)
    return pl.BlockSpec(shape, lambda *_: (0,) * nd, pipeline_mode=pl.Buffered(1))


def _ln(x, g, b):
    mu = jnp.mean(x, -1, keepdims=True)
    xc = x - mu
    var = jnp.mean(xc * xc, -1, keepdims=True)
    return xc * lax.rsqrt(var + LN_EPS) * g + b


def _silu(x):
    return x * (1.0 / (1.0 + jnp.exp(-x)))


def _dot(a, b):
    return jnp.dot(a, b, preferred_element_type=F32)


def _dot_nt(a, b):
    return lax.dot_general(a, b, (((1,), (1,)), ((), ())), preferred_element_type=F32)


def _proj_kernel(x_ref, w_ref, cos_ref, sin_ref, *out_refs, nb, tl, d_model, head_split_q):
    if head_split_q:
        ret_ref, sqh_ref, sk_ref, sv_ref = out_refs
    else:
        ret_ref, sq_ref, skb_ref, svb_ref, sk_ref, sv_ref = out_refs
    m = nb * tl
    xb = x_ref[...].reshape(m, d_model).astype(BF16)
    cos = jnp.broadcast_to(cos_ref[...][None], (nb, tl, RET_D)).reshape(m, RET_D)
    sin = jnp.broadcast_to(sin_ref[...][None], (nb, tl, RET_D)).reshape(m, RET_D)
    hw = RET_HEADS * RET_D

    def seg(i):
        return _dot(xb, w_ref[:, i * hw:(i + 1) * hw])

    def rope(r, scale):
        outs = []
        for h in range(RET_HEADS):
            xh = r[:, h * RET_D:(h + 1) * RET_D]
            o = xh * cos + pltpu.roll(xh, RET_D // 2, axis=1) * sin
            outs.append(o * scale if scale != 1.0 else o)
        return jnp.concatenate(outs, axis=-1)

    def put_ret(i, val):
        ret_ref[:, :, i * hw:(i + 1) * hw] = val.astype(BF16).reshape(nb, tl, hw)

    put_ret(0, rope(seg(0), 1.0))
    put_ret(1, rope(seg(1), RET_D ** -0.5))
    put_ret(2, seg(2))
    put_ret(3, seg(3))

    def put_heads(ref, val):
        for h in range(SB_HEADS):
            ref[:, h, :, :] = val[:, h * SB_DH:(h + 1) * SB_DH].astype(ref.dtype).reshape(nb, tl, SB_DH)

    sq = seg(4) * (SB_DH ** -0.5)
    sk = seg(5)
    sv = seg(6)
    if head_split_q:
        put_heads(sqh_ref, sq)
        put_heads(sk_ref, sk)
        put_heads(sv_ref, sv)
    else:
        sq_ref[...] = sq.astype(BF16).reshape(nb, tl, hw)
        skb_ref[...] = sk.astype(BF16).reshape(nb, tl, hw)
        svb_ref[...] = sv.astype(BF16).reshape(nb, tl, hw)
        sk_ref[0] = sk.T.reshape(SB_HEADS, SB_DH, tl)
        sv_ref[0] = sv.T.reshape(SB_HEADS, SB_DH, tl)


def _proj(x, w_in_b, cos, sin, *, nb, tl, head_split_q):
    b, t, d = x.shape
    hw = RET_HEADS * RET_D
    grid = (b // nb, t // tl)
    hs = lambda dt: jax.ShapeDtypeStruct((b, SB_HEADS, t, SB_DH), dt)
    hs_spec = pl.BlockSpec((nb, SB_HEADS, tl, SB_DH), lambda i, j: (i, 0, j, 0))
    row_spec = lambda w: pl.BlockSpec((nb, tl, w), lambda i, j: (i, j, 0))
    out_shape = [jax.ShapeDtypeStruct((b, t, 4 * hw), BF16)]
    out_specs = [row_spec(4 * hw)]
    if head_split_q:
        out_shape += [hs(BF16), hs(F32), hs(F32)]
        out_specs += [hs_spec] * 3
    else:
        assert nb == 1
        out_shape += [jax.ShapeDtypeStruct((b, t, hw), BF16)] * 3
        out_specs += [row_spec(hw)] * 3
        out_shape += [jax.ShapeDtypeStruct((b, SB_HEADS, SB_DH, t), F32)] * 2
        out_specs += [pl.BlockSpec((1, SB_HEADS, SB_DH, tl), lambda i, j: (i, 0, 0, j))] * 2
    return pl.pallas_call(
        functools.partial(_proj_kernel, nb=nb, tl=tl, d_model=d, head_split_q=head_split_q),
        grid=grid,
        in_specs=[row_spec(d), _const_spec(w_in_b.shape),
                  pl.BlockSpec((tl, RET_D), lambda i, j: (j, 0)),
                  pl.BlockSpec((tl, RET_D), lambda i, j: (j, 0))],
        out_specs=out_specs, out_shape=out_shape,
        compiler_params=pltpu.CompilerParams(
            dimension_semantics=("parallel", "parallel"), vmem_limit_bytes=VMEM_LIMIT),
        name="proj",
    )(x, w_in_b, cos, sin)


def _memkv_kernel(m_ref, wk_ref, wv_ref, mk_ref, mv_ref, mkb_ref, mvb_ref, *, dh):
    mb = m_ref[0].astype(BF16)
    for w_ref, o_ref, ob_ref in ((wk_ref, mk_ref, mkb_ref), (wv_ref, mv_ref, mvb_ref)):
        r = _dot(mb, w_ref[...])
        for h in range(MEM_HEADS):
            rh = r[:, h * dh:(h + 1) * dh]
            o_ref[0, h] = rh
            ob_ref[0, h] = rh.astype(BF16)


def _memkv(mem, wk_b, wv_b):
    b, n, d = mem.shape
    dh = d // MEM_HEADS
    spec = pl.BlockSpec((1, MEM_HEADS, n, dh), lambda i: (i, 0, 0, 0))
    return pl.pallas_call(
        functools.partial(_memkv_kernel, dh=dh),
        grid=(b,),
        in_specs=[pl.BlockSpec((1, n, d), lambda i: (i, 0, 0)), _const_spec(wk_b.shape), _const_spec(wv_b.shape)],
        out_specs=[spec] * 4,
        out_shape=[jax.ShapeDtypeStruct((b, MEM_HEADS, n, dh), F32)] * 2
        + [jax.ShapeDtypeStruct((b, MEM_HEADS, n, dh), BF16)] * 2,
        compiler_params=pltpu.CompilerParams(dimension_semantics=("parallel",), vmem_limit_bytes=VMEM_LIMIT),
        name="memkv",
    )(mem, wk_b, wv_b)


def _ret_kernel(*refs, c, n_chunks, has_s0):
    if has_s0:
        ret_ref, s0_ref, ro_ref, sf_ref, s_scr, dmask_scr = refs
    else:
        ret_ref, ro_ref, sf_ref, s_scr, dmask_scr = refs
    t = pl.program_id(1)
    log_g = [math.log1p(-(2.0 ** (-5.0 - h))) for h in range(RET_HEADS)]

    @pl.when(t == 0)
    def _():
        s_scr[...] = s0_ref[0] if has_s0 else jnp.zeros_like(s_scr)
        row = lax.broadcasted_iota(jnp.int32, (c, c), 0)
        col = lax.broadcasted_iota(jnp.int32, (c, c), 1)
        diff = (row - col).astype(F32)
        for h in range(RET_HEADS):
            dmask_scr[h] = jnp.where(diff >= 0, jnp.exp(log_g[h] * jnp.maximum(diff, 0.0)), 0.0)

    hw = RET_HEADS * RET_D
    idx = lax.broadcasted_iota(jnp.int32, (c, 1), 0).astype(F32)
    for ci in range(n_chunks):
        rows = slice(ci * c, (ci + 1) * c)
        for h in range(RET_HEADS):
            sl = lambda i: ret_ref[0, rows, i * hw + h * RET_D:i * hw + (h + 1) * RET_D]
            q, k, v, g = sl(0), sl(1), sl(2), sl(3)
            scores = _dot_nt(q, k) * dmask_scr[h]
            o = _dot(scores.astype(BF16), v)
            s_prev = s_scr[h]
            qd = (q.astype(F32) * jnp.exp(log_g[h] * (idx + 1.0))).astype(BF16)
            o = o + _dot(qd, s_prev.astype(BF16))
            kd = (k.astype(F32) * jnp.exp(log_g[h] * (c - 1.0 - idx))).astype(BF16)
            s_scr[h] = math.exp(log_g[h] * c) * s_prev + _dot(kd.T, v)
            o = o * lax.rsqrt(jnp.mean(o * o, -1, keepdims=True) + RMS_EPS)
            ro_ref[0, rows, h * RET_D:(h + 1) * RET_D] = (o * _silu(g.astype(F32))).astype(BF16)

    @pl.when(t == pl.num_programs(1) - 1)
    def _():
        sf_ref[0] = s_scr[...]


def _retention(ret, s0, *, c, n_chunks):
    b, t, w = ret.shape
    hw = RET_HEADS * RET_D
    tl = c * n_chunks
    st_spec = pl.BlockSpec((1, RET_HEADS, RET_D, RET_D), lambda i, j: (i, 0, 0, 0))
    in_specs = [pl.BlockSpec((1, tl, w), lambda i, j: (i, j, 0))]
    args = [ret]
    if s0 is not None:
        in_specs.append(st_spec)
        args.append(s0)
    return pl.pallas_call(
        functools.partial(_ret_kernel, c=c, n_chunks=n_chunks, has_s0=s0 is not None),
        grid=(b, t // tl),
        in_specs=in_specs,
        out_specs=[pl.BlockSpec((1, tl, hw), lambda i, j: (i, j, 0)), st_spec],
        out_shape=[jax.ShapeDtypeStruct((b, t, hw), BF16),
                   jax.ShapeDtypeStruct((b, RET_HEADS, RET_D, RET_D), F32)],
        scratch_shapes=[pltpu.VMEM((RET_HEADS, RET_D, RET_D), F32), pltpu.VMEM((RET_HEADS, c, c), F32)],
        compiler_params=pltpu.CompilerParams(
            dimension_semantics=("parallel", "arbitrary"), vmem_limit_bytes=VMEM_LIMIT),
        name="retention",
    )(*args)


SB_SKIP_ABOVE = 106.0


def _suffix_ones(n):
    j = lax.broadcasted_iota(jnp.int32, (n, n), 0)
    s = lax.broadcasted_iota(jnp.int32, (n, n), 1)
    return jnp.where(j >= s, 1.0, 0.0).astype(BF16)


def _sb_weights(z, carry, valid, u):
    sp = jnp.maximum(z, 0.0) + jnp.log(1.0 + jnp.exp2(jnp.abs(z) * -LOG2E))
    if valid is not None:
        sp = jnp.where(valid, sp, 0.0)
    cs = _dot(sp.astype(BF16), u)
    a = jnp.exp((z - carry) - cs)
    if valid is not None:
        a = jnp.where(valid, a, 0.0)
    return a.astype(BF16), carry + cs[:, 0:1]


def _sb_live(carry):
    return (jnp.min(carry) < SB_SKIP_ABOVE).astype(jnp.int32)


def _sb_prompt_kernel(q_ref, k_ref, v_ref, o_ref, *, tq):
    qi = pl.program_id(1)
    lanes = 2 * SB_DH
    n_pairs = q_ref.shape[-1] // lanes
    pair = lambda p: slice(p * lanes, (p + 1) * lanes)
    lo_half = lax.broadcasted_iota(jnp.int32, (1, lanes), 1) < SB_DH
    q2 = []
    for p in range(n_pairs):
        q = q_ref[0, :, pair(p)]
        zero = jnp.zeros_like(q)
        q2.append(jnp.concatenate([jnp.where(lo_half, q, zero), jnp.where(lo_half, zero, q)], axis=0))
    m = 2 * n_pairs * tq
    row = lax.broadcasted_iota(jnp.int32, (m, tq), 0) & (tq - 1)
    col = lax.broadcasted_iota(jnp.int32, (m, tq), 1)
    u = _suffix_ones(tq)

    def tile(kt, acc, carry, valid):
        ks = pl.multiple_of(kt * tq, tq)
        z = jnp.concatenate([_dot_nt(q2[p], k_ref[0, pl.ds(ks, tq), pair(p)]) for p in range(n_pairs)], axis=0)
        a, carry = _sb_weights(z, carry, valid, u)
        pv = jnp.concatenate([_dot(a[p * 2 * tq:(p + 1) * 2 * tq], v_ref[0, pl.ds(ks, tq), pair(p)])
                              for p in range(n_pairs)], axis=0)
        return acc + pv, carry

    acc = jnp.zeros((m, lanes), F32)
    carry = jnp.zeros((m, 1), F32)
    acc, carry = tile(qi, acc, carry, col < row)

    def cond(c):
        return jnp.logical_and(c[0] < qi, c[3] > 0)

    def body(c):
        acc, carry = tile(qi - 1 - c[0], c[1], c[2], None)
        return c[0] + 1, acc, carry, _sb_live(carry)

    _, acc, _, _ = lax.while_loop(cond, body, (jnp.int32(0), acc, carry, _sb_live(carry)))
    o_ref[0] = jnp.concatenate(
        [jnp.where(lo_half, acc[2 * p * tq:(2 * p + 1) * tq], acc[(2 * p + 1) * tq:(2 * p + 2) * tq])
         for p in range(n_pairs)], axis=-1).astype(BF16)


def _sb_prompt(sq, sk, sv, *, tq):
    b, t, w = sq.shape
    q_spec = pl.BlockSpec((1, tq, w), lambda i, j: (i, j, 0))
    kv_spec = pl.BlockSpec((1, t, w), lambda i, j: (i, 0, 0))
    return pl.pallas_call(
        functools.partial(_sb_prompt_kernel, tq=tq),
        grid=(b, t // tq),
        in_specs=[q_spec, kv_spec, kv_spec],
        out_specs=q_spec,
        out_shape=jax.ShapeDtypeStruct((b, t, w), BF16),
        compiler_params=pltpu.CompilerParams(
            dimension_semantics=("parallel", "arbitrary"), vmem_limit_bytes=VMEM_LIMIT),
        name="sb_prompt",
    )(sq, sk, sv)


def _sb_sample_kernel(q_ref, kn_ref, vn_ref, kct_ref, vct_ref, o_ref, acc_scr, carry_scr, *, tl, tk):
    nh = q_ref.shape[1]
    n_tiles = kct_ref.shape[3] // tk
    m = nh * tl

    def per_head(fn):
        return jnp.concatenate([fn(h) for h in range(nh)], axis=0)

    row = lax.broadcasted_iota(jnp.int32, (m, tl), 0) & (tl - 1)
    col = lax.broadcasted_iota(jnp.int32, (m, tl), 1)
    z = per_head(lambda h: _dot_nt(q_ref[0, h], kn_ref[0, h].astype(BF16)))
    a, carry = _sb_weights(z, jnp.zeros((m, 1), F32), col < row, _suffix_ones(tl))
    acc_scr[...] = per_head(lambda h: _dot(a[h * tl:(h + 1) * tl], vn_ref[0, h].astype(BF16)))
    carry_scr[...] = carry
    u = _suffix_ones(tk)
    for j in reversed(range(n_tiles)):
        @pl.when(jnp.min(carry_scr[...]) < SB_SKIP_ABOVE)
        def _(j=j):
            cols = slice(j * tk, (j + 1) * tk)
            z = per_head(lambda h: _dot(q_ref[0, h], kct_ref[0, h, :, cols].astype(BF16)))
            a, carry = _sb_weights(z, carry_scr[...], None, u)
            acc_scr[...] += per_head(
                lambda h: _dot_nt(a[h * tl:(h + 1) * tl], vct_ref[0, h, :, cols].astype(BF16)))
            carry_scr[...] = carry

    acc = acc_scr[...]
    o_ref[0] = jnp.concatenate([acc[h * tl:(h + 1) * tl] for h in range(nh)], axis=-1).astype(BF16)


def _sb_sample(sqh, sk_new, sv_new, k_cache_t, v_cache_t, *, tk):
    b, nh, tl, dh = sqh.shape
    past = k_cache_t.shape[3]
    new_spec = pl.BlockSpec((1, nh, tl, dh), lambda i: (i, 0, 0, 0))
    cache_spec = pl.BlockSpec((1, nh, dh, past), lambda i: (i, 0, 0, 0))
    return pl.pallas_call(
        functools.partial(_sb_sample_kernel, tl=tl, tk=tk),
        grid=(b,),
        in_specs=[new_spec, new_spec, new_spec, cache_spec, cache_spec],
        out_specs=pl.BlockSpec((1, tl, nh * dh), lambda i: (i, 0, 0)),
        out_shape=jax.ShapeDtypeStruct((b, tl, nh * dh), BF16),
        scratch_shapes=[pltpu.VMEM((nh * tl, dh), F32), pltpu.VMEM((nh * tl, 1), F32)],
        compiler_params=pltpu.CompilerParams(dimension_semantics=("parallel",), vmem_limit_bytes=VMEM_LIMIT),
        name="sb_sample",
    )(sqh, sk_new, sv_new, k_cache_t, v_cache_t)


def _post_kernel(x_ref, ro_ref, so_ref, mk_ref, mv_ref, cpast_ref,
                 wo_ref, ln1g_ref, ln1b_ref, wq_ref, wom_ref, ln2g_ref, ln2b_ref,
                 wup_ref, cw_ref, cb_ref, wdn_ref, ln3g_ref, ln3b_ref,
                 y_ref, cst_ref, prev_scr, act_scr, *, nb, tl, alpha, fb, n_groups):
    t = pl.program_id(1)
    m = nb * tl
    sub = CONV_W - 1

    @pl.when(t == 0)
    def _():
        prev_scr[...] = jnp.zeros_like(prev_scr)
        prev_scr[:, 8 - sub:8, :] = cpast_ref[...]

    d = x_ref.shape[-1]
    hw = ro_ref.shape[-1]
    dh = mk_ref.shape[3]
    d_ff = wdn_ref.shape[0]
    if nb == 1:
        r = tl // n_groups
        groups = [[(0, i * r, r)] for i in range(n_groups)]
    else:
        per = nb // n_groups
        groups = [[(b, b * tl, tl) for b in range(i * per, (i + 1) * per)] for i in range(n_groups)]
    x_all = x_ref[...].reshape(m, d)
    ro_all = ro_ref[...].reshape(m, hw)
    so_all = so_ref[...].reshape(m, hw)

    def mixer_and_memory(segs):
        g0, g1 = segs[0][1], segs[-1][1] + segs[-1][2]
        mix_in = jnp.concatenate([ro_all[g0:g1], so_all[g0:g1]], axis=-1)
        x1 = _ln(alpha * x_all[g0:g1] + _dot(mix_in, wo_ref[...]), ln1g_ref[...], ln1b_ref[...])
        q = _dot(x1.astype(BF16), wq_ref[...])
        outs = []
        for b, s0, n in segs:
            heads = []
            for h in range(MEM_HEADS):
                qh = q[s0 - g0:s0 - g0 + n, h * dh:(h + 1) * dh].astype(BF16)
                s = _dot_nt(qh, mk_ref[b, h]) * (dh ** -0.5)
                e = jnp.exp(s - jnp.max(s, -1, keepdims=True))
                p = e * (1.0 / jnp.sum(e, -1, keepdims=True))
                heads.append(_dot(p.astype(BF16), mv_ref[b, h]).astype(BF16))
            outs.append(jnp.concatenate(heads, axis=-1))
        att_in = outs[0] if len(outs) == 1 else jnp.concatenate(outs, axis=0)
        return _ln(alpha * x1 + _dot(att_in, wom_ref[...]), ln2g_ref[...], ln2b_ref[...])

    x2 = [mixer_and_memory(segs) for segs in groups]
    x2b = [v.astype(BF16) for v in x2]

    sub_idx = lax.broadcasted_iota(jnp.int32, (1, 8, fb), 1)

    def conv(u, prev8, c0):
        g = u.shape[0] // 8
        u3 = u.reshape(g, 8, fb)
        ext = jnp.concatenate([prev8[None], u3], axis=0)
        r1 = pltpu.roll(ext, 1, axis=1)
        r2 = pltpu.roll(ext, 2, axis=1)
        u1 = jnp.where(sub_idx < 1, r1[:-1], r1[1:])
        u2 = jnp.where(sub_idx < 2, r2[:-1], r2[1:])
        cols = slice(c0, c0 + fb)
        c = (cb_ref[:, cols][None] + cw_ref[0:1, cols][None] * u2
             + cw_ref[1:2, cols][None] * u1 + cw_ref[2:3, cols][None] * u3)
        return c.reshape(u.shape), u3[g - 1]

    def conv_group(u, segs, g0, c0, chain):
        parts = []
        for b, s0, n in segs:
            prev8 = chain.get(b)
            if prev8 is None:
                prev8 = prev_scr[b, :, c0:c0 + fb]
            c, chain[b] = conv(u[s0 - g0:s0 - g0 + n], prev8, c0)
            parts.append(c)
        return parts[0] if len(parts) == 1 else jnp.concatenate(parts, axis=0)

    for blk in range(d_ff // fb):
        ca, cg = blk * fb, d_ff + blk * fb
        chain_a, chain_g = {}, {}
        for segs, xg in zip(groups, x2b):
            g0, g1 = segs[0][1], segs[-1][1] + segs[-1][2]
            a = conv_group(_dot(xg, wup_ref[:, ca:ca + fb]), segs, g0, ca, chain_a)
            g = conv_group(_dot(xg, wup_ref[:, cg:cg + fb]), segs, g0, cg, chain_g)
            act_scr[g0:g1, ca:ca + fb] = (_silu(a) * g).astype(BF16)
        for b in range(nb):
            prev_scr[b, :, ca:ca + fb] = chain_a[b]
            prev_scr[b, :, cg:cg + fb] = chain_g[b]

    for segs, x2g in zip(groups, x2):
        g0, g1 = segs[0][1], segs[-1][1] + segs[-1][2]
        f = _dot(act_scr[g0:g1, :], wdn_ref[...])
        y = _ln(alpha * x2g + f, ln3g_ref[...], ln3b_ref[...])
        for b, s0, n in segs:
            y_ref[b, s0 - b * tl:s0 - b * tl + n, :] = y[s0 - g0:s0 - g0 + n]
    cst_ref[...] = prev_scr[:, 8 - sub:8, :]


def _post(x, ro, so, mk_b, mv_b, conv_past, lw, *, nb, tl, alpha):
    b, t, d = x.shape
    assert nb == 1 or tl == t
    hw = ro.shape[-1]
    two_ff = lw["w_up"].shape[1]
    row = lambda w: pl.BlockSpec((nb, tl, w), lambda i, j: (i, j, 0))
    mem_spec = pl.BlockSpec((nb,) + mk_b.shape[1:], lambda i, j: (i, 0, 0, 0))
    cst_spec = pl.BlockSpec((nb, CONV_W - 1, two_ff), lambda i, j: (i, 0, 0))
    names = ("w_o", "ln1_g", "ln1_b", "w_q_mem", "w_o_mem", "ln2_g", "ln2_b",
             "w_up", "conv_w", "conv_b", "w_down", "ln3_g", "ln3_b")
    weights = [lw[n] for n in names]
    n_groups = 2
    assert (tl if nb == 1 else nb) % n_groups == 0 and tl % 8 == 0
    return pl.pallas_call(
        functools.partial(_post_kernel, nb=nb, tl=tl, alpha=alpha, fb=256, n_groups=n_groups),
        grid=(b // nb, t // tl),
        in_specs=[row(d), row(hw), row(hw), mem_spec, mem_spec, cst_spec] + [_const_spec(w.shape) for w in weights],
        out_specs=[row(d), cst_spec],
        out_shape=[jax.ShapeDtypeStruct((b, t, d), F32), jax.ShapeDtypeStruct((b, CONV_W - 1, two_ff), F32)],
        scratch_shapes=[pltpu.VMEM((nb, 8, two_ff), F32), pltpu.VMEM((nb * tl, two_ff // 2), BF16)],
        compiler_params=pltpu.CompilerParams(
            dimension_semantics=("parallel", "arbitrary"), vmem_limit_bytes=VMEM_LIMIT),
        name="post",
    )(x, ro, so, mk_b, mv_b, conv_past, *weights)


def _rope_tables(first, n):
    half = RET_D // 2
    inv = 1.0 / (ROPE_BASE ** (np.arange(half, dtype=np.float64) / half))
    ang = np.arange(first, first + n, dtype=np.float64)[:, None] * inv[None, :]
    c, s = np.cos(ang).astype(np.float32), np.sin(ang).astype(np.float32)
    return jnp.asarray(np.concatenate([c, c], axis=-1)), jnp.asarray(np.concatenate([-s, s], axis=-1))


def kernel(x_prompt, x_sample, cache_sb_k, cache_sb_v, state_ret, state_ffn_conv, cache_mem_k, cache_mem_v, mem_prompt, w_in, w_o, ln1_g, ln1_b, w_q_mem, w_k_mem, w_v_mem, w_o_mem, ln2_g, ln2_b, w_up, conv_w, conv_b, w_down, ln3_g, ln3_b):
    depth = w_in.shape[0]
    alpha = (2.0 * depth) ** 0.25
    xp, xs = x_prompt, x_sample
    bp, tp, _ = xp.shape
    bs, ts, _ = xs.shape
    past_len = cache_sb_k.shape[3]
    two_ff = w_up.shape[2]
    cos_p, sin_p = _rope_tables(0, tp)
    cos_s, sin_s = _rope_tables(past_len, ts)
    row2 = lambda a: a.reshape(1, -1)
    swap = lambda a: jnp.swapaxes(a, -1, -2)
    outs = [[] for _ in range(10)]
    for l in range(depth):
        lw = {"w_o": w_o[l].astype(BF16), "ln1_g": row2(ln1_g[l]), "ln1_b": row2(ln1_b[l]),
              "w_q_mem": w_q_mem[l].astype(BF16), "w_o_mem": w_o_mem[l].astype(BF16),
              "ln2_g": row2(ln2_g[l]), "ln2_b": row2(ln2_b[l]),
              "w_up": w_up[l].astype(BF16), "conv_w": conv_w[l], "conv_b": row2(conv_b[l]),
              "w_down": w_down[l].astype(BF16), "ln3_g": row2(ln3_g[l]), "ln3_b": row2(ln3_b[l])}
        w_in_b = w_in[l].astype(BF16)

        mk, mv, mk_b, mv_b = _memkv(mem_prompt, w_k_mem[l].astype(BF16), w_v_mem[l].astype(BF16))
        ret, sq, skb, svb, pk_t, pv_t = _proj(xp, w_in_b, cos_p, sin_p, nb=1, tl=512, head_split_q=False)
        ro, ps = _retention(ret, None, c=256, n_chunks=2)
        so = _sb_prompt(sq, skb, svb, tq=256)
        xp, pc = _post(xp, ro, so, mk_b, mv_b, jnp.zeros((bp, CONV_W - 1, two_ff), F32), lw,
                       nb=1, tl=512, alpha=alpha)

        ret, sqh, sk, sv = _proj(xs, w_in_b, cos_s, sin_s, nb=bs, tl=ts, head_split_q=True)
        ro, ss = _retention(ret, state_ret[l], c=min(REF_CHUNK, ts), n_chunks=1)
        so = _sb_sample(sqh, sk, sv, swap(cache_sb_k[l]), swap(cache_sb_v[l]), tk=256)
        xs, sc = _post(xs, ro, so, cache_mem_k[l].astype(BF16), cache_mem_v[l].astype(BF16),
                       state_ffn_conv[l], lw, nb=min(bs, 8), tl=ts, alpha=alpha)
        for lst, val in zip(outs, (swap(pk_t), swap(pv_t), ps, pc, mk, mv, sk, sv, ss, sc)):
            lst.append(val)
    return (xp, xs) + tuple(jnp.stack(o) for o in outs)
```

```python
import functools
import math

import jax
import jax.numpy as jnp
import numpy as np
from jax import lax
from jax.experimental import pallas as pl
from jax.experimental.pallas import tpu as pltpu

F32 = jnp.float32
BF16 = jnp.bfloat16

RET_HEADS = 4
RET_D = 128
SB_HEADS = 8
SB_DH = 64
MEM_HEADS = 4
CONV_W = 3
ROPE_BASE = 10000.0
LN_EPS = 1e-5
RMS_EPS = 1e-6
LOG2E = math.log2(math.e)
REF_CHUNK = 64

VMEM_LIMIT = 56 * 1024 * 1024


def _const_spec(shape):
    nd = len(shape)
    return pl.BlockSpec(shape, lambda *_: (0,) * nd, pipeline_mode=pl.Buffered(1))


def _ln(x, g, b):
    mu = jnp.mean(x, -1, keepdims=True)
    xc = x - mu
    var = jnp.mean(xc * xc, -1, keepdims=True)
    return xc * lax.rsqrt(var + LN_EPS) * g + b


def _silu(x):
    return x * (1.0 / (1.0 + jnp.exp(-x)))


def _dot(a, b):
    return jnp.dot(a, b, preferred_element_type=F32)


def _dot_nt(a, b):
    return lax.dot_general(a, b, (((1,), (1,)), ((), ())), preferred_element_type=F32)


def _ret_log_gamma(h):
    return math.log1p(-(2.0 ** (-5.0 - h)))


def _ret_init(s_scr, dmask_scr, s0, c):
    s_scr[...] = jnp.zeros_like(s_scr) if s0 is None else s0
    row = lax.broadcasted_iota(jnp.int32, (c, c), 0)
    col = lax.broadcasted_iota(jnp.int32, (c, c), 1)
    diff = (row - col).astype(F32)
    for h in range(RET_HEADS):
        dmask_scr[h] = jnp.where(diff >= 0, jnp.exp(_ret_log_gamma(h) * jnp.maximum(diff, 0.0)), 0.0)


def _ret_chunk(q, k, v, g, h, s_scr, dmask_scr):
    c = q.shape[0]
    log_g = _ret_log_gamma(h)
    idx = lax.broadcasted_iota(jnp.int32, (c, 1), 0).astype(F32)
    scores = _dot_nt(q, k) * dmask_scr[h]
    o = _dot(scores.astype(BF16), v)
    s_prev = s_scr[h]
    qd = (q.astype(F32) * jnp.exp(log_g * (idx + 1.0))).astype(BF16)
    o = o + _dot(qd, s_prev.astype(BF16))
    kd = (k.astype(F32) * jnp.exp(log_g * (c - 1.0 - idx))).astype(BF16)
    s_scr[h] = math.exp(log_g * c) * s_prev + _dot(kd.T, v)
    o = o * lax.rsqrt(jnp.mean(o * o, -1, keepdims=True) + RMS_EPS)
    return (o * _silu(g.astype(F32))).astype(BF16)


def _side_cast(weights, n_steps, step_of):
    in_specs, out_specs, out_shapes = [], [], []
    for w in weights:
        rows, cols = w.shape
        n_chunks = max(n for n in range(1, n_steps + 1) if (rows // 16) % n == 0)
        spec = pl.BlockSpec((rows // n_chunks, cols),
                            lambda *idx, last=n_chunks - 1: (jnp.minimum(step_of(*idx), last), 0))
        in_specs.append(spec)
        out_specs.append(spec)
        out_shapes.append(jax.ShapeDtypeStruct(w.shape, BF16))
    return in_specs, out_specs, out_shapes


def _proj_kernel(x_ref, w_ref, cos_ref, sin_ref, *refs, nb, tl, d_model, fused_chunk, n_cast):
    cast_in, refs = refs[:n_cast], refs[n_cast:]
    if fused_chunk is None:
        ret_ref, sqh_ref, sk_ref, sv_ref = refs[:4]
        cast_out = refs[4:4 + n_cast]
    else:
        ro_ref, sf_ref, sq_ref, skb_ref, svb_ref, sk_ref, sv_ref = refs[:7]
        cast_out = refs[7:7 + n_cast]
        s_scr, dmask_scr = refs[7 + n_cast:]
    for src, dst in zip(cast_in, cast_out):
        dst[...] = src[...].astype(BF16)
    m = nb * tl
    xb = x_ref[...].reshape(m, d_model).astype(BF16)
    cos = jnp.broadcast_to(cos_ref[...][None], (nb, tl, RET_D)).reshape(m, RET_D)
    sin = jnp.broadcast_to(sin_ref[...][None], (nb, tl, RET_D)).reshape(m, RET_D)
    hw = RET_HEADS * RET_D

    def seg(i):
        return _dot(xb, w_ref[:, i * hw:(i + 1) * hw])

    def rope(r, scale):
        outs = []
        for h in range(RET_HEADS):
            xh = r[:, h * RET_D:(h + 1) * RET_D]
            o = xh * cos + pltpu.roll(xh, RET_D // 2, axis=1) * sin
            outs.append(o * scale if scale != 1.0 else o)
        return jnp.concatenate(outs, axis=-1)

    ret_in = [rope(seg(0), 1.0).astype(BF16), rope(seg(1), RET_D ** -0.5).astype(BF16),
              seg(2).astype(BF16), seg(3).astype(BF16)]
    if fused_chunk is None:
        for i, val in enumerate(ret_in):
            ret_ref[:, :, i * hw:(i + 1) * hw] = val.reshape(nb, tl, hw)
    else:
        t = pl.program_id(1)

        @pl.when(t == 0)
        def _():
            _ret_init(s_scr, dmask_scr, None, fused_chunk)

        for c0 in range(0, tl, fused_chunk):
            for h in range(RET_HEADS):
                blk = [val[c0:c0 + fused_chunk, h * RET_D:(h + 1) * RET_D] for val in ret_in]
                ro_ref[0, c0:c0 + fused_chunk, h * RET_D:(h + 1) * RET_D] = _ret_chunk(*blk, h, s_scr, dmask_scr)

        @pl.when(t == pl.num_programs(1) - 1)
        def _():
            sf_ref[0] = s_scr[...]

    def put_heads(ref, val):
        for h in range(SB_HEADS):
            ref[:, h, :, :] = val[:, h * SB_DH:(h + 1) * SB_DH].astype(ref.dtype).reshape(nb, tl, SB_DH)

    sq = seg(4) * (SB_DH ** -0.5)
    sk = seg(5)
    sv = seg(6)
    if fused_chunk is None:
        put_heads(sqh_ref, sq)
        put_heads(sk_ref, sk)
        put_heads(sv_ref, sv)
    else:
        sq_ref[...] = sq.astype(BF16).reshape(nb, tl, hw)
        skb_ref[...] = sk.astype(BF16).reshape(nb, tl, hw)
        svb_ref[...] = sv.astype(BF16).reshape(nb, tl, hw)
        sk_ref[0] = sk.T.reshape(SB_HEADS, SB_DH, tl)
        sv_ref[0] = sv.T.reshape(SB_HEADS, SB_DH, tl)


def _proj(x, w_in_b, cos, sin, *, nb, tl, fused_chunk, cast=()):
    b, t, d = x.shape
    hw = RET_HEADS * RET_D
    grid = (b // nb, t // tl)
    cast_in, cast_out, cast_shapes = _side_cast(cast, grid[0] * grid[1], lambda i, j: i * grid[1] + j)
    row_spec = lambda w: pl.BlockSpec((nb, tl, w), lambda i, j: (i, j, 0))
    scratch = []
    if fused_chunk is None:
        hs = lambda dt: jax.ShapeDtypeStruct((b, SB_HEADS, t, SB_DH), dt)
        hs_spec = pl.BlockSpec((nb, SB_HEADS, tl, SB_DH), lambda i, j: (i, 0, j, 0))
        out_shape = [jax.ShapeDtypeStruct((b, t, 4 * hw), BF16), hs(BF16), hs(F32), hs(F32)]
        out_specs = [row_spec(4 * hw)] + [hs_spec] * 3
    else:
        assert nb == 1 and tl % fused_chunk == 0
        state = (RET_HEADS, RET_D, RET_D)
        out_shape = ([jax.ShapeDtypeStruct((b, t, hw), BF16), jax.ShapeDtypeStruct((b,) + state, F32)]
                     + [jax.ShapeDtypeStruct((b, t, hw), BF16)] * 3
                     + [jax.ShapeDtypeStruct((b, SB_HEADS, SB_DH, t), F32)] * 2)
        out_specs = ([row_spec(hw), pl.BlockSpec((1,) + state, lambda i, j: (i, 0, 0, 0))] + [row_spec(hw)] * 3
                     + [pl.BlockSpec((1, SB_HEADS, SB_DH, tl), lambda i, j: (i, 0, 0, j))] * 2)
        scratch = [pltpu.VMEM(state, F32), pltpu.VMEM((RET_HEADS, fused_chunk, fused_chunk), F32)]
    return pl.pallas_call(
        functools.partial(_proj_kernel, nb=nb, tl=tl, d_model=d, fused_chunk=fused_chunk, n_cast=len(cast)),
        grid=grid,
        in_specs=[row_spec(d), _const_spec(w_in_b.shape),
                  pl.BlockSpec((tl, RET_D), lambda i, j: (j, 0)),
                  pl.BlockSpec((tl, RET_D), lambda i, j: (j, 0))] + cast_in,
        out_specs=out_specs + cast_out, out_shape=out_shape + cast_shapes, scratch_shapes=scratch,
        compiler_params=pltpu.CompilerParams(
            dimension_semantics=("arbitrary", "arbitrary"), vmem_limit_bytes=VMEM_LIMIT),
        name="proj",
    )(x, w_in_b, cos, sin, *cast)


def _memkv_kernel(m_ref, wk_ref, wv_ref, *refs, dh, n_cast):
    cast_in = refs[:n_cast]
    mk_ref, mv_ref, mkb_ref, mvb_ref = refs[n_cast:n_cast + 4]
    cast_out = refs[n_cast + 4:]
    for src, dst in zip(cast_in, cast_out):
        dst[...] = src[...].astype(BF16)
    mb = m_ref[0].astype(BF16)
    for w_ref, o_ref, ob_ref in ((wk_ref, mk_ref, mkb_ref), (wv_ref, mv_ref, mvb_ref)):
        r = _dot(mb, w_ref[...].astype(BF16))
        for h in range(MEM_HEADS):
            rh = r[:, h * dh:(h + 1) * dh]
            o_ref[0, h] = rh
            ob_ref[0, h] = rh.astype(BF16)


def _memkv(mem, wk, wv, cast=()):
    b, n, d = mem.shape
    dh = d // MEM_HEADS
    spec = pl.BlockSpec((1, MEM_HEADS, n, dh), lambda i: (i, 0, 0, 0))
    cast_in, cast_out, cast_shapes = _side_cast(cast, b, lambda i: i)
    return pl.pallas_call(
        functools.partial(_memkv_kernel, dh=dh, n_cast=len(cast)),
        grid=(b,),
        in_specs=[pl.BlockSpec((1, n, d), lambda i: (i, 0, 0)), _const_spec(wk.shape), _const_spec(wv.shape)] + cast_in,
        out_specs=[spec] * 4 + cast_out,
        out_shape=[jax.ShapeDtypeStruct((b, MEM_HEADS, n, dh), F32)] * 2
        + [jax.ShapeDtypeStruct((b, MEM_HEADS, n, dh), BF16)] * 2 + cast_shapes,
        compiler_params=pltpu.CompilerParams(dimension_semantics=("arbitrary",), vmem_limit_bytes=VMEM_LIMIT),
        name="memkv",
    )(mem, wk, wv, *cast)


def _ret_kernel(*refs, c, n_chunks, has_s0):
    if has_s0:
        ret_ref, s0_ref, ro_ref, sf_ref, s_scr, dmask_scr = refs
    else:
        ret_ref, ro_ref, sf_ref, s_scr, dmask_scr = refs
    t = pl.program_id(1)

    @pl.when(t == 0)
    def _():
        _ret_init(s_scr, dmask_scr, s0_ref[0] if has_s0 else None, c)

    hw = RET_HEADS * RET_D
    for ci in range(n_chunks):
        rows = slice(ci * c, (ci + 1) * c)
        for h in range(RET_HEADS):
            blk = [ret_ref[0, rows, i * hw + h * RET_D:i * hw + (h + 1) * RET_D] for i in range(4)]
            ro_ref[0, rows, h * RET_D:(h + 1) * RET_D] = _ret_chunk(*blk, h, s_scr, dmask_scr)

    @pl.when(t == pl.num_programs(1) - 1)
    def _():
        sf_ref[0] = s_scr[...]


def _retention(ret, s0, *, c, n_chunks):
    b, t, w = ret.shape
    hw = RET_HEADS * RET_D
    tl = c * n_chunks
    st_spec = pl.BlockSpec((1, RET_HEADS, RET_D, RET_D), lambda i, j: (i, 0, 0, 0))
    in_specs = [pl.BlockSpec((1, tl, w), lambda i, j: (i, j, 0))]
    args = [ret]
    if s0 is not None:
        in_specs.append(st_spec)
        args.append(s0)
    return pl.pallas_call(
        functools.partial(_ret_kernel, c=c, n_chunks=n_chunks, has_s0=s0 is not None),
        grid=(b, t // tl),
        in_specs=in_specs,
        out_specs=[pl.BlockSpec((1, tl, hw), lambda i, j: (i, j, 0)), st_spec],
        out_shape=[jax.ShapeDtypeStruct((b, t, hw), BF16),
                   jax.ShapeDtypeStruct((b, RET_HEADS, RET_D, RET_D), F32)],
        scratch_shapes=[pltpu.VMEM((RET_HEADS, RET_D, RET_D), F32), pltpu.VMEM((RET_HEADS, c, c), F32)],
        compiler_params=pltpu.CompilerParams(
            dimension_semantics=("parallel", "arbitrary"), vmem_limit_bytes=VMEM_LIMIT),
        name="retention",
    )(*args)


SB_SKIP_ABOVE = 106.0


def _suffix_ones(n):
    j = lax.broadcasted_iota(jnp.int32, (n, n), 0)
    s = lax.broadcasted_iota(jnp.int32, (n, n), 1)
    return jnp.where(j >= s, 1.0, 0.0).astype(BF16)


def _sb_weights(z, carry, valid, u):
    sp = jnp.maximum(z, 0.0) + jnp.log(1.0 + jnp.exp2(jnp.abs(z) * -LOG2E))
    if valid is not None:
        sp = jnp.where(valid, sp, 0.0)
    cs = _dot(sp.astype(BF16), u)
    a = jnp.exp((z - carry) - cs)
    if valid is not None:
        a = jnp.where(valid, a, 0.0)
    return a.astype(BF16), carry + cs[:, 0:1]


def _sb_live(carry):
    return (jnp.min(carry) < SB_SKIP_ABOVE).astype(jnp.int32)


def _sb_prompt_kernel(q_ref, k_ref, v_ref, o_ref, *, tq):
    qi = pl.program_id(1)
    lanes = 2 * SB_DH
    n_pairs = q_ref.shape[-1] // lanes
    pair = lambda p: slice(p * lanes, (p + 1) * lanes)
    lo_half = lax.broadcasted_iota(jnp.int32, (1, lanes), 1) < SB_DH
    q2 = []
    for p in range(n_pairs):
        q = q_ref[0, :, pair(p)]
        zero = jnp.zeros_like(q)
        q2.append(jnp.concatenate([jnp.where(lo_half, q, zero), jnp.where(lo_half, zero, q)], axis=0))
    m = 2 * n_pairs * tq
    row = lax.broadcasted_iota(jnp.int32, (m, tq), 0) & (tq - 1)
    col = lax.broadcasted_iota(jnp.int32, (m, tq), 1)
    u = _suffix_ones(tq)

    def tile(kt, acc, carry, valid):
        ks = pl.multiple_of(kt * tq, tq)
        z = jnp.concatenate([_dot_nt(q2[p], k_ref[0, pl.ds(ks, tq), pair(p)]) for p in range(n_pairs)], axis=0)
        a, carry = _sb_weights(z, carry, valid, u)
        pv = jnp.concatenate([_dot(a[p * 2 * tq:(p + 1) * 2 * tq], v_ref[0, pl.ds(ks, tq), pair(p)])
                              for p in range(n_pairs)], axis=0)
        return acc + pv, carry

    acc = jnp.zeros((m, lanes), F32)
    carry = jnp.zeros((m, 1), F32)
    acc, carry = tile(qi, acc, carry, col < row)

    def cond(c):
        return jnp.logical_and(c[0] < qi, c[3] > 0)

    def body(c):
        acc, carry = tile(qi - 1 - c[0], c[1], c[2], None)
        return c[0] + 1, acc, carry, _sb_live(carry)

    _, acc, _, _ = lax.while_loop(cond, body, (jnp.int32(0), acc, carry, _sb_live(carry)))
    o_ref[0] = jnp.concatenate(
        [jnp.where(lo_half, acc[2 * p * tq:(2 * p + 1) * tq], acc[(2 * p + 1) * tq:(2 * p + 2) * tq])
         for p in range(n_pairs)], axis=-1).astype(BF16)


def _sb_prompt(sq, sk, sv, *, tq):
    b, t, w = sq.shape
    q_spec = pl.BlockSpec((1, tq, w), lambda i, j: (i, j, 0))
    kv_spec = pl.BlockSpec((1, t, w), lambda i, j: (i, 0, 0))
    return pl.pallas_call(
        functools.partial(_sb_prompt_kernel, tq=tq),
        grid=(b, t // tq),
        in_specs=[q_spec, kv_spec, kv_spec],
        out_specs=q_spec,
        out_shape=jax.ShapeDtypeStruct((b, t, w), BF16),
        compiler_params=pltpu.CompilerParams(
            dimension_semantics=("parallel", "arbitrary"), vmem_limit_bytes=VMEM_LIMIT),
        name="sb_prompt",
    )(sq, sk, sv)


def _sb_sample_kernel(q_ref, kn_ref, vn_ref, kct_ref, vct_ref, o_ref, acc_scr, carry_scr, *, tl, tk):
    nh = q_ref.shape[1]
    n_tiles = kct_ref.shape[3] // tk
    m = nh * tl

    def per_head(fn):
        return jnp.concatenate([fn(h) for h in range(nh)], axis=0)

    row = lax.broadcasted_iota(jnp.int32, (m, tl), 0) & (tl - 1)
    col = lax.broadcasted_iota(jnp.int32, (m, tl), 1)
    z = per_head(lambda h: _dot_nt(q_ref[0, h], kn_ref[0, h].astype(BF16)))
    a, carry = _sb_weights(z, jnp.zeros((m, 1), F32), col < row, _suffix_ones(tl))
    acc_scr[...] = per_head(lambda h: _dot(a[h * tl:(h + 1) * tl], vn_ref[0, h].astype(BF16)))
    carry_scr[...] = carry
    u = _suffix_ones(tk)
    for j in reversed(range(n_tiles)):
        @pl.when(jnp.min(carry_scr[...]) < SB_SKIP_ABOVE)
        def _(j=j):
            cols = slice(j * tk, (j + 1) * tk)
            z = per_head(lambda h: _dot(q_ref[0, h], kct_ref[0, h, :, cols].astype(BF16)))
            a, carry = _sb_weights(z, carry_scr[...], None, u)
            acc_scr[...] += per_head(
                lambda h: _dot_nt(a[h * tl:(h + 1) * tl], vct_ref[0, h, :, cols].astype(BF16)))
            carry_scr[...] = carry

    acc = acc_scr[...]
    o_ref[0] = jnp.concatenate([acc[h * tl:(h + 1) * tl] for h in range(nh)], axis=-1).astype(BF16)


def _sb_sample(sqh, sk_new, sv_new, k_cache_t, v_cache_t, *, tk):
    b, nh, tl, dh = sqh.shape
    past = k_cache_t.shape[3]
    new_spec = pl.BlockSpec((1, nh, tl, dh), lambda i: (i, 0, 0, 0))
    cache_spec = pl.BlockSpec((1, nh, dh, past), lambda i: (i, 0, 0, 0))
    return pl.pallas_call(
        functools.partial(_sb_sample_kernel, tl=tl, tk=tk),
        grid=(b,),
        in_specs=[new_spec, new_spec, new_spec, cache_spec, cache_spec],
        out_specs=pl.BlockSpec((1, tl, nh * dh), lambda i: (i, 0, 0)),
        out_shape=jax.ShapeDtypeStruct((b, tl, nh * dh), BF16),
        scratch_shapes=[pltpu.VMEM((nh * tl, dh), F32), pltpu.VMEM((nh * tl, 1), F32)],
        compiler_params=pltpu.CompilerParams(dimension_semantics=("parallel",), vmem_limit_bytes=VMEM_LIMIT),
        name="sb_sample",
    )(sqh, sk_new, sv_new, k_cache_t, v_cache_t)


def _post_kernel(x_ref, ro_ref, so_ref, mk_ref, mv_ref, cpast_ref,
                 wo_ref, ln1g_ref, ln1b_ref, wq_ref, wom_ref, ln2g_ref, ln2b_ref,
                 wup_ref, cw_ref, cb_ref, wdn_ref, ln3g_ref, ln3b_ref,
                 y_ref, cst_ref, prev_scr, act_scr, *, nb, tl, alpha, fb, n_groups):
    t = pl.program_id(1)
    m = nb * tl
    sub = CONV_W - 1

    @pl.when(t == 0)
    def _():
        prev_scr[...] = jnp.zeros_like(prev_scr)
        prev_scr[:, 8 - sub:8, :] = cpast_ref[...]

    d = x_ref.shape[-1]
    hw = ro_ref.shape[-1]
    dh = mk_ref.shape[3]
    d_ff = wdn_ref.shape[0]
    if nb == 1:
        r = tl // n_groups
        groups = [[(0, i * r, r)] for i in range(n_groups)]
    else:
        per = nb // n_groups
        groups = [[(b, b * tl, tl) for b in range(i * per, (i + 1) * per)] for i in range(n_groups)]
    x_all = x_ref[...].reshape(m, d)
    ro_all = ro_ref[...].reshape(m, hw)
    so_all = so_ref[...].reshape(m, hw)

    def mixer_and_memory(segs):
        g0, g1 = segs[0][1], segs[-1][1] + segs[-1][2]
        mix_in = jnp.concatenate([ro_all[g0:g1], so_all[g0:g1]], axis=-1)
        x1 = _ln(alpha * x_all[g0:g1] + _dot(mix_in, wo_ref[...]), ln1g_ref[...], ln1b_ref[...])
        q = _dot(x1.astype(BF16), wq_ref[...])
        qb = q.astype(BF16)
        s = jnp.concatenate([_dot_nt(qb[s0 - g0:s0 - g0 + n, h * dh:(h + 1) * dh], mk_ref[b, h])
                             for b, s0, n in segs for h in range(MEM_HEADS)], axis=0) * (dh ** -0.5)
        e = jnp.exp(s - jnp.max(s, -1, keepdims=True))
        p = (e * (1.0 / jnp.sum(e, -1, keepdims=True))).astype(BF16)
        outs, r0 = [], 0
        for b, s0, n in segs:
            heads = []
            for h in range(MEM_HEADS):
                heads.append(_dot(p[r0:r0 + n], mv_ref[b, h]).astype(BF16))
                r0 += n
            outs.append(jnp.concatenate(heads, axis=-1))
        att_in = outs[0] if len(outs) == 1 else jnp.concatenate(outs, axis=0)
        return _ln(alpha * x1 + _dot(att_in, wom_ref[...]), ln2g_ref[...], ln2b_ref[...])

    x2 = [mixer_and_memory(segs) for segs in groups]
    x2b = [v.astype(BF16) for v in x2]

    sub_idx = lax.broadcasted_iota(jnp.int32, (1, 8, fb), 1)

    def conv(u, prev8, c0):
        g = u.shape[0] // 8
        u3 = u.reshape(g, 8, fb)
        ext = jnp.concatenate([prev8[None], u3], axis=0)
        r1 = pltpu.roll(ext, 1, axis=1)
        r2 = pltpu.roll(ext, 2, axis=1)
        u1 = jnp.where(sub_idx < 1, r1[:-1], r1[1:])
        u2 = jnp.where(sub_idx < 2, r2[:-1], r2[1:])
        cols = slice(c0, c0 + fb)
        c = (cb_ref[:, cols][None] + cw_ref[0:1, cols][None] * u2
             + cw_ref[1:2, cols][None] * u1 + cw_ref[2:3, cols][None] * u3)
        return c.reshape(u.shape), u3[g - 1]

    def conv_group(u, segs, g0, c0, chain):
        parts = []
        for b, s0, n in segs:
            prev8 = chain.get(b)
            if prev8 is None:
                prev8 = prev_scr[b, :, c0:c0 + fb]
            c, chain[b] = conv(u[s0 - g0:s0 - g0 + n], prev8, c0)
            parts.append(c)
        return parts[0] if len(parts) == 1 else jnp.concatenate(parts, axis=0)

    for blk in range(d_ff // fb):
        ca, cg = blk * fb, d_ff + blk * fb
        chain_a, chain_g = {}, {}
        for segs, xg in zip(groups, x2b):
            g0, g1 = segs[0][1], segs[-1][1] + segs[-1][2]
            a = conv_group(_dot(xg, wup_ref[:, ca:ca + fb]), segs, g0, ca, chain_a)
            g = conv_group(_dot(xg, wup_ref[:, cg:cg + fb]), segs, g0, cg, chain_g)
            act_scr[g0:g1, ca:ca + fb] = (_silu(a) * g).astype(BF16)
        for b in range(nb):
            prev_scr[b, :, ca:ca + fb] = chain_a[b]
            prev_scr[b, :, cg:cg + fb] = chain_g[b]

    for segs, x2g in zip(groups, x2):
        g0, g1 = segs[0][1], segs[-1][1] + segs[-1][2]
        f = _dot(act_scr[g0:g1, :], wdn_ref[...])
        y = _ln(alpha * x2g + f, ln3g_ref[...], ln3b_ref[...])
        for b, s0, n in segs:
            y_ref[b, s0 - b * tl:s0 - b * tl + n, :] = y[s0 - g0:s0 - g0 + n]
    cst_ref[...] = prev_scr[:, 8 - sub:8, :]


def _post(x, ro, so, mk_b, mv_b, conv_past, lw, *, nb, tl, n_groups, alpha):
    b, t, d = x.shape
    assert nb == 1 or tl == t
    hw = ro.shape[-1]
    two_ff = lw["w_up"].shape[1]
    row = lambda w: pl.BlockSpec((nb, tl, w), lambda i, j: (i, j, 0))
    mem_spec = pl.BlockSpec((nb,) + mk_b.shape[1:], lambda i, j: (i, 0, 0, 0),
                            pipeline_mode=pl.Buffered(1) if nb == b else None)
    cst_spec = pl.BlockSpec((nb, CONV_W - 1, two_ff), lambda i, j: (i, 0, 0))
    names = ("w_o", "ln1_g", "ln1_b", "w_q_mem", "w_o_mem", "ln2_g", "ln2_b",
             "w_up", "conv_w", "conv_b", "w_down", "ln3_g", "ln3_b")
    weights = [lw[n] for n in names]
    assert (tl if nb == 1 else nb) % n_groups == 0 and tl % 8 == 0
    return pl.pallas_call(
        functools.partial(_post_kernel, nb=nb, tl=tl, alpha=alpha, fb=256, n_groups=n_groups),
        grid=(b // nb, t // tl),
        in_specs=[row(d), row(hw), row(hw), mem_spec, mem_spec, cst_spec] + [_const_spec(w.shape) for w in weights],
        out_specs=[row(d), cst_spec],
        out_shape=[jax.ShapeDtypeStruct((b, t, d), F32), jax.ShapeDtypeStruct((b, CONV_W - 1, two_ff), F32)],
        scratch_shapes=[pltpu.VMEM((nb, 8, two_ff), F32), pltpu.VMEM((nb * tl, two_ff // 2), BF16)],
        compiler_params=pltpu.CompilerParams(
            dimension_semantics=("parallel", "arbitrary"), vmem_limit_bytes=VMEM_LIMIT),
        name="post",
    )(x, ro, so, mk_b, mv_b, conv_past, *weights)


def _rope_tables(first, n):
    half = RET_D // 2
    inv = 1.0 / (ROPE_BASE ** (np.arange(half, dtype=np.float64) / half))
    ang = np.arange(first, first + n, dtype=np.float64)[:, None] * inv[None, :]
    c, s = np.cos(ang).astype(np.float32), np.sin(ang).astype(np.float32)
    return jnp.asarray(np.concatenate([c, c], axis=-1)), jnp.asarray(np.concatenate([-s, s], axis=-1))


def kernel(x_prompt, x_sample, cache_sb_k, cache_sb_v, state_ret, state_ffn_conv, cache_mem_k, cache_mem_v, mem_prompt, w_in, w_o, ln1_g, ln1_b, w_q_mem, w_k_mem, w_v_mem, w_o_mem, ln2_g, ln2_b, w_up, conv_w, conv_b, w_down, ln3_g, ln3_b):
    depth = w_in.shape[0]
    alpha = (2.0 * depth) ** 0.25
    xp, xs = x_prompt, x_sample
    bp, tp, _ = xp.shape
    bs, ts, _ = xs.shape
    past_len = cache_sb_k.shape[3]
    two_ff = w_up.shape[2]
    cos_p, sin_p = _rope_tables(0, tp)
    cos_s, sin_s = _rope_tables(past_len, ts)
    row2 = lambda a: a.reshape(1, -1)
    swap = lambda a: jnp.swapaxes(a, -1, -2)
    outs = [[] for _ in range(10)]
    for l in range(depth):
        lw = {"ln1_g": row2(ln1_g[l]), "ln1_b": row2(ln1_b[l]), "ln2_g": row2(ln2_g[l]), "ln2_b": row2(ln2_b[l]),
              "conv_w": conv_w[l], "conv_b": row2(conv_b[l]), "ln3_g": row2(ln3_g[l]), "ln3_b": row2(ln3_b[l])}

        mk, mv, mk_b, mv_b, w_in_b = _memkv(mem_prompt, w_k_mem[l], w_v_mem[l], cast=(w_in[l],))
        late = ("w_o", "w_q_mem", "w_o_mem", "w_up", "w_down")
        ro, ps, sq, skb, svb, pk_t, pv_t, *late_b = _proj(
            xp, w_in_b, cos_p, sin_p, nb=1, tl=512, fused_chunk=256,
            cast=(w_o[l], w_q_mem[l], w_o_mem[l], w_up[l], w_down[l]))
        lw.update(zip(late, late_b))
        so = _sb_prompt(sq, skb, svb, tq=256)
        xp, pc = _post(xp, ro, so, mk_b, mv_b, jnp.zeros((bp, CONV_W - 1, two_ff), F32), lw,
                       nb=1, tl=512, n_groups=2, alpha=alpha)

        ret, sqh, sk, sv = _proj(xs, w_in_b, cos_s, sin_s, nb=bs, tl=ts, fused_chunk=None)
        ro, ss = _retention(ret, state_ret[l], c=min(REF_CHUNK, ts), n_chunks=1)
        so = _sb_sample(sqh, sk, sv, swap(cache_sb_k[l]), swap(cache_sb_v[l]), tk=256)
        xs, sc = _post(xs, ro, so, cache_mem_k[l].astype(BF16), cache_mem_v[l].astype(BF16),
                       state_ffn_conv[l], lw, nb=min(bs, 8), tl=ts, n_groups=1, alpha=alpha)
        for lst, val in zip(outs, (swap(pk_t), swap(pv_t), ps, pc, mk, mv, sk, sv, ss, sc)):
            lst.append(val)
    return (xp, xs) + tuple(jnp.stack(o) for o in outs)
```

```python
import functools
import math

import jax
import jax.numpy as jnp
import numpy as np
from jax import lax
from jax.experimental import pallas as pl
from jax.experimental.pallas import tpu as pltpu

F32 = jnp.float32
BF16 = jnp.bfloat16

RET_HEADS = 4
RET_D = 128
SB_HEADS = 8
SB_DH = 64
MEM_HEADS = 4
CONV_W = 3
ROPE_BASE = 10000.0
LN_EPS = 1e-5
RMS_EPS = 1e-6
LOG2E = math.log2(math.e)
REF_CHUNK = 64

VMEM_LIMIT = 56 * 1024 * 1024


def _const_spec(shape):
    nd = len(shape)
    return pl.BlockSpec(shape, lambda *_: (0,) * nd, pipeline_mode=pl.Buffered(1))


def _ln(x, g, b):
    mu = jnp.mean(x, -1, keepdims=True)
    xc = x - mu
    var = jnp.mean(xc * xc, -1, keepdims=True)
    return xc * lax.rsqrt(var + LN_EPS) * g + b


def _silu(x):
    return x * (1.0 / (1.0 + jnp.exp(-x)))


def _dot(a, b):
    return jnp.dot(a, b, preferred_element_type=F32)


def _dot_nt(a, b):
    return lax.dot_general(a, b, (((1,), (1,)), ((), ())), preferred_element_type=F32)


def _ret_log_gamma(h):
    return math.log1p(-(2.0 ** (-5.0 - h)))


def _ret_init(s_scr, dmask_scr, s0, c):
    s_scr[...] = jnp.zeros_like(s_scr) if s0 is None else s0
    row = lax.broadcasted_iota(jnp.int32, (c, c), 0)
    col = lax.broadcasted_iota(jnp.int32, (c, c), 1)
    diff = (row - col).astype(F32)
    for h in range(RET_HEADS):
        dmask_scr[h] = jnp.where(diff >= 0, jnp.exp(_ret_log_gamma(h) * jnp.maximum(diff, 0.0)), 0.0)


def _ret_chunk(q, k, v, g, h, s_scr, dmask_scr):
    c = q.shape[0]
    log_g = _ret_log_gamma(h)
    idx = lax.broadcasted_iota(jnp.int32, (c, 1), 0).astype(F32)
    scores = _dot_nt(q, k) * dmask_scr[h]
    o = _dot(scores.astype(BF16), v)
    s_prev = s_scr[h]
    qd = (q.astype(F32) * jnp.exp(log_g * (idx + 1.0))).astype(BF16)
    o = o + _dot(qd, s_prev.astype(BF16))
    kd = (k.astype(F32) * jnp.exp(log_g * (c - 1.0 - idx))).astype(BF16)
    s_scr[h] = math.exp(log_g * c) * s_prev + _dot(kd.T, v)
    o = o * lax.rsqrt(jnp.mean(o * o, -1, keepdims=True) + RMS_EPS)
    return (o * _silu(g.astype(F32))).astype(BF16)


def _side_cast(weights, n_steps, step_of):
    in_specs, out_specs, out_shapes = [], [], []
    for w in weights:
        rows, cols = w.shape
        n_chunks = max(n for n in range(1, n_steps + 1) if (rows // 16) % n == 0)
        spec = pl.BlockSpec((rows // n_chunks, cols),
                            lambda *idx, last=n_chunks - 1: (jnp.minimum(step_of(*idx), last), 0))
        in_specs.append(spec)
        out_specs.append(spec)
        out_shapes.append(jax.ShapeDtypeStruct(w.shape, BF16))
    return in_specs, out_specs, out_shapes


def _proj_kernel(x_ref, w_ref, cos_ref, sin_ref, *refs, nb, tl, d_model, fused_chunk, n_cast):
    cast_in, refs = refs[:n_cast], refs[n_cast:]
    if fused_chunk is None:
        ret_ref, sqh_ref, sk_ref, sv_ref = refs[:4]
        cast_out = refs[4:4 + n_cast]
    else:
        ro_ref, sf_ref, sq_ref, skb_ref, svb_ref, sk_ref, sv_ref = refs[:7]
        cast_out = refs[7:7 + n_cast]
        s_scr, dmask_scr = refs[7 + n_cast:]
    for src, dst in zip(cast_in, cast_out):
        dst[...] = src[...].astype(BF16)
    m = nb * tl
    xb = x_ref[...].reshape(m, d_model).astype(BF16)
    cos = jnp.broadcast_to(cos_ref[...][None], (nb, tl, RET_D)).reshape(m, RET_D)
    sin = jnp.broadcast_to(sin_ref[...][None], (nb, tl, RET_D)).reshape(m, RET_D)
    hw = RET_HEADS * RET_D

    def seg(i):
        return _dot(xb, w_ref[:, i * hw:(i + 1) * hw])

    def rope(r, scale):
        outs = []
        for h in range(RET_HEADS):
            xh = r[:, h * RET_D:(h + 1) * RET_D]
            o = xh * cos + pltpu.roll(xh, RET_D // 2, axis=1) * sin
            outs.append(o * scale if scale != 1.0 else o)
        return jnp.concatenate(outs, axis=-1)

    ret_in = [rope(seg(0), 1.0).astype(BF16), rope(seg(1), RET_D ** -0.5).astype(BF16),
              seg(2).astype(BF16), seg(3).astype(BF16)]
    if fused_chunk is None:
        for i, val in enumerate(ret_in):
            ret_ref[:, :, i * hw:(i + 1) * hw] = val.reshape(nb, tl, hw)
    else:
        t = pl.program_id(1)

        @pl.when(t == 0)
        def _():
            _ret_init(s_scr, dmask_scr, None, fused_chunk)

        for c0 in range(0, tl, fused_chunk):
            for h in range(RET_HEADS):
                blk = [val[c0:c0 + fused_chunk, h * RET_D:(h + 1) * RET_D] for val in ret_in]
                ro_ref[0, c0:c0 + fused_chunk, h * RET_D:(h + 1) * RET_D] = _ret_chunk(*blk, h, s_scr, dmask_scr)

        @pl.when(t == pl.num_programs(1) - 1)
        def _():
            sf_ref[0] = s_scr[...]

    def put_heads(ref, val):
        for h in range(SB_HEADS):
            ref[:, h, :, :] = val[:, h * SB_DH:(h + 1) * SB_DH].astype(ref.dtype).reshape(nb, tl, SB_DH)

    sq = seg(4) * (SB_DH ** -0.5)
    sk = seg(5)
    sv = seg(6)
    if fused_chunk is None:
        put_heads(sqh_ref, sq)
        put_heads(sk_ref, sk)
        put_heads(sv_ref, sv)
    else:
        sq_ref[...] = sq.astype(BF16).reshape(nb, tl, hw)
        skb_ref[...] = sk.astype(BF16).reshape(nb, tl, hw)
        svb_ref[...] = sv.astype(BF16).reshape(nb, tl, hw)
        sk_ref[0] = sk.T.reshape(SB_HEADS, SB_DH, tl)
        sv_ref[0] = sv.T.reshape(SB_HEADS, SB_DH, tl)


def _proj(x, w_in_b, cos, sin, *, nb, tl, fused_chunk, cast=()):
    b, t, d = x.shape
    hw = RET_HEADS * RET_D
    grid = (b // nb, t // tl)
    cast_in, cast_out, cast_shapes = _side_cast(cast, grid[0] * grid[1], lambda i, j: i * grid[1] + j)
    row_spec = lambda w: pl.BlockSpec((nb, tl, w), lambda i, j: (i, j, 0))
    scratch = []
    if fused_chunk is None:
        hs = lambda dt: jax.ShapeDtypeStruct((b, SB_HEADS, t, SB_DH), dt)
        hs_spec = pl.BlockSpec((nb, SB_HEADS, tl, SB_DH), lambda i, j: (i, 0, j, 0))
        out_shape = [jax.ShapeDtypeStruct((b, t, 4 * hw), BF16), hs(BF16), hs(F32), hs(F32)]
        out_specs = [row_spec(4 * hw)] + [hs_spec] * 3
    else:
        assert nb == 1 and tl % fused_chunk == 0
        state = (RET_HEADS, RET_D, RET_D)
        out_shape = ([jax.ShapeDtypeStruct((b, t, hw), BF16), jax.ShapeDtypeStruct((b,) + state, F32)]
                     + [jax.ShapeDtypeStruct((b, t, hw), BF16)] * 3
                     + [jax.ShapeDtypeStruct((b, SB_HEADS, SB_DH, t), F32)] * 2)
        out_specs = ([row_spec(hw), pl.BlockSpec((1,) + state, lambda i, j: (i, 0, 0, 0))] + [row_spec(hw)] * 3
                     + [pl.BlockSpec((1, SB_HEADS, SB_DH, tl), lambda i, j: (i, 0, 0, j))] * 2)
        scratch = [pltpu.VMEM(state, F32), pltpu.VMEM((RET_HEADS, fused_chunk, fused_chunk), F32)]
    return pl.pallas_call(
        functools.partial(_proj_kernel, nb=nb, tl=tl, d_model=d, fused_chunk=fused_chunk, n_cast=len(cast)),
        grid=grid,
        in_specs=[row_spec(d), _const_spec(w_in_b.shape),
                  pl.BlockSpec((tl, RET_D), lambda i, j: (j, 0)),
                  pl.BlockSpec((tl, RET_D), lambda i, j: (j, 0))] + cast_in,
        out_specs=out_specs + cast_out, out_shape=out_shape + cast_shapes, scratch_shapes=scratch,
        compiler_params=pltpu.CompilerParams(
            dimension_semantics=("arbitrary", "arbitrary"), vmem_limit_bytes=VMEM_LIMIT),
        name="proj",
    )(x, w_in_b, cos, sin, *cast)


def _memkv_kernel(m_ref, wk_ref, wv_ref, *refs, dh, n_cast):
    cast_in = refs[:n_cast]
    mk_ref, mv_ref, mkb_ref, mvb_ref = refs[n_cast:n_cast + 4]
    cast_out = refs[n_cast + 4:]
    for src, dst in zip(cast_in, cast_out):
        dst[...] = src[...].astype(BF16)
    mb = m_ref[0].astype(BF16)
    for w_ref, o_ref, ob_ref in ((wk_ref, mk_ref, mkb_ref), (wv_ref, mv_ref, mvb_ref)):
        r = _dot(mb, w_ref[...].astype(BF16))
        for h in range(MEM_HEADS):
            rh = r[:, h * dh:(h + 1) * dh]
            o_ref[0, h] = rh
            ob_ref[0, h] = rh.astype(BF16)


def _memkv(mem, wk, wv, cast=()):
    b, n, d = mem.shape
    dh = d // MEM_HEADS
    spec = pl.BlockSpec((1, MEM_HEADS, n, dh), lambda i: (i, 0, 0, 0))
    cast_in, cast_out, cast_shapes = _side_cast(cast, b, lambda i: i)
    return pl.pallas_call(
        functools.partial(_memkv_kernel, dh=dh, n_cast=len(cast)),
        grid=(b,),
        in_specs=[pl.BlockSpec((1, n, d), lambda i: (i, 0, 0)), _const_spec(wk.shape), _const_spec(wv.shape)] + cast_in,
        out_specs=[spec] * 4 + cast_out,
        out_shape=[jax.ShapeDtypeStruct((b, MEM_HEADS, n, dh), F32)] * 2
        + [jax.ShapeDtypeStruct((b, MEM_HEADS, n, dh), BF16)] * 2 + cast_shapes,
        compiler_params=pltpu.CompilerParams(dimension_semantics=("arbitrary",), vmem_limit_bytes=VMEM_LIMIT),
        name="memkv",
    )(mem, wk, wv, *cast)


def _ret_kernel(ret_ref, s0_ref, *refs, c, n_chunks, n_cast):
    cast_in = refs[:n_cast]
    ro_ref, sf_ref = refs[n_cast:n_cast + 2]
    cast_out = refs[n_cast + 2:2 * n_cast + 2]
    s_scr, dmask_scr = refs[2 * n_cast + 2:]
    for src, dst in zip(cast_in, cast_out):
        dst[...] = src[...].astype(BF16)
    t = pl.program_id(1)

    @pl.when(t == 0)
    def _():
        _ret_init(s_scr, dmask_scr, s0_ref[0], c)

    hw = RET_HEADS * RET_D
    for ci in range(n_chunks):
        rows = slice(ci * c, (ci + 1) * c)
        for h in range(RET_HEADS):
            blk = [ret_ref[0, rows, i * hw + h * RET_D:i * hw + (h + 1) * RET_D] for i in range(4)]
            ro_ref[0, rows, h * RET_D:(h + 1) * RET_D] = _ret_chunk(*blk, h, s_scr, dmask_scr)

    @pl.when(t == pl.num_programs(1) - 1)
    def _():
        sf_ref[0] = s_scr[...]


def _retention(ret, s0, *, c, n_chunks, cast=()):
    b, t, w = ret.shape
    hw = RET_HEADS * RET_D
    tl = c * n_chunks
    grid = (b, t // tl)
    st_spec = pl.BlockSpec((1, RET_HEADS, RET_D, RET_D), lambda i, j: (i, 0, 0, 0))
    cast_in, cast_out, cast_shapes = _side_cast(cast, grid[0] * grid[1], lambda i, j: i * grid[1] + j)
    return pl.pallas_call(
        functools.partial(_ret_kernel, c=c, n_chunks=n_chunks, n_cast=len(cast)),
        grid=grid,
        in_specs=[pl.BlockSpec((1, tl, w), lambda i, j: (i, j, 0)), st_spec] + cast_in,
        out_specs=[pl.BlockSpec((1, tl, hw), lambda i, j: (i, j, 0)), st_spec] + cast_out,
        out_shape=[jax.ShapeDtypeStruct((b, t, hw), BF16),
                   jax.ShapeDtypeStruct((b, RET_HEADS, RET_D, RET_D), F32)] + cast_shapes,
        scratch_shapes=[pltpu.VMEM((RET_HEADS, RET_D, RET_D), F32), pltpu.VMEM((RET_HEADS, c, c), F32)],
        compiler_params=pltpu.CompilerParams(
            dimension_semantics=("arbitrary", "arbitrary"), vmem_limit_bytes=VMEM_LIMIT),
        name="retention",
    )(ret, s0, *cast)


SB_SKIP_ABOVE = 106.0


def _suffix_ones(n):
    j = lax.broadcasted_iota(jnp.int32, (n, n), 0)
    s = lax.broadcasted_iota(jnp.int32, (n, n), 1)
    return jnp.where(j >= s, 1.0, 0.0).astype(BF16)


def _sb_weights(z, carry, valid, u):
    sp = jnp.maximum(z, 0.0) + jnp.log(1.0 + jnp.exp2(jnp.abs(z) * -LOG2E))
    if valid is not None:
        sp = jnp.where(valid, sp, 0.0)
    run = _dot(sp.astype(BF16), u) + carry
    a = jnp.exp(z - run)
    if valid is not None:
        a = jnp.where(valid, a, 0.0)
    return a.astype(BF16), run[:, 0:1]


def _sb_live(carry):
    return (jnp.min(carry) < SB_SKIP_ABOVE).astype(jnp.int32)


def _sb_prompt_kernel(q_ref, k_ref, v_ref, o_ref, *, tq):
    qi = pl.program_id(1)
    lanes = 2 * SB_DH
    n_pairs = q_ref.shape[-1] // lanes
    pair = lambda p: slice(p * lanes, (p + 1) * lanes)
    lo_half = lax.broadcasted_iota(jnp.int32, (1, lanes), 1) < SB_DH
    q2 = []
    for p in range(n_pairs):
        q = q_ref[0, :, pair(p)]
        zero = jnp.zeros_like(q)
        q2.append(jnp.concatenate([jnp.where(lo_half, q, zero), jnp.where(lo_half, zero, q)], axis=0))
    m = 2 * n_pairs * tq
    row = lax.broadcasted_iota(jnp.int32, (m, tq), 0) & (tq - 1)
    col = lax.broadcasted_iota(jnp.int32, (m, tq), 1)
    u = _suffix_ones(tq)

    def tile(kt, acc, carry, valid):
        ks = pl.multiple_of(kt * tq, tq)
        z = jnp.concatenate([_dot_nt(q2[p], k_ref[0, pl.ds(ks, tq), pair(p)]) for p in range(n_pairs)], axis=0)
        a, carry = _sb_weights(z, carry, valid, u)
        pv = jnp.concatenate([_dot(a[p * 2 * tq:(p + 1) * 2 * tq], v_ref[0, pl.ds(ks, tq), pair(p)])
                              for p in range(n_pairs)], axis=0)
        return acc + pv, carry

    acc = jnp.zeros((m, lanes), F32)
    carry = jnp.zeros((m, 1), F32)
    acc, carry = tile(qi, acc, carry, col < row)
    no_tile = jnp.where(qi == 0, 1e30, 0.0).astype(F32)
    acc, carry = tile(jnp.maximum(qi - 1, 0), acc, carry + no_tile, None)

    def cond(c):
        return jnp.logical_and(c[0] < qi, c[3] > 0)

    def body(c):
        acc, carry = tile(qi - 1 - c[0], c[1], c[2], None)
        return c[0] + 1, acc, carry, _sb_live(carry)

    _, acc, _, _ = lax.while_loop(cond, body, (jnp.int32(1), acc, carry, _sb_live(carry)))
    o_ref[0] = jnp.concatenate(
        [jnp.where(lo_half, acc[2 * p * tq:(2 * p + 1) * tq], acc[(2 * p + 1) * tq:(2 * p + 2) * tq])
         for p in range(n_pairs)], axis=-1).astype(BF16)


def _sb_prompt(sq, sk, sv, *, tq):
    b, t, w = sq.shape
    q_spec = pl.BlockSpec((1, tq, w), lambda i, j: (i, j, 0))
    kv_spec = pl.BlockSpec((1, t, w), lambda i, j: (i, 0, 0))
    return pl.pallas_call(
        functools.partial(_sb_prompt_kernel, tq=tq),
        grid=(b, t // tq),
        in_specs=[q_spec, kv_spec, kv_spec],
        out_specs=q_spec,
        out_shape=jax.ShapeDtypeStruct((b, t, w), BF16),
        compiler_params=pltpu.CompilerParams(
            dimension_semantics=("parallel", "arbitrary"), vmem_limit_bytes=VMEM_LIMIT),
        name="sb_prompt",
    )(sq, sk, sv)


def _sb_sample_kernel(q_ref, kn_ref, vn_ref, kct_ref, vct_ref, o_ref, acc_scr, carry_scr, *, tl, tk):
    nh = q_ref.shape[1]
    n_tiles = kct_ref.shape[3] // tk
    m = nh * tl

    def per_head(fn):
        return jnp.concatenate([fn(h) for h in range(nh)], axis=0)

    row = lax.broadcasted_iota(jnp.int32, (m, tl), 0) & (tl - 1)
    col = lax.broadcasted_iota(jnp.int32, (m, tl), 1)
    z = per_head(lambda h: _dot_nt(q_ref[0, h], kn_ref[0, h].astype(BF16)))
    a, carry = _sb_weights(z, jnp.zeros((m, 1), F32), col < row, _suffix_ones(tl))
    acc_scr[...] = per_head(lambda h: _dot(a[h * tl:(h + 1) * tl], vn_ref[0, h].astype(BF16)))
    carry_scr[...] = carry
    u = _suffix_ones(tk)
    for j in reversed(range(n_tiles)):
        @pl.when(jnp.min(carry_scr[...]) < SB_SKIP_ABOVE)
        def _(j=j):
            cols = slice(j * tk, (j + 1) * tk)
            z = per_head(lambda h: _dot(q_ref[0, h], kct_ref[0, h, :, cols].astype(BF16)))
            a, carry = _sb_weights(z, carry_scr[...], None, u)
            acc_scr[...] += per_head(
                lambda h: _dot_nt(a[h * tl:(h + 1) * tl], vct_ref[0, h, :, cols].astype(BF16)))
            carry_scr[...] = carry

    acc = acc_scr[...]
    o_ref[0] = jnp.concatenate([acc[h * tl:(h + 1) * tl] for h in range(nh)], axis=-1).astype(BF16)


def _sb_sample(sqh, sk_new, sv_new, k_cache_t, v_cache_t, *, tk):
    b, nh, tl, dh = sqh.shape
    past = k_cache_t.shape[3]
    new_spec = pl.BlockSpec((1, nh, tl, dh), lambda i: (i, 0, 0, 0))
    cache_spec = pl.BlockSpec((1, nh, dh, past), lambda i: (i, 0, 0, 0))
    return pl.pallas_call(
        functools.partial(_sb_sample_kernel, tl=tl, tk=tk),
        grid=(b,),
        in_specs=[new_spec, new_spec, new_spec, cache_spec, cache_spec],
        out_specs=pl.BlockSpec((1, tl, nh * dh), lambda i: (i, 0, 0)),
        out_shape=jax.ShapeDtypeStruct((b, tl, nh * dh), BF16),
        scratch_shapes=[pltpu.VMEM((nh * tl, dh), F32), pltpu.VMEM((nh * tl, 1), F32)],
        compiler_params=pltpu.CompilerParams(dimension_semantics=("parallel",), vmem_limit_bytes=VMEM_LIMIT),
        name="sb_sample",
    )(sqh, sk_new, sv_new, k_cache_t, v_cache_t)


def _post_kernel(x_ref, ro_ref, so_ref, mk_ref, mv_ref, cpast_ref,
                 wo_ref, ln1g_ref, ln1b_ref, wq_ref, wom_ref, ln2g_ref, ln2b_ref,
                 wup_ref, cw_ref, cb_ref, wdn_ref, ln3g_ref, ln3b_ref,
                 y_ref, cst_ref, prev_scr, act_scr, *, nb, tl, alpha, fb, n_groups):
    t = pl.program_id(1)
    m = nb * tl
    sub = CONV_W - 1

    @pl.when(t == 0)
    def _():
        prev_scr[...] = jnp.zeros_like(prev_scr)
        prev_scr[:, 8 - sub:8, :] = cpast_ref[...]

    d = x_ref.shape[-1]
    hw = ro_ref.shape[-1]
    dh = mk_ref.shape[3]
    d_ff = wdn_ref.shape[0]
    if nb == 1:
        r = tl // n_groups
        groups = [[(0, i * r, r)] for i in range(n_groups)]
    else:
        per = nb // n_groups
        groups = [[(b, b * tl, tl) for b in range(i * per, (i + 1) * per)] for i in range(n_groups)]
    x_all = x_ref[...].reshape(m, d)
    ro_all = ro_ref[...].reshape(m, hw)
    so_all = so_ref[...].reshape(m, hw)

    def mixer_and_memory(segs):
        g0, g1 = segs[0][1], segs[-1][1] + segs[-1][2]
        mix_in = jnp.concatenate([ro_all[g0:g1], so_all[g0:g1]], axis=-1)
        x1 = _ln(alpha * x_all[g0:g1] + _dot(mix_in, wo_ref[...]), ln1g_ref[...], ln1b_ref[...])
        q = _dot(x1.astype(BF16), wq_ref[...])
        qb = q.astype(BF16)
        s = jnp.concatenate([_dot_nt(qb[s0 - g0:s0 - g0 + n, h * dh:(h + 1) * dh], mk_ref[b, h])
                             for b, s0, n in segs for h in range(MEM_HEADS)], axis=0) * (dh ** -0.5)
        e = jnp.exp(s - jnp.max(s, -1, keepdims=True))
        p = (e * (1.0 / jnp.sum(e, -1, keepdims=True))).astype(BF16)
        outs, r0 = [], 0
        for b, s0, n in segs:
            heads = []
            for h in range(MEM_HEADS):
                heads.append(_dot(p[r0:r0 + n], mv_ref[b, h]).astype(BF16))
                r0 += n
            outs.append(jnp.concatenate(heads, axis=-1))
        att_in = outs[0] if len(outs) == 1 else jnp.concatenate(outs, axis=0)
        return _ln(alpha * x1 + _dot(att_in, wom_ref[...]), ln2g_ref[...], ln2b_ref[...])

    x2 = [mixer_and_memory(segs) for segs in groups]
    x2b = [v.astype(BF16) for v in x2]

    sub_idx = lax.broadcasted_iota(jnp.int32, (1, 8, fb), 1)

    def conv(u, prev8, c0):
        g = u.shape[0] // 8
        u3 = u.reshape(g, 8, fb)
        ext = jnp.concatenate([prev8[None], u3], axis=0)
        r1 = pltpu.roll(ext, 1, axis=1)
        r2 = pltpu.roll(ext, 2, axis=1)
        u1 = jnp.where(sub_idx < 1, r1[:-1], r1[1:])
        u2 = jnp.where(sub_idx < 2, r2[:-1], r2[1:])
        cols = slice(c0, c0 + fb)
        c = (cb_ref[:, cols][None] + cw_ref[0:1, cols][None] * u2
             + cw_ref[1:2, cols][None] * u1 + cw_ref[2:3, cols][None] * u3)
        return c.reshape(u.shape), u3[g - 1]

    def conv_group(u, segs, g0, c0, chain):
        parts = []
        for b, s0, n in segs:
            prev8 = chain.get(b)
            if prev8 is None:
                prev8 = prev_scr[b, :, c0:c0 + fb]
            c, chain[b] = conv(u[s0 - g0:s0 - g0 + n], prev8, c0)
            parts.append(c)
        return parts[0] if len(parts) == 1 else jnp.concatenate(parts, axis=0)

    for blk in range(d_ff // fb):
        ca, cg = blk * fb, d_ff + blk * fb
        chain_a, chain_g = {}, {}
        for segs, xg in zip(groups, x2b):
            g0, g1 = segs[0][1], segs[-1][1] + segs[-1][2]
            a = conv_group(_dot(xg, wup_ref[:, ca:ca + fb]), segs, g0, ca, chain_a)
            g = conv_group(_dot(xg, wup_ref[:, cg:cg + fb]), segs, g0, cg, chain_g)
            act_scr[g0:g1, ca:ca + fb] = (_silu(a) * g).astype(BF16)
        for b in range(nb):
            prev_scr[b, :, ca:ca + fb] = chain_a[b]
            prev_scr[b, :, cg:cg + fb] = chain_g[b]

    for segs, x2g in zip(groups, x2):
        g0, g1 = segs[0][1], segs[-1][1] + segs[-1][2]
        f = _dot(act_scr[g0:g1, :], wdn_ref[...])
        y = _ln(alpha * x2g + f, ln3g_ref[...], ln3b_ref[...])
        for b, s0, n in segs:
            y_ref[b, s0 - b * tl:s0 - b * tl + n, :] = y[s0 - g0:s0 - g0 + n]
    cst_ref[...] = prev_scr[:, 8 - sub:8, :]


def _post(x, ro, so, mk_b, mv_b, conv_past, lw, *, nb, tl, n_groups, alpha):
    b, t, d = x.shape
    assert nb == 1 or tl == t
    hw = ro.shape[-1]
    two_ff = lw["w_up"].shape[1]
    row = lambda w: pl.BlockSpec((nb, tl, w), lambda i, j: (i, j, 0))
    mem_spec = pl.BlockSpec((nb,) + mk_b.shape[1:], lambda i, j: (i, 0, 0, 0),
                            pipeline_mode=pl.Buffered(1) if nb == b else None)
    cst_spec = pl.BlockSpec((nb, CONV_W - 1, two_ff), lambda i, j: (i, 0, 0))
    names = ("w_o", "ln1_g", "ln1_b", "w_q_mem", "w_o_mem", "ln2_g", "ln2_b",
             "w_up", "conv_w", "conv_b", "w_down", "ln3_g", "ln3_b")
    weights = [lw[n] for n in names]
    assert (tl if nb == 1 else nb) % n_groups == 0 and tl % 8 == 0
    return pl.pallas_call(
        functools.partial(_post_kernel, nb=nb, tl=tl, alpha=alpha, fb=256, n_groups=n_groups),
        grid=(b // nb, t // tl),
        in_specs=[row(d), row(hw), row(hw), mem_spec, mem_spec, cst_spec] + [_const_spec(w.shape) for w in weights],
        out_specs=[row(d), cst_spec],
        out_shape=[jax.ShapeDtypeStruct((b, t, d), F32), jax.ShapeDtypeStruct((b, CONV_W - 1, two_ff), F32)],
        scratch_shapes=[pltpu.VMEM((nb, 8, two_ff), F32), pltpu.VMEM((nb * tl, two_ff // 2), BF16)],
        compiler_params=pltpu.CompilerParams(
            dimension_semantics=("parallel", "arbitrary"), vmem_limit_bytes=VMEM_LIMIT),
        name="post",
    )(x, ro, so, mk_b, mv_b, conv_past, *weights)


def _rope_tables(first, n):
    half = RET_D // 2
    inv = 1.0 / (ROPE_BASE ** (np.arange(half, dtype=np.float64) / half))
    ang = np.arange(first, first + n, dtype=np.float64)[:, None] * inv[None, :]
    c, s = np.cos(ang).astype(np.float32), np.sin(ang).astype(np.float32)
    return jnp.asarray(np.concatenate([c, c], axis=-1)), jnp.asarray(np.concatenate([-s, s], axis=-1))


def kernel(x_prompt, x_sample, cache_sb_k, cache_sb_v, state_ret, state_ffn_conv, cache_mem_k, cache_mem_v, mem_prompt, w_in, w_o, ln1_g, ln1_b, w_q_mem, w_k_mem, w_v_mem, w_o_mem, ln2_g, ln2_b, w_up, conv_w, conv_b, w_down, ln3_g, ln3_b):
    depth = w_in.shape[0]
    alpha = (2.0 * depth) ** 0.25
    xp, xs = x_prompt, x_sample
    bp, tp, _ = xp.shape
    bs, ts, _ = xs.shape
    past_len = cache_sb_k.shape[3]
    two_ff = w_up.shape[2]
    cos_p, sin_p = _rope_tables(0, tp)
    cos_s, sin_s = _rope_tables(past_len, ts)
    row2 = lambda a: a.reshape(1, -1)
    swap = lambda a: jnp.swapaxes(a, -1, -2)
    outs = [[] for _ in range(10)]
    for l in range(depth):
        lw = {"ln1_g": row2(ln1_g[l]), "ln1_b": row2(ln1_b[l]), "ln2_g": row2(ln2_g[l]), "ln2_b": row2(ln2_b[l]),
              "conv_w": conv_w[l], "conv_b": row2(conv_b[l]), "ln3_g": row2(ln3_g[l]), "ln3_b": row2(ln3_b[l])}

        mk, mv, mk_b, mv_b, w_in_b = _memkv(mem_prompt, w_k_mem[l], w_v_mem[l], cast=(w_in[l],))
        late = ("w_o", "w_q_mem", "w_o_mem", "w_up", "w_down")
        ro, ps, sq, skb, svb, pk_t, pv_t, *late_b = _proj(
            xp, w_in_b, cos_p, sin_p, nb=1, tl=512, fused_chunk=256,
            cast=(w_o[l], w_q_mem[l], w_o_mem[l], w_up[l], w_down[l]))
        lw.update(zip(late, late_b))
        so = _sb_prompt(sq, skb, svb, tq=256)
        xp, pc = _post(xp, ro, so, mk_b, mv_b, jnp.zeros((bp, CONV_W - 1, two_ff), F32), lw,
                       nb=1, tl=512, n_groups=2, alpha=alpha)

        ret, sqh, sk, sv = _proj(xs, w_in_b, cos_s, sin_s, nb=bs, tl=ts, fused_chunk=None)
        mem_shape = cache_mem_k.shape[1:]
        flat = lambda a: a.reshape(-1, mem_shape[-1])
        ro, ss, cmk_b, cmv_b = _retention(ret, state_ret[l], c=min(REF_CHUNK, ts), n_chunks=1,
                                          cast=(flat(cache_mem_k[l]), flat(cache_mem_v[l])))
        so = _sb_sample(sqh, sk, sv, swap(cache_sb_k[l]), swap(cache_sb_v[l]), tk=256)
        xs, sc = _post(xs, ro, so, cmk_b.reshape(mem_shape), cmv_b.reshape(mem_shape),
                       state_ffn_conv[l], lw, nb=min(bs, 8), tl=ts, n_groups=1, alpha=alpha)
        for lst, val in zip(outs, (swap(pk_t), swap(pv_t), ps, pc, mk, mv, sk, sv, ss, sc)):
            lst.append(val)
    return (xp, xs) + tuple(jnp.stack(o) for o in outs)
```

```python
import functools
import math

import jax
import jax.numpy as jnp
import numpy as np
from jax import lax
from jax.experimental import pallas as pl
from jax.experimental.pallas import tpu as pltpu

F32 = jnp.float32
BF16 = jnp.bfloat16

RET_HEADS = 4
RET_D = 128
SB_HEADS = 8
SB_DH = 64
MEM_HEADS = 4
CONV_W = 3
ROPE_BASE = 10000.0
LN_EPS = 1e-5
RMS_EPS = 1e-6
LOG2E = math.log2(math.e)
REF_CHUNK = 64

VMEM_LIMIT = 56 * 1024 * 1024


def _const_spec(shape):
    nd = len(shape)
    return pl.BlockSpec(shape, lambda *_: (0,) * nd, pipeline_mode=pl.Buffered(1))


def _ln(x, g, b):
    mu = jnp.mean(x, -1, keepdims=True)
    xc = x - mu
    var = jnp.mean(xc * xc, -1, keepdims=True)
    return xc * lax.rsqrt(var + LN_EPS) * g + b


def _silu(x):
    return x * (1.0 / (1.0 + jnp.exp(-x)))


def _dot(a, b):
    return jnp.dot(a, b, preferred_element_type=F32)


def _dot_nt(a, b):
    return lax.dot_general(a, b, (((1,), (1,)), ((), ())), preferred_element_type=F32)


def _ret_log_gamma(h):
    return math.log1p(-(2.0 ** (-5.0 - h)))


def _ret_init(s_scr, dmask_scr, s0, c):
    s_scr[...] = jnp.zeros_like(s_scr) if s0 is None else s0
    row = lax.broadcasted_iota(jnp.int32, (c, c), 0)
    col = lax.broadcasted_iota(jnp.int32, (c, c), 1)
    diff = (row - col).astype(F32)
    for h in range(RET_HEADS):
        dmask_scr[h] = jnp.where(diff >= 0, jnp.exp(_ret_log_gamma(h) * jnp.maximum(diff, 0.0)), 0.0)


def _ret_chunk(q, k, v, g, h, s_scr, dmask_scr):
    c = q.shape[0]
    log_g = _ret_log_gamma(h)
    idx = lax.broadcasted_iota(jnp.int32, (c, 1), 0).astype(F32)
    scores = _dot_nt(q, k) * dmask_scr[h]
    o = _dot(scores.astype(BF16), v)
    s_prev = s_scr[h]
    qd = (q.astype(F32) * jnp.exp(log_g * (idx + 1.0))).astype(BF16)
    o = o + _dot(qd, s_prev.astype(BF16))
    kd = (k.astype(F32) * jnp.exp(log_g * (c - 1.0 - idx))).astype(BF16)
    s_scr[h] = math.exp(log_g * c) * s_prev + _dot(kd.T, v)
    o = o * lax.rsqrt(jnp.mean(o * o, -1, keepdims=True) + RMS_EPS)
    return (o * _silu(g.astype(F32))).astype(BF16)


def _side_cast(weights, n_steps, step_of):
    in_specs, out_specs, out_shapes = [], [], []
    for w in weights:
        rows, cols = w.shape
        n_chunks = max(n for n in range(1, n_steps + 1) if (rows // 16) % n == 0)
        spec = pl.BlockSpec((rows // n_chunks, cols),
                            lambda *idx, last=n_chunks - 1: (jnp.minimum(step_of(*idx), last), 0))
        in_specs.append(spec)
        out_specs.append(spec)
        out_shapes.append(jax.ShapeDtypeStruct(w.shape, BF16))
    return in_specs, out_specs, out_shapes


def _proj_kernel(x_ref, w_ref, cos_ref, sin_ref, *refs, nb, tl, d_model, fused_chunk, n_cast):
    cast_in, refs = refs[:n_cast], refs[n_cast:]
    if fused_chunk is None:
        ret_ref, sqh_ref, sk_ref, sv_ref = refs[:4]
        cast_out = refs[4:4 + n_cast]
    else:
        ro_ref, sf_ref, sq_ref, skb_ref, svb_ref, sk_ref, sv_ref = refs[:7]
        cast_out = refs[7:7 + n_cast]
        s_scr, dmask_scr = refs[7 + n_cast:]
    for src, dst in zip(cast_in, cast_out):
        dst[...] = src[...].astype(BF16)
    m = nb * tl
    xb = x_ref[...].reshape(m, d_model).astype(BF16)
    cos = jnp.broadcast_to(cos_ref[...][None], (nb, tl, RET_D)).reshape(m, RET_D)
    sin = jnp.broadcast_to(sin_ref[...][None], (nb, tl, RET_D)).reshape(m, RET_D)
    hw = RET_HEADS * RET_D

    def seg(i):
        return _dot(xb, w_ref[:, i * hw:(i + 1) * hw])

    def rope(r, scale):
        outs = []
        for h in range(RET_HEADS):
            xh = r[:, h * RET_D:(h + 1) * RET_D]
            o = xh * cos + pltpu.roll(xh, RET_D // 2, axis=1) * sin
            outs.append(o * scale if scale != 1.0 else o)
        return jnp.concatenate(outs, axis=-1)

    if fused_chunk is not None:
        @pl.when(pl.program_id(1) == 0)
        def _():
            _ret_init(s_scr, dmask_scr, None, fused_chunk)

    ret_in = [rope(seg(0), 1.0).astype(BF16), rope(seg(1), RET_D ** -0.5).astype(BF16),
              seg(2).astype(BF16), seg(3).astype(BF16)]
    def put_heads(ref, val):
        for h in range(SB_HEADS):
            ref[:, h, :, :] = val[:, h * SB_DH:(h + 1) * SB_DH].astype(ref.dtype).reshape(nb, tl, SB_DH)

    if fused_chunk is None:
        for i, val in enumerate(ret_in):
            ret_ref[:, :, i * hw:(i + 1) * hw] = val.reshape(nb, tl, hw)
        put_heads(sqh_ref, seg(4) * (SB_DH ** -0.5))
        put_heads(sk_ref, seg(5))
        put_heads(sv_ref, seg(6))
        return

    def ret_unit(c0, h):
        blk = [val[c0:c0 + fused_chunk, h * RET_D:(h + 1) * RET_D] for val in ret_in]
        ro_ref[0, c0:c0 + fused_chunk, h * RET_D:(h + 1) * RET_D] = _ret_chunk(*blk, h, s_scr, dmask_scr)

    def sb_q():
        sq_ref[...] = (seg(4) * (SB_DH ** -0.5)).astype(BF16).reshape(nb, tl, hw)

    def sb_kv(i, b_ref, t_ref):
        r = seg(i)
        b_ref[...] = r.astype(BF16).reshape(nb, tl, hw)
        t_ref[0] = r.T.reshape(SB_HEADS, SB_DH, tl)

    ret_units = [functools.partial(ret_unit, c0, h) for c0 in range(0, tl, fused_chunk) for h in range(RET_HEADS)]
    mxu_units = [sb_q, functools.partial(sb_kv, 5, skb_ref, sk_ref), functools.partial(sb_kv, 6, svb_ref, sv_ref)]
    per = -(-len(ret_units) // len(mxu_units))
    for i, unit in enumerate(mxu_units):
        unit()
        for r in ret_units[i * per:(i + 1) * per]:
            r()

    @pl.when(pl.program_id(1) == pl.num_programs(1) - 1)
    def _():
        sf_ref[0] = s_scr[...]


def _proj(x, w_in_b, cos, sin, *, nb, tl, fused_chunk, cast=()):
    b, t, d = x.shape
    hw = RET_HEADS * RET_D
    grid = (b // nb, t // tl)
    cast_in, cast_out, cast_shapes = _side_cast(cast, grid[0] * grid[1], lambda i, j: i * grid[1] + j)
    row_spec = lambda w: pl.BlockSpec((nb, tl, w), lambda i, j: (i, j, 0))
    scratch = []
    if fused_chunk is None:
        hs = lambda dt: jax.ShapeDtypeStruct((b, SB_HEADS, t, SB_DH), dt)
        hs_spec = pl.BlockSpec((nb, SB_HEADS, tl, SB_DH), lambda i, j: (i, 0, j, 0))
        out_shape = [jax.ShapeDtypeStruct((b, t, 4 * hw), BF16), hs(BF16), hs(F32), hs(F32)]
        out_specs = [row_spec(4 * hw)] + [hs_spec] * 3
    else:
        assert nb == 1 and tl % fused_chunk == 0
        state = (RET_HEADS, RET_D, RET_D)
        out_shape = ([jax.ShapeDtypeStruct((b, t, hw), BF16), jax.ShapeDtypeStruct((b,) + state, F32)]
                     + [jax.ShapeDtypeStruct((b, t, hw), BF16)] * 3
                     + [jax.ShapeDtypeStruct((b, SB_HEADS, SB_DH, t), F32)] * 2)
        out_specs = ([row_spec(hw), pl.BlockSpec((1,) + state, lambda i, j: (i, 0, 0, 0))] + [row_spec(hw)] * 3
                     + [pl.BlockSpec((1, SB_HEADS, SB_DH, tl), lambda i, j: (i, 0, 0, j))] * 2)
        scratch = [pltpu.VMEM(state, F32), pltpu.VMEM((RET_HEADS, fused_chunk, fused_chunk), F32)]
    return pl.pallas_call(
        functools.partial(_proj_kernel, nb=nb, tl=tl, d_model=d, fused_chunk=fused_chunk, n_cast=len(cast)),
        grid=grid,
        in_specs=[row_spec(d), _const_spec(w_in_b.shape),
                  pl.BlockSpec((tl, RET_D), lambda i, j: (j, 0)),
                  pl.BlockSpec((tl, RET_D), lambda i, j: (j, 0))] + cast_in,
        out_specs=out_specs + cast_out, out_shape=out_shape + cast_shapes, scratch_shapes=scratch,
        compiler_params=pltpu.CompilerParams(
            dimension_semantics=("arbitrary", "arbitrary"), vmem_limit_bytes=VMEM_LIMIT),
        name="proj",
    )(x, w_in_b, cos, sin, *cast)


def _memkv_kernel(m_ref, wk_ref, wv_ref, *refs, dh, n_cast):
    cast_in = refs[:n_cast]
    mk_ref, mv_ref, mkb_ref, mvb_ref = refs[n_cast:n_cast + 4]
    cast_out = refs[n_cast + 4:]
    for src, dst in zip(cast_in, cast_out):
        dst[...] = src[...].astype(BF16)
    mb = m_ref[0].astype(BF16)
    for w_ref, o_ref, ob_ref in ((wk_ref, mk_ref, mkb_ref), (wv_ref, mv_ref, mvb_ref)):
        r = _dot(mb, w_ref[...].astype(BF16))
        for h in range(MEM_HEADS):
            rh = r[:, h * dh:(h + 1) * dh]
            o_ref[0, h] = rh
            ob_ref[0, h] = rh.astype(BF16)


def _memkv(mem, wk, wv, cast=()):
    b, n, d = mem.shape
    dh = d // MEM_HEADS
    spec = pl.BlockSpec((1, MEM_HEADS, n, dh), lambda i: (i, 0, 0, 0))
    cast_in, cast_out, cast_shapes = _side_cast(cast, b, lambda i: i)
    return pl.pallas_call(
        functools.partial(_memkv_kernel, dh=dh, n_cast=len(cast)),
        grid=(b,),
        in_specs=[pl.BlockSpec((1, n, d), lambda i: (i, 0, 0)), _const_spec(wk.shape), _const_spec(wv.shape)] + cast_in,
        out_specs=[spec] * 4 + cast_out,
        out_shape=[jax.ShapeDtypeStruct((b, MEM_HEADS, n, dh), F32)] * 2
        + [jax.ShapeDtypeStruct((b, MEM_HEADS, n, dh), BF16)] * 2 + cast_shapes,
        compiler_params=pltpu.CompilerParams(dimension_semantics=("arbitrary",), vmem_limit_bytes=VMEM_LIMIT),
        name="memkv",
    )(mem, wk, wv, *cast)


def _ret_kernel(ret_ref, s0_ref, *refs, c, n_chunks, n_cast):
    cast_in = refs[:n_cast]
    ro_ref, sf_ref = refs[n_cast:n_cast + 2]
    cast_out = refs[n_cast + 2:2 * n_cast + 2]
    s_scr, dmask_scr = refs[2 * n_cast + 2:]
    for src, dst in zip(cast_in, cast_out):
        dst[...] = src[...].astype(BF16)
    t = pl.program_id(1)

    @pl.when(t == 0)
    def _():
        _ret_init(s_scr, dmask_scr, s0_ref[0], c)

    hw = RET_HEADS * RET_D
    for ci in range(n_chunks):
        rows = slice(ci * c, (ci + 1) * c)
        for h in range(RET_HEADS):
            blk = [ret_ref[0, rows, i * hw + h * RET_D:i * hw + (h + 1) * RET_D] for i in range(4)]
            ro_ref[0, rows, h * RET_D:(h + 1) * RET_D] = _ret_chunk(*blk, h, s_scr, dmask_scr)

    @pl.when(t == pl.num_programs(1) - 1)
    def _():
        sf_ref[0] = s_scr[...]


def _retention(ret, s0, *, c, n_chunks, cast=()):
    b, t, w = ret.shape
    hw = RET_HEADS * RET_D
    tl = c * n_chunks
    grid = (b, t // tl)
    st_spec = pl.BlockSpec((1, RET_HEADS, RET_D, RET_D), lambda i, j: (i, 0, 0, 0))
    cast_in, cast_out, cast_shapes = _side_cast(cast, grid[0] * grid[1], lambda i, j: i * grid[1] + j)
    return pl.pallas_call(
        functools.partial(_ret_kernel, c=c, n_chunks=n_chunks, n_cast=len(cast)),
        grid=grid,
        in_specs=[pl.BlockSpec((1, tl, w), lambda i, j: (i, j, 0)), st_spec] + cast_in,
        out_specs=[pl.BlockSpec((1, tl, hw), lambda i, j: (i, j, 0)), st_spec] + cast_out,
        out_shape=[jax.ShapeDtypeStruct((b, t, hw), BF16),
                   jax.ShapeDtypeStruct((b, RET_HEADS, RET_D, RET_D), F32)] + cast_shapes,
        scratch_shapes=[pltpu.VMEM((RET_HEADS, RET_D, RET_D), F32), pltpu.VMEM((RET_HEADS, c, c), F32)],
        compiler_params=pltpu.CompilerParams(
            dimension_semantics=("arbitrary", "arbitrary"), vmem_limit_bytes=VMEM_LIMIT),
        name="retention",
    )(ret, s0, *cast)


SB_SKIP_ABOVE = 106.0


def _suffix_ones(n):
    j = lax.broadcasted_iota(jnp.int32, (n, n), 0)
    s = lax.broadcasted_iota(jnp.int32, (n, n), 1)
    return jnp.where(j >= s, 1.0, 0.0).astype(BF16)


def _sb_weights(z, carry, valid, u):
    sp = jnp.maximum(z, 0.0) + jnp.log(1.0 + jnp.exp2(jnp.abs(z) * -LOG2E))
    if valid is not None:
        sp = jnp.where(valid, sp, 0.0)
    run = _dot(sp.astype(BF16), u) + carry
    a = jnp.exp(z - run)
    if valid is not None:
        a = jnp.where(valid, a, 0.0)
    return a.astype(BF16), run[:, 0:1]


def _sb_live(carry):
    return (jnp.min(carry) < SB_SKIP_ABOVE).astype(jnp.int32)


def _sb_prompt_kernel(q_ref, k_ref, v_ref, o_ref, *, tq):
    qi = pl.program_id(1)
    lanes = 2 * SB_DH
    n_pairs = q_ref.shape[-1] // lanes
    pair = lambda p: slice(p * lanes, (p + 1) * lanes)
    lo_half = lax.broadcasted_iota(jnp.int32, (1, lanes), 1) < SB_DH
    q2 = []
    for p in range(n_pairs):
        q = q_ref[0, :, pair(p)]
        zero = jnp.zeros_like(q)
        q2.append(jnp.concatenate([jnp.where(lo_half, q, zero), jnp.where(lo_half, zero, q)], axis=0))
    m = 2 * n_pairs * tq
    row = lax.broadcasted_iota(jnp.int32, (m, tq), 0) & (tq - 1)
    col = lax.broadcasted_iota(jnp.int32, (m, tq), 1)
    u = _suffix_ones(tq)

    def tile(kt, acc, carry, valid):
        ks = pl.multiple_of(kt * tq, tq)
        z = jnp.concatenate([_dot_nt(q2[p], k_ref[0, pl.ds(ks, tq), pair(p)]) for p in range(n_pairs)], axis=0)
        a, carry = _sb_weights(z, carry, valid, u)
        pv = jnp.concatenate([_dot(a[p * 2 * tq:(p + 1) * 2 * tq], v_ref[0, pl.ds(ks, tq), pair(p)])
                              for p in range(n_pairs)], axis=0)
        return acc + pv, carry

    acc = jnp.zeros((m, lanes), F32)
    carry = jnp.zeros((m, 1), F32)
    acc, carry = tile(qi, acc, carry, col < row)
    no_tile = jnp.where(qi == 0, 1e30, 0.0).astype(F32)
    acc, carry = tile(jnp.maximum(qi - 1, 0), acc, carry + no_tile, None)

    def cond(c):
        return jnp.logical_and(c[0] < qi, c[3] > 0)

    def body(c):
        acc, carry = tile(qi - 1 - c[0], c[1], c[2], None)
        return c[0] + 1, acc, carry, _sb_live(carry)

    _, acc, _, _ = lax.while_loop(cond, body, (jnp.int32(1), acc, carry, _sb_live(carry)))
    o_ref[0] = jnp.concatenate(
        [jnp.where(lo_half, acc[2 * p * tq:(2 * p + 1) * tq], acc[(2 * p + 1) * tq:(2 * p + 2) * tq])
         for p in range(n_pairs)], axis=-1).astype(BF16)


def _sb_prompt(sq, sk, sv, *, tq):
    b, t, w = sq.shape
    q_spec = pl.BlockSpec((1, tq, w), lambda i, j: (i, j, 0))
    kv_spec = pl.BlockSpec((1, t, w), lambda i, j: (i, 0, 0))
    return pl.pallas_call(
        functools.partial(_sb_prompt_kernel, tq=tq),
        grid=(b, t // tq),
        in_specs=[q_spec, kv_spec, kv_spec],
        out_specs=q_spec,
        out_shape=jax.ShapeDtypeStruct((b, t, w), BF16),
        compiler_params=pltpu.CompilerParams(
            dimension_semantics=("parallel", "arbitrary"), vmem_limit_bytes=VMEM_LIMIT),
        name="sb_prompt",
    )(sq, sk, sv)


def _sb_sample_kernel(q_ref, kn_ref, vn_ref, kct_ref, vct_ref, o_ref, carry_ref, acc_scr, carry_scr, *, tl, tk):
    nh = q_ref.shape[1]
    n_tiles = kct_ref.shape[3] // tk
    m = nh * tl

    def per_head(fn):
        return jnp.concatenate([fn(h) for h in range(nh)], axis=0)

    row = lax.broadcasted_iota(jnp.int32, (m, tl), 0) & (tl - 1)
    col = lax.broadcasted_iota(jnp.int32, (m, tl), 1)
    z = per_head(lambda h: _dot_nt(q_ref[0, h], kn_ref[0, h].astype(BF16)))
    a, carry = _sb_weights(z, jnp.zeros((m, 1), F32), col < row, _suffix_ones(tl))
    acc_scr[...] = per_head(lambda h: _dot(a[h * tl:(h + 1) * tl], vn_ref[0, h].astype(BF16)))
    carry_scr[...] = carry
    u = _suffix_ones(tk)
    for j in reversed(range(n_tiles)):
        @pl.when(jnp.min(carry_scr[...]) < SB_SKIP_ABOVE)
        def _(j=j):
            cols = slice(j * tk, (j + 1) * tk)
            z = per_head(lambda h: _dot(q_ref[0, h], kct_ref[0, h, :, cols].astype(BF16)))
            a, carry = _sb_weights(z, carry_scr[...], None, u)
            acc_scr[...] += per_head(
                lambda h: _dot_nt(a[h * tl:(h + 1) * tl], vct_ref[0, h, :, cols].astype(BF16)))
            carry_scr[...] = carry

    acc = acc_scr[...]
    o_ref[0] = jnp.concatenate([acc[h * tl:(h + 1) * tl] for h in range(nh)], axis=-1).astype(BF16)
    carry_ref[...] = jnp.full(carry_ref.shape, jnp.min(carry_scr[...]), F32)


def _sb_sample_window(sqh, sk_new, sv_new, k_cache_t, v_cache_t, *, tk, window):
    b, nh, tl, dh = sqh.shape
    past = k_cache_t.shape[3]
    assert past % window == 0 and window % tk == 0
    new_spec = pl.BlockSpec((1, nh, tl, dh), lambda i: (i, 0, 0, 0))
    cache_spec = pl.BlockSpec((1, nh, dh, window), lambda i: (i, 0, 0, past // window - 1))
    flag_spec = pl.BlockSpec((1, 8, 128), lambda i: (i, 0, 0))
    return pl.pallas_call(
        functools.partial(_sb_sample_kernel, tl=tl, tk=tk),
        grid=(b,),
        in_specs=[new_spec, new_spec, new_spec, cache_spec, cache_spec],
        out_specs=[pl.BlockSpec((1, tl, nh * dh), lambda i: (i, 0, 0)), flag_spec],
        out_shape=[jax.ShapeDtypeStruct((b, tl, nh * dh), BF16), jax.ShapeDtypeStruct((b, 8, 128), F32)],
        scratch_shapes=[pltpu.VMEM((nh * tl, dh), F32), pltpu.VMEM((nh * tl, 1), F32)],
        compiler_params=pltpu.CompilerParams(dimension_semantics=("parallel",), vmem_limit_bytes=VMEM_LIMIT),
        name="sb_sample",
    )(sqh, sk_new, sv_new, k_cache_t, v_cache_t)


def _sb_sample(sqh, sk_new, sv_new, k_cache_t, v_cache_t, *, tk, window):
    so, carry = _sb_sample_window(sqh, sk_new, sv_new, k_cache_t, v_cache_t, tk=tk, window=window)
    past = k_cache_t.shape[3]
    if window == past:
        return so
    return lax.cond(jnp.min(carry) < SB_SKIP_ABOVE,
                    lambda: _sb_sample_window(sqh, sk_new, sv_new, k_cache_t, v_cache_t, tk=tk, window=past)[0],
                    lambda: so)


def _post_kernel(x_ref, ro_ref, so_ref, mk_ref, mv_ref, cpast_ref,
                 wo_ref, ln1g_ref, ln1b_ref, wq_ref, wom_ref, ln2g_ref, ln2b_ref,
                 wup_ref, cw_ref, cb_ref, wdn_ref, ln3g_ref, ln3b_ref,
                 y_ref, cst_ref, prev_scr, act_scr, *, nb, tl, alpha, fb, n_groups):
    t = pl.program_id(1)
    m = nb * tl
    sub = CONV_W - 1

    @pl.when(t == 0)
    def _():
        prev_scr[...] = jnp.zeros_like(prev_scr)
        prev_scr[:, 8 - sub:8, :] = cpast_ref[...]

    d = x_ref.shape[-1]
    hw = ro_ref.shape[-1]
    dh = mk_ref.shape[3]
    d_ff = wdn_ref.shape[0]
    if nb == 1:
        r = tl // n_groups
        groups = [[(0, i * r, r)] for i in range(n_groups)]
    else:
        per = nb // n_groups
        groups = [[(b, b * tl, tl) for b in range(i * per, (i + 1) * per)] for i in range(n_groups)]
    x_all = x_ref[...].reshape(m, d)
    ro_all = ro_ref[...].reshape(m, hw)
    so_all = so_ref[...].reshape(m, hw)

    def mixer_and_memory(segs):
        g0, g1 = segs[0][1], segs[-1][1] + segs[-1][2]
        mix_in = jnp.concatenate([ro_all[g0:g1], so_all[g0:g1]], axis=-1)
        x1 = _ln(alpha * x_all[g0:g1] + _dot(mix_in, wo_ref[...]), ln1g_ref[...], ln1b_ref[...])
        q = _dot(x1.astype(BF16), wq_ref[...])
        qb = q.astype(BF16)
        s = jnp.concatenate([_dot_nt(qb[s0 - g0:s0 - g0 + n, h * dh:(h + 1) * dh], mk_ref[b, h])
                             for b, s0, n in segs for h in range(MEM_HEADS)], axis=0) * (dh ** -0.5)
        e = jnp.exp(s - jnp.max(s, -1, keepdims=True))
        p = (e * (1.0 / jnp.sum(e, -1, keepdims=True))).astype(BF16)
        outs, r0 = [], 0
        for b, s0, n in segs:
            heads = []
            for h in range(MEM_HEADS):
                heads.append(_dot(p[r0:r0 + n], mv_ref[b, h]).astype(BF16))
                r0 += n
            outs.append(jnp.concatenate(heads, axis=-1))
        att_in = outs[0] if len(outs) == 1 else jnp.concatenate(outs, axis=0)
        return _ln(alpha * x1 + _dot(att_in, wom_ref[...]), ln2g_ref[...], ln2b_ref[...])

    x2 = [mixer_and_memory(segs) for segs in groups]
    x2b = [v.astype(BF16) for v in x2]

    sub_idx = lax.broadcasted_iota(jnp.int32, (1, 8, fb), 1)

    def conv(u, prev8, c0):
        g = u.shape[0] // 8
        u3 = u.reshape(g, 8, fb)
        ext = jnp.concatenate([prev8[None], u3], axis=0)
        r1 = pltpu.roll(ext, 1, axis=1)
        r2 = pltpu.roll(ext, 2, axis=1)
        u1 = jnp.where(sub_idx < 1, r1[:-1], r1[1:])
        u2 = jnp.where(sub_idx < 2, r2[:-1], r2[1:])
        cols = slice(c0, c0 + fb)
        c = (cb_ref[:, cols][None] + cw_ref[0:1, cols][None] * u2
             + cw_ref[1:2, cols][None] * u1 + cw_ref[2:3, cols][None] * u3)
        return c.reshape(u.shape), u3[g - 1]

    def conv_group(u, segs, g0, c0, chain):
        parts = []
        for b, s0, n in segs:
            prev8 = chain.get(b)
            if prev8 is None:
                prev8 = prev_scr[b, :, c0:c0 + fb]
            c, chain[b] = conv(u[s0 - g0:s0 - g0 + n], prev8, c0)
            parts.append(c)
        return parts[0] if len(parts) == 1 else jnp.concatenate(parts, axis=0)

    for blk in range(d_ff // fb):
        ca, cg = blk * fb, d_ff + blk * fb
        chain_a, chain_g = {}, {}
        for segs, xg in zip(groups, x2b):
            g0, g1 = segs[0][1], segs[-1][1] + segs[-1][2]
            a = conv_group(_dot(xg, wup_ref[:, ca:ca + fb]), segs, g0, ca, chain_a)
            g = conv_group(_dot(xg, wup_ref[:, cg:cg + fb]), segs, g0, cg, chain_g)
            act_scr[g0:g1, ca:ca + fb] = (_silu(a) * g).astype(BF16)
        for b in range(nb):
            prev_scr[b, :, ca:ca + fb] = chain_a[b]
            prev_scr[b, :, cg:cg + fb] = chain_g[b]

    for segs, x2g in zip(groups, x2):
        g0, g1 = segs[0][1], segs[-1][1] + segs[-1][2]
        f = _dot(act_scr[g0:g1, :], wdn_ref[...])
        y = _ln(alpha * x2g + f, ln3g_ref[...], ln3b_ref[...])
        for b, s0, n in segs:
            y_ref[b, s0 - b * tl:s0 - b * tl + n, :] = y[s0 - g0:s0 - g0 + n]
    cst_ref[...] = prev_scr[:, 8 - sub:8, :]


def _post(x, ro, so, mk_b, mv_b, conv_past, lw, *, nb, tl, n_groups, alpha):
    b, t, d = x.shape
    assert nb == 1 or tl == t
    hw = ro.shape[-1]
    two_ff = lw["w_up"].shape[1]
    row = lambda w: pl.BlockSpec((nb, tl, w), lambda i, j: (i, j, 0))
    mem_spec = pl.BlockSpec((nb,) + mk_b.shape[1:], lambda i, j: (i, 0, 0, 0),
                            pipeline_mode=pl.Buffered(1) if nb == b else None)
    cst_spec = pl.BlockSpec((nb, CONV_W - 1, two_ff), lambda i, j: (i, 0, 0))
    names = ("w_o", "ln1_g", "ln1_b", "w_q_mem", "w_o_mem", "ln2_g", "ln2_b",
             "w_up", "conv_w", "conv_b", "w_down", "ln3_g", "ln3_b")
    weights = [lw[n] for n in names]
    assert (tl if nb == 1 else nb) % n_groups == 0 and tl % 8 == 0
    return pl.pallas_call(
        functools.partial(_post_kernel, nb=nb, tl=tl, alpha=alpha, fb=256, n_groups=n_groups),
        grid=(b // nb, t // tl),
        in_specs=[row(d), row(hw), row(hw), mem_spec, mem_spec, cst_spec] + [_const_spec(w.shape) for w in weights],
        out_specs=[row(d), cst_spec],
        out_shape=[jax.ShapeDtypeStruct((b, t, d), F32), jax.ShapeDtypeStruct((b, CONV_W - 1, two_ff), F32)],
        scratch_shapes=[pltpu.VMEM((nb, 8, two_ff), F32), pltpu.VMEM((nb * tl, two_ff // 2), BF16)],
        compiler_params=pltpu.CompilerParams(
            dimension_semantics=("parallel", "arbitrary"), vmem_limit_bytes=VMEM_LIMIT),
        name="post",
    )(x, ro, so, mk_b, mv_b, conv_past, *weights)


def _rope_tables(first, n):
    half = RET_D // 2
    inv = 1.0 / (ROPE_BASE ** (np.arange(half, dtype=np.float64) / half))
    ang = np.arange(first, first + n, dtype=np.float64)[:, None] * inv[None, :]
    c, s = np.cos(ang).astype(np.float32), np.sin(ang).astype(np.float32)
    return jnp.asarray(np.concatenate([c, c], axis=-1)), jnp.asarray(np.concatenate([-s, s], axis=-1))


def kernel(x_prompt, x_sample, cache_sb_k, cache_sb_v, state_ret, state_ffn_conv, cache_mem_k, cache_mem_v, mem_prompt, w_in, w_o, ln1_g, ln1_b, w_q_mem, w_k_mem, w_v_mem, w_o_mem, ln2_g, ln2_b, w_up, conv_w, conv_b, w_down, ln3_g, ln3_b):
    depth = w_in.shape[0]
    alpha = (2.0 * depth) ** 0.25
    xp, xs = x_prompt, x_sample
    bp, tp, _ = xp.shape
    bs, ts, _ = xs.shape
    past_len = cache_sb_k.shape[3]
    two_ff = w_up.shape[2]
    cos_p, sin_p = _rope_tables(0, tp)
    cos_s, sin_s = _rope_tables(past_len, ts)
    row2 = lambda a: a.reshape(1, -1)
    swap = lambda a: jnp.swapaxes(a, -1, -2)
    outs = [[] for _ in range(10)]
    for l in range(depth):
        lw = {"ln1_g": row2(ln1_g[l]), "ln1_b": row2(ln1_b[l]), "ln2_g": row2(ln2_g[l]), "ln2_b": row2(ln2_b[l]),
              "conv_w": conv_w[l], "conv_b": row2(conv_b[l]), "ln3_g": row2(ln3_g[l]), "ln3_b": row2(ln3_b[l])}

        mk, mv, mk_b, mv_b, w_in_b = _memkv(mem_prompt, w_k_mem[l], w_v_mem[l], cast=(w_in[l],))
        late = ("w_o", "w_q_mem", "w_o_mem", "w_up", "w_down")
        ro, ps, sq, skb, svb, pk_t, pv_t, *late_b = _proj(
            xp, w_in_b, cos_p, sin_p, nb=1, tl=512, fused_chunk=256,
            cast=(w_o[l], w_q_mem[l], w_o_mem[l], w_up[l], w_down[l]))
        lw.update(zip(late, late_b))
        so = _sb_prompt(sq, skb, svb, tq=256)
        xp, pc = _post(xp, ro, so, mk_b, mv_b, jnp.zeros((bp, CONV_W - 1, two_ff), F32), lw,
                       nb=1, tl=512, n_groups=2, alpha=alpha)

        ret, sqh, sk, sv = _proj(xs, w_in_b, cos_s, sin_s, nb=bs, tl=ts, fused_chunk=None)
        mem_shape = cache_mem_k.shape[1:]
        flat = lambda a: a.reshape(-1, mem_shape[-1])
        ro, ss, cmk_b, cmv_b = _retention(ret, state_ret[l], c=min(REF_CHUNK, ts), n_chunks=1,
                                          cast=(flat(cache_mem_k[l]), flat(cache_mem_v[l])))
        so = _sb_sample(sqh, sk, sv, swap(cache_sb_k[l]), swap(cache_sb_v[l]), tk=256, window=min(512, past_len))
        xs, sc = _post(xs, ro, so, cmk_b.reshape(mem_shape), cmv_b.reshape(mem_shape),
                       state_ffn_conv[l], lw, nb=min(bs, 8), tl=ts, n_groups=1, alpha=alpha)
        for lst, val in zip(outs, (swap(pk_t), swap(pv_t), ps, pc, mk, mv, sk, sv, ss, sc)):
            lst.append(val)
    return (xp, xs) + tuple(jnp.stack(o) for o in outs)
```

```python
import functools
import math

import jax
import jax.numpy as jnp
import numpy as np
from jax import lax
from jax.experimental import pallas as pl
from jax.experimental.pallas import tpu as pltpu

F32 = jnp.float32
BF16 = jnp.bfloat16

RET_HEADS = 4
RET_D = 128
SB_HEADS = 8
SB_DH = 64
MEM_HEADS = 4
CONV_W = 3
ROPE_BASE = 10000.0
LN_EPS = 1e-5
RMS_EPS = 1e-6
LOG2E = math.log2(math.e)
REF_CHUNK = 64

VMEM_LIMIT = 56 * 1024 * 1024


def _const_spec(shape):
    nd = len(shape)
    return pl.BlockSpec(shape, lambda *_: (0,) * nd, pipeline_mode=pl.Buffered(1))


def _ln(x, g, b):
    mu = jnp.mean(x, -1, keepdims=True)
    xc = x - mu
    var = jnp.mean(xc * xc, -1, keepdims=True)
    return xc * lax.rsqrt(var + LN_EPS) * g + b


def _silu(x):
    return x * (1.0 / (1.0 + jnp.exp(-x)))


def _dot(a, b):
    return jnp.dot(a, b, preferred_element_type=F32)


def _dot_nt(a, b):
    return lax.dot_general(a, b, (((1,), (1,)), ((), ())), preferred_element_type=F32)


def _ret_log_gamma(h):
    return math.log1p(-(2.0 ** (-5.0 - h)))


def _ret_init(s_scr, dmask_scr, s0, c):
    s_scr[...] = jnp.zeros_like(s_scr) if s0 is None else s0
    row = lax.broadcasted_iota(jnp.int32, (c, c), 0)
    col = lax.broadcasted_iota(jnp.int32, (c, c), 1)
    diff = (row - col).astype(F32)
    for h in range(RET_HEADS):
        dmask_scr[h] = jnp.where(diff >= 0, jnp.exp(_ret_log_gamma(h) * jnp.maximum(diff, 0.0)), 0.0)


def _ret_chunk(q, k, v, g, h, s_scr, dmask_scr):
    c = q.shape[0]
    log_g = _ret_log_gamma(h)
    idx = lax.broadcasted_iota(jnp.int32, (c, 1), 0).astype(F32)
    scores = _dot_nt(q, k) * dmask_scr[h]
    o = _dot(scores.astype(BF16), v)
    s_prev = s_scr[h]
    qd = (q.astype(F32) * jnp.exp(log_g * (idx + 1.0))).astype(BF16)
    o = o + _dot(qd, s_prev.astype(BF16))
    kd = (k.astype(F32) * jnp.exp(log_g * (c - 1.0 - idx))).astype(BF16)
    s_scr[h] = math.exp(log_g * c) * s_prev + _dot(kd.T, v)
    o = o * lax.rsqrt(jnp.mean(o * o, -1, keepdims=True) + RMS_EPS)
    return (o * _silu(g.astype(F32))).astype(BF16)


def _side_cast(weights, n_steps, step_of):
    in_specs, out_specs, out_shapes = [], [], []
    for w in weights:
        rows, cols = w.shape
        n_chunks = max(n for n in range(1, n_steps + 1) if (rows // 16) % n == 0)
        spec = pl.BlockSpec((rows // n_chunks, cols),
                            lambda *idx, last=n_chunks - 1: (jnp.minimum(step_of(*idx), last), 0))
        in_specs.append(spec)
        out_specs.append(spec)
        out_shapes.append(jax.ShapeDtypeStruct(w.shape, BF16))
    return in_specs, out_specs, out_shapes


def _proj_kernel(x_ref, w_ref, cos_ref, sin_ref, *refs, nb, tl, d_model, fused_chunk, n_cast):
    cast_in, refs = refs[:n_cast], refs[n_cast:]
    if fused_chunk is None:
        ret_ref, sqh_ref, sk_ref, sv_ref = refs[:4]
        cast_out = refs[4:4 + n_cast]
    else:
        ro_ref, sf_ref, sq_ref, skb_ref, svb_ref, sk_ref, sv_ref = refs[:7]
        cast_out = refs[7:7 + n_cast]
        s_scr, dmask_scr = refs[7 + n_cast:]
    for src, dst in zip(cast_in, cast_out):
        dst[...] = src[...].astype(BF16)
    m = nb * tl
    xb = x_ref[...].reshape(m, d_model).astype(BF16)
    cos = jnp.broadcast_to(cos_ref[...][None], (nb, tl, RET_D)).reshape(m, RET_D)
    sin = jnp.broadcast_to(sin_ref[...][None], (nb, tl, RET_D)).reshape(m, RET_D)
    hw = RET_HEADS * RET_D

    def seg(i):
        return _dot(xb, w_ref[:, i * hw:(i + 1) * hw])

    def rope(r, scale):
        outs = []
        for h in range(RET_HEADS):
            xh = r[:, h * RET_D:(h + 1) * RET_D]
            o = xh * cos + pltpu.roll(xh, RET_D // 2, axis=1) * sin
            outs.append(o * scale if scale != 1.0 else o)
        return jnp.concatenate(outs, axis=-1)

    if fused_chunk is not None:
        @pl.when(pl.program_id(1) == 0)
        def _():
            _ret_init(s_scr, dmask_scr, None, fused_chunk)

    ret_in = [rope(seg(0), 1.0).astype(BF16), rope(seg(1), RET_D ** -0.5).astype(BF16),
              seg(2).astype(BF16), seg(3).astype(BF16)]
    def put_heads(ref, val):
        for h in range(SB_HEADS):
            ref[:, h, :, :] = val[:, h * SB_DH:(h + 1) * SB_DH].astype(ref.dtype).reshape(nb, tl, SB_DH)

    if fused_chunk is None:
        for i, val in enumerate(ret_in):
            ret_ref[:, :, i * hw:(i + 1) * hw] = val.reshape(nb, tl, hw)
        put_heads(sqh_ref, seg(4) * (SB_DH ** -0.5))
        put_heads(sk_ref, seg(5))
        put_heads(sv_ref, seg(6))
        return

    def ret_unit(c0, h):
        blk = [val[c0:c0 + fused_chunk, h * RET_D:(h + 1) * RET_D] for val in ret_in]
        ro_ref[0, c0:c0 + fused_chunk, h * RET_D:(h + 1) * RET_D] = _ret_chunk(*blk, h, s_scr, dmask_scr)

    def sb_q():
        sq_ref[...] = (seg(4) * (SB_DH ** -0.5)).astype(BF16).reshape(nb, tl, hw)

    def sb_kv(i, b_ref, t_ref):
        r = seg(i)
        b_ref[...] = r.astype(BF16).reshape(nb, tl, hw)
        t_ref[0] = r.T.reshape(SB_HEADS, SB_DH, tl)

    ret_units = [functools.partial(ret_unit, c0, h) for c0 in range(0, tl, fused_chunk) for h in range(RET_HEADS)]
    mxu_units = [sb_q, functools.partial(sb_kv, 5, skb_ref, sk_ref), functools.partial(sb_kv, 6, svb_ref, sv_ref)]
    per = -(-len(ret_units) // len(mxu_units))
    for i, unit in enumerate(mxu_units):
        unit()
        for r in ret_units[i * per:(i + 1) * per]:
            r()

    @pl.when(pl.program_id(1) == pl.num_programs(1) - 1)
    def _():
        sf_ref[0] = s_scr[...]


def _proj(x, w_in_b, cos, sin, *, nb, tl, fused_chunk, cast=()):
    b, t, d = x.shape
    hw = RET_HEADS * RET_D
    grid = (b // nb, t // tl)
    cast_in, cast_out, cast_shapes = _side_cast(cast, grid[0] * grid[1], lambda i, j: i * grid[1] + j)
    row_spec = lambda w: pl.BlockSpec((nb, tl, w), lambda i, j: (i, j, 0))
    scratch = []
    if fused_chunk is None:
        hs = lambda dt: jax.ShapeDtypeStruct((b, SB_HEADS, t, SB_DH), dt)
        hs_spec = pl.BlockSpec((nb, SB_HEADS, tl, SB_DH), lambda i, j: (i, 0, j, 0))
        out_shape = [jax.ShapeDtypeStruct((b, t, 4 * hw), BF16), hs(BF16), hs(F32), hs(F32)]
        out_specs = [row_spec(4 * hw)] + [hs_spec] * 3
    else:
        assert nb == 1 and tl % fused_chunk == 0
        state = (RET_HEADS, RET_D, RET_D)
        out_shape = ([jax.ShapeDtypeStruct((b, t, hw), BF16), jax.ShapeDtypeStruct((b,) + state, F32)]
                     + [jax.ShapeDtypeStruct((b, t, hw), BF16)] * 3
                     + [jax.ShapeDtypeStruct((b, SB_HEADS, SB_DH, t), F32)] * 2)
        out_specs = ([row_spec(hw), pl.BlockSpec((1,) + state, lambda i, j: (i, 0, 0, 0))] + [row_spec(hw)] * 3
                     + [pl.BlockSpec((1, SB_HEADS, SB_DH, tl), lambda i, j: (i, 0, 0, j))] * 2)
        scratch = [pltpu.VMEM(state, F32), pltpu.VMEM((RET_HEADS, fused_chunk, fused_chunk), F32)]
    return pl.pallas_call(
        functools.partial(_proj_kernel, nb=nb, tl=tl, d_model=d, fused_chunk=fused_chunk, n_cast=len(cast)),
        grid=grid,
        in_specs=[row_spec(d), _const_spec(w_in_b.shape),
                  pl.BlockSpec((tl, RET_D), lambda i, j: (j, 0)),
                  pl.BlockSpec((tl, RET_D), lambda i, j: (j, 0))] + cast_in,
        out_specs=out_specs + cast_out, out_shape=out_shape + cast_shapes, scratch_shapes=scratch,
        compiler_params=pltpu.CompilerParams(
            dimension_semantics=("arbitrary", "arbitrary"), vmem_limit_bytes=VMEM_LIMIT),
        name="proj",
    )(x, w_in_b, cos, sin, *cast)


def _memkv_kernel(m_ref, wk_ref, wv_ref, *refs, dh, n_cast):
    cast_in = refs[:n_cast]
    mk_ref, mv_ref, mkb_ref, mvb_ref = refs[n_cast:n_cast + 4]
    cast_out = refs[n_cast + 4:]
    for src, dst in zip(cast_in, cast_out):
        dst[...] = src[...].astype(BF16)
    mb = m_ref[0].astype(BF16)
    for w_ref, o_ref, ob_ref in ((wk_ref, mk_ref, mkb_ref), (wv_ref, mv_ref, mvb_ref)):
        r = _dot(mb, w_ref[...].astype(BF16))
        for h in range(MEM_HEADS):
            rh = r[:, h * dh:(h + 1) * dh]
            o_ref[0, h] = rh
            ob_ref[0, h] = rh.astype(BF16)


def _memkv(mem, wk, wv, cast=()):
    b, n, d = mem.shape
    dh = d // MEM_HEADS
    spec = pl.BlockSpec((1, MEM_HEADS, n, dh), lambda i: (i, 0, 0, 0))
    cast_in, cast_out, cast_shapes = _side_cast(cast, b, lambda i: i)
    return pl.pallas_call(
        functools.partial(_memkv_kernel, dh=dh, n_cast=len(cast)),
        grid=(b,),
        in_specs=[pl.BlockSpec((1, n, d), lambda i: (i, 0, 0)), _const_spec(wk.shape), _const_spec(wv.shape)] + cast_in,
        out_specs=[spec] * 4 + cast_out,
        out_shape=[jax.ShapeDtypeStruct((b, MEM_HEADS, n, dh), F32)] * 2
        + [jax.ShapeDtypeStruct((b, MEM_HEADS, n, dh), BF16)] * 2 + cast_shapes,
        compiler_params=pltpu.CompilerParams(dimension_semantics=("arbitrary",), vmem_limit_bytes=VMEM_LIMIT),
        name="memkv",
    )(mem, wk, wv, *cast)


def _ret_kernel(ret_ref, s0_ref, ro_ref, sf_ref, s_scr, dmask_scr, *, c, n_chunks):
    t = pl.program_id(1)

    @pl.when(t == 0)
    def _():
        _ret_init(s_scr, dmask_scr, s0_ref[...], c)

    hw = RET_HEADS * RET_D
    for s in range(ret_ref.shape[0]):
        for ci in range(n_chunks):
            rows = slice(ci * c, (ci + 1) * c)
            for h in range(RET_HEADS):
                blk = [ret_ref[s, rows, i * hw + h * RET_D:i * hw + (h + 1) * RET_D] for i in range(4)]
                ro_ref[s, rows, h * RET_D:(h + 1) * RET_D] = _ret_chunk(*blk, h, s_scr.at[s], dmask_scr)

    @pl.when(t == pl.num_programs(1) - 1)
    def _():
        sf_ref[...] = s_scr[...]


def _retention(ret, s0, *, nb, c, n_chunks):
    b, t, w = ret.shape
    hw = RET_HEADS * RET_D
    tl = c * n_chunks
    state = (nb, RET_HEADS, RET_D, RET_D)
    st_spec = pl.BlockSpec(state, lambda i, j: (i, 0, 0, 0))
    return pl.pallas_call(
        functools.partial(_ret_kernel, c=c, n_chunks=n_chunks),
        grid=(b // nb, t // tl),
        in_specs=[pl.BlockSpec((nb, tl, w), lambda i, j: (i, j, 0)), st_spec],
        out_specs=[pl.BlockSpec((nb, tl, hw), lambda i, j: (i, j, 0)), st_spec],
        out_shape=[jax.ShapeDtypeStruct((b, t, hw), BF16),
                   jax.ShapeDtypeStruct((b, RET_HEADS, RET_D, RET_D), F32)],
        scratch_shapes=[pltpu.VMEM(state, F32), pltpu.VMEM((RET_HEADS, c, c), F32)],
        compiler_params=pltpu.CompilerParams(
            dimension_semantics=("parallel", "arbitrary"), vmem_limit_bytes=VMEM_LIMIT),
        name="retention",
    )(ret, s0)


SB_SKIP_ABOVE = 106.0


def _suffix_ones(n):
    j = lax.broadcasted_iota(jnp.int32, (n, n), 0)
    s = lax.broadcasted_iota(jnp.int32, (n, n), 1)
    return jnp.where(j >= s, 1.0, 0.0).astype(BF16)


def _sb_weights(z, carry, valid, u):
    sp = jnp.maximum(z, 0.0) + jnp.log(1.0 + jnp.exp2(jnp.abs(z) * -LOG2E))
    if valid is not None:
        sp = jnp.where(valid, sp, 0.0)
    run = _dot(sp.astype(BF16), u) + carry
    a = jnp.exp(z - run)
    if valid is not None:
        a = jnp.where(valid, a, 0.0)
    return a.astype(BF16), run[:, 0:1]


def _sb_live(carry):
    return (jnp.min(carry) < SB_SKIP_ABOVE).astype(jnp.int32)


def _sb_prompt_kernel(q_ref, k_ref, v_ref, *refs, tq, n_cast):
    o_ref = refs[n_cast]
    for src, dst in zip(refs[:n_cast], refs[n_cast + 1:]):
        dst[...] = src[...].astype(BF16)
    qi = pl.program_id(1)
    lanes = 2 * SB_DH
    n_pairs = q_ref.shape[-1] // lanes
    pair = lambda p: slice(p * lanes, (p + 1) * lanes)
    lo_half = lax.broadcasted_iota(jnp.int32, (1, lanes), 1) < SB_DH
    q2 = []
    for p in range(n_pairs):
        q = q_ref[0, :, pair(p)]
        zero = jnp.zeros_like(q)
        q2.append(jnp.concatenate([jnp.where(lo_half, q, zero), jnp.where(lo_half, zero, q)], axis=0))
    m = 2 * n_pairs * tq
    row = lax.broadcasted_iota(jnp.int32, (m, tq), 0) & (tq - 1)
    col = lax.broadcasted_iota(jnp.int32, (m, tq), 1)
    u = _suffix_ones(tq)

    def tile(kt, acc, carry, valid):
        ks = pl.multiple_of(kt * tq, tq)
        z = jnp.concatenate([_dot_nt(q2[p], k_ref[0, pl.ds(ks, tq), pair(p)]) for p in range(n_pairs)], axis=0)
        a, carry = _sb_weights(z, carry, valid, u)
        pv = jnp.concatenate([_dot(a[p * 2 * tq:(p + 1) * 2 * tq], v_ref[0, pl.ds(ks, tq), pair(p)])
                              for p in range(n_pairs)], axis=0)
        return acc + pv, carry

    acc = jnp.zeros((m, lanes), F32)
    carry = jnp.zeros((m, 1), F32)
    acc, carry = tile(qi, acc, carry, col < row)
    no_tile = jnp.where(qi == 0, 1e30, 0.0).astype(F32)
    acc, carry = tile(jnp.maximum(qi - 1, 0), acc, carry + no_tile, None)

    def cond(c):
        return jnp.logical_and(c[0] < qi, c[3] > 0)

    def body(c):
        acc, carry = tile(qi - 1 - c[0], c[1], c[2], None)
        return c[0] + 1, acc, carry, _sb_live(carry)

    _, acc, _, _ = lax.while_loop(cond, body, (jnp.int32(1), acc, carry, _sb_live(carry)))
    o_ref[0] = jnp.concatenate(
        [jnp.where(lo_half, acc[2 * p * tq:(2 * p + 1) * tq], acc[(2 * p + 1) * tq:(2 * p + 2) * tq])
         for p in range(n_pairs)], axis=-1).astype(BF16)


def _sb_prompt(sq, sk, sv, *, tq, cast=()):
    b, t, w = sq.shape
    grid = (b, t // tq)
    q_spec = pl.BlockSpec((1, tq, w), lambda i, j: (i, j, 0))
    kv_spec = pl.BlockSpec((1, t, w), lambda i, j: (i, 0, 0))
    cast_in, cast_out, cast_shapes = _side_cast(cast, grid[0] * grid[1], lambda i, j: i * grid[1] + j)
    return pl.pallas_call(
        functools.partial(_sb_prompt_kernel, tq=tq, n_cast=len(cast)),
        grid=grid,
        in_specs=[q_spec, kv_spec, kv_spec] + cast_in,
        out_specs=[q_spec] + cast_out,
        out_shape=[jax.ShapeDtypeStruct((b, t, w), BF16)] + cast_shapes,
        compiler_params=pltpu.CompilerParams(
            dimension_semantics=("arbitrary", "arbitrary"), vmem_limit_bytes=VMEM_LIMIT),
        name="sb_prompt",
    )(sq, sk, sv, *cast)


def _sb_sample_kernel(q_ref, kn_ref, vn_ref, kct_ref, vct_ref, o_ref, carry_ref, acc_scr, carry_scr, *, tl, tk):
    ns, nh = q_ref.shape[:2]
    n_tiles = kct_ref.shape[3] // tk
    m = ns * nh * tl
    units = [(s, h) for s in range(ns) for h in range(nh)]

    def per_unit(fn):
        return jnp.concatenate([fn(i, s, h) for i, (s, h) in enumerate(units)], axis=0)

    row = lax.broadcasted_iota(jnp.int32, (m, tl), 0) & (tl - 1)
    col = lax.broadcasted_iota(jnp.int32, (m, tl), 1)
    z = per_unit(lambda i, s, h: _dot_nt(q_ref[s, h], kn_ref[s, h].astype(BF16)))
    a, carry = _sb_weights(z, jnp.zeros((m, 1), F32), col < row, _suffix_ones(tl))
    acc_scr[...] = per_unit(lambda i, s, h: _dot(a[i * tl:(i + 1) * tl], vn_ref[s, h].astype(BF16)))
    carry_scr[...] = carry
    u = _suffix_ones(tk)
    for j in reversed(range(n_tiles)):
        @pl.when(jnp.min(carry_scr[...]) < SB_SKIP_ABOVE)
        def _(j=j):
            cols = slice(j * tk, (j + 1) * tk)
            z = per_unit(lambda i, s, h: _dot(q_ref[s, h], kct_ref[s, h, :, cols].astype(BF16)))
            a, carry = _sb_weights(z, carry_scr[...], None, u)
            acc_scr[...] += per_unit(
                lambda i, s, h: _dot_nt(a[i * tl:(i + 1) * tl], vct_ref[s, h, :, cols].astype(BF16)))
            carry_scr[...] = carry

    acc = acc_scr[...]
    for s in range(ns):
        o_ref[s] = jnp.concatenate([acc[(s * nh + h) * tl:(s * nh + h + 1) * tl] for h in range(nh)],
                                   axis=-1).astype(BF16)
    carry_ref[...] = jnp.full(carry_ref.shape, jnp.min(carry_scr[...]), F32)


def _sb_sample_window(sqh, sk_new, sv_new, k_cache_t, v_cache_t, *, tk, window, ns):
    b, nh, tl, dh = sqh.shape
    past = k_cache_t.shape[3]
    assert past % window == 0 and window % tk == 0 and b % ns == 0
    new_spec = pl.BlockSpec((ns, nh, tl, dh), lambda i: (i, 0, 0, 0))
    cache_spec = pl.BlockSpec((ns, nh, dh, window), lambda i: (i, 0, 0, past // window - 1))
    flag_spec = pl.BlockSpec((1, 8, 128), lambda i: (i, 0, 0))
    return pl.pallas_call(
        functools.partial(_sb_sample_kernel, tl=tl, tk=tk),
        grid=(b // ns,),
        in_specs=[new_spec, new_spec, new_spec, cache_spec, cache_spec],
        out_specs=[pl.BlockSpec((ns, tl, nh * dh), lambda i: (i, 0, 0)), flag_spec],
        out_shape=[jax.ShapeDtypeStruct((b, tl, nh * dh), BF16), jax.ShapeDtypeStruct((b // ns, 8, 128), F32)],
        scratch_shapes=[pltpu.VMEM((ns * nh * tl, dh), F32), pltpu.VMEM((ns * nh * tl, 1), F32)],
        compiler_params=pltpu.CompilerParams(dimension_semantics=("parallel",), vmem_limit_bytes=VMEM_LIMIT),
        name="sb_sample",
    )(sqh, sk_new, sv_new, k_cache_t, v_cache_t)


def _sb_sample(sqh, sk_new, sv_new, k_cache_t, v_cache_t, *, tk, window, ns):
    so, carry = _sb_sample_window(sqh, sk_new, sv_new, k_cache_t, v_cache_t, tk=tk, window=window, ns=ns)
    past = k_cache_t.shape[3]
    if window == past:
        return so
    return lax.cond(jnp.min(carry) < SB_SKIP_ABOVE,
                    lambda: _sb_sample_window(sqh, sk_new, sv_new, k_cache_t, v_cache_t, tk=tk, window=past, ns=1)[0],
                    lambda: so)


def _post_kernel(x_ref, ro_ref, so_ref, mk_ref, mv_ref, cpast_ref,
                 wo_ref, ln1g_ref, ln1b_ref, wq_ref, wom_ref, ln2g_ref, ln2b_ref,
                 wup_ref, cw_ref, cb_ref, wdn_ref, ln3g_ref, ln3b_ref,
                 y_ref, cst_ref, prev_scr, act_scr, *, nb, tl, alpha, fb, n_groups):
    t = pl.program_id(1)
    m = nb * tl
    sub = CONV_W - 1

    @pl.when(t == 0)
    def _():
        prev_scr[...] = jnp.zeros_like(prev_scr)
        prev_scr[:, 8 - sub:8, :] = cpast_ref[...]

    d = x_ref.shape[-1]
    hw = ro_ref.shape[-1]
    dh = mk_ref.shape[3]
    d_ff = wdn_ref.shape[0]
    if nb == 1:
        r = tl // n_groups
        groups = [[(0, i * r, r)] for i in range(n_groups)]
    else:
        per = nb // n_groups
        groups = [[(b, b * tl, tl) for b in range(i * per, (i + 1) * per)] for i in range(n_groups)]
    x_all = x_ref[...].reshape(m, d)
    ro_all = ro_ref[...].reshape(m, hw)
    so_all = so_ref[...].reshape(m, hw)

    def mixer_and_memory(segs):
        g0, g1 = segs[0][1], segs[-1][1] + segs[-1][2]
        mix_in = jnp.concatenate([ro_all[g0:g1], so_all[g0:g1]], axis=-1)
        x1 = _ln(alpha * x_all[g0:g1] + _dot(mix_in, wo_ref[...]), ln1g_ref[...], ln1b_ref[...])
        q = _dot(x1.astype(BF16), wq_ref[...])
        qb = q.astype(BF16)
        s = jnp.concatenate([_dot_nt(qb[s0 - g0:s0 - g0 + n, h * dh:(h + 1) * dh], mk_ref[b, h])
                             for b, s0, n in segs for h in range(MEM_HEADS)], axis=0) * (dh ** -0.5)
        e = jnp.exp(s - jnp.max(s, -1, keepdims=True))
        p = (e * (1.0 / jnp.sum(e, -1, keepdims=True))).astype(BF16)
        outs, r0 = [], 0
        for b, s0, n in segs:
            heads = []
            for h in range(MEM_HEADS):
                heads.append(_dot(p[r0:r0 + n], mv_ref[b, h]).astype(BF16))
                r0 += n
            outs.append(jnp.concatenate(heads, axis=-1))
        att_in = outs[0] if len(outs) == 1 else jnp.concatenate(outs, axis=0)
        return _ln(alpha * x1 + _dot(att_in, wom_ref[...]), ln2g_ref[...], ln2b_ref[...])

    x2 = [mixer_and_memory(segs) for segs in groups]
    x2b = [v.astype(BF16) for v in x2]

    sub_idx = lax.broadcasted_iota(jnp.int32, (1, 8, fb), 1)

    def conv(u, prev8, c0):
        g = u.shape[0] // 8
        u3 = u.reshape(g, 8, fb)
        ext = jnp.concatenate([prev8[None], u3], axis=0)
        r1 = pltpu.roll(ext, 1, axis=1)
        r2 = pltpu.roll(r1, 1, axis=1)
        u1 = jnp.where(sub_idx < 1, r1[:-1], r1[1:])
        u2 = jnp.where(sub_idx < 2, r2[:-1], r2[1:])
        cols = slice(c0, c0 + fb)
        c = (cb_ref[:, cols][None] + cw_ref[0:1, cols][None] * u2
             + cw_ref[1:2, cols][None] * u1 + cw_ref[2:3, cols][None] * u3)
        return c.reshape(u.shape), u3[g - 1]

    def conv_group(u, segs, g0, c0, chain):
        parts = []
        for b, s0, n in segs:
            prev8 = chain.get(b)
            if prev8 is None:
                prev8 = prev_scr[b, :, c0:c0 + fb]
            c, chain[b] = conv(u[s0 - g0:s0 - g0 + n], prev8, c0)
            parts.append(c)
        return parts[0] if len(parts) == 1 else jnp.concatenate(parts, axis=0)

    for blk in range(d_ff // fb):
        ca, cg = blk * fb, d_ff + blk * fb
        chain_a, chain_g = {}, {}
        for segs, xg in zip(groups, x2b):
            g0, g1 = segs[0][1], segs[-1][1] + segs[-1][2]
            a = conv_group(_dot(xg, wup_ref[:, ca:ca + fb]), segs, g0, ca, chain_a)
            g = conv_group(_dot(xg, wup_ref[:, cg:cg + fb]), segs, g0, cg, chain_g)
            act_scr[g0:g1, ca:ca + fb] = (_silu(a) * g).astype(BF16)
        for b in range(nb):
            prev_scr[b, :, ca:ca + fb] = chain_a[b]
            prev_scr[b, :, cg:cg + fb] = chain_g[b]

    for segs, x2g in zip(groups, x2):
        g0, g1 = segs[0][1], segs[-1][1] + segs[-1][2]
        f = _dot(act_scr[g0:g1, :], wdn_ref[...])
        y = _ln(alpha * x2g + f, ln3g_ref[...], ln3b_ref[...])
        for b, s0, n in segs:
            y_ref[b, s0 - b * tl:s0 - b * tl + n, :] = y[s0 - g0:s0 - g0 + n]
    cst_ref[...] = prev_scr[:, 8 - sub:8, :]


def _post(x, ro, so, mk_b, mv_b, conv_past, lw, *, nb, tl, n_groups, alpha):
    b, t, d = x.shape
    assert nb == 1 or tl == t
    hw = ro.shape[-1]
    two_ff = lw["w_up"].shape[1]
    row = lambda w: pl.BlockSpec((nb, tl, w), lambda i, j: (i, j, 0))
    mem_spec = pl.BlockSpec((nb,) + mk_b.shape[1:], lambda i, j: (i, 0, 0, 0),
                            pipeline_mode=pl.Buffered(1) if nb == b else None)
    cst_spec = pl.BlockSpec((nb, CONV_W - 1, two_ff), lambda i, j: (i, 0, 0))
    names = ("w_o", "ln1_g", "ln1_b", "w_q_mem", "w_o_mem", "ln2_g", "ln2_b",
             "w_up", "conv_w", "conv_b", "w_down", "ln3_g", "ln3_b")
    weights = [lw[n] for n in names]
    assert (tl if nb == 1 else nb) % n_groups == 0 and tl % 8 == 0
    return pl.pallas_call(
        functools.partial(_post_kernel, nb=nb, tl=tl, alpha=alpha, fb=256, n_groups=n_groups),
        grid=(b // nb, t // tl),
        in_specs=[row(d), row(hw), row(hw), mem_spec, mem_spec, cst_spec] + [_const_spec(w.shape) for w in weights],
        out_specs=[row(d), cst_spec],
        out_shape=[jax.ShapeDtypeStruct((b, t, d), F32), jax.ShapeDtypeStruct((b, CONV_W - 1, two_ff), F32)],
        scratch_shapes=[pltpu.VMEM((nb, 8, two_ff), F32), pltpu.VMEM((nb * tl, two_ff // 2), BF16)],
        compiler_params=pltpu.CompilerParams(
            dimension_semantics=("parallel", "arbitrary"), vmem_limit_bytes=VMEM_LIMIT),
        name="post",
    )(x, ro, so, mk_b, mv_b, conv_past, *weights)


def _rope_tables(first, n):
    half = RET_D // 2
    inv = 1.0 / (ROPE_BASE ** (np.arange(half, dtype=np.float64) / half))
    ang = np.arange(first, first + n, dtype=np.float64)[:, None] * inv[None, :]
    c, s = np.cos(ang).astype(np.float32), np.sin(ang).astype(np.float32)
    return jnp.asarray(np.concatenate([c, c], axis=-1)), jnp.asarray(np.concatenate([-s, s], axis=-1))


def kernel(x_prompt, x_sample, cache_sb_k, cache_sb_v, state_ret, state_ffn_conv, cache_mem_k, cache_mem_v, mem_prompt, w_in, w_o, ln1_g, ln1_b, w_q_mem, w_k_mem, w_v_mem, w_o_mem, ln2_g, ln2_b, w_up, conv_w, conv_b, w_down, ln3_g, ln3_b):
    depth = w_in.shape[0]
    alpha = (2.0 * depth) ** 0.25
    xp, xs = x_prompt, x_sample
    bp, tp, _ = xp.shape
    bs, ts, _ = xs.shape
    past_len = cache_sb_k.shape[3]
    two_ff = w_up.shape[2]
    cos_p, sin_p = _rope_tables(0, tp)
    cos_s, sin_s = _rope_tables(past_len, ts)
    row2 = lambda a: a.reshape(1, -1)
    swap = lambda a: jnp.swapaxes(a, -1, -2)
    outs = [[] for _ in range(10)]
    for l in range(depth):
        lw = {"ln1_g": row2(ln1_g[l]), "ln1_b": row2(ln1_b[l]), "ln2_g": row2(ln2_g[l]), "ln2_b": row2(ln2_b[l]),
              "conv_w": conv_w[l], "conv_b": row2(conv_b[l]), "ln3_g": row2(ln3_g[l]), "ln3_b": row2(ln3_b[l])}

        mk, mv, mk_b, mv_b, w_in_b = _memkv(mem_prompt, w_k_mem[l], w_v_mem[l], cast=(w_in[l],))
        late = ("w_o", "w_q_mem", "w_o_mem", "w_up", "w_down")
        ro, ps, sq, skb, svb, pk_t, pv_t, *late_b = _proj(
            xp, w_in_b, cos_p, sin_p, nb=1, tl=512, fused_chunk=256,
            cast=(w_o[l], w_q_mem[l], w_o_mem[l], w_up[l], w_down[l]))
        lw.update(zip(late, late_b))
        mem_shape = cache_mem_k.shape[1:]
        flat = lambda a: a.reshape(-1, mem_shape[-1])
        so, cmk_b, cmv_b = _sb_prompt(sq, skb, svb, tq=256, cast=(flat(cache_mem_k[l]), flat(cache_mem_v[l])))
        xp, pc = _post(xp, ro, so, mk_b, mv_b, jnp.zeros((bp, CONV_W - 1, two_ff), F32), lw,
                       nb=1, tl=512, n_groups=2, alpha=alpha)

        ret, sqh, sk, sv = _proj(xs, w_in_b, cos_s, sin_s, nb=bs, tl=ts, fused_chunk=None)
        ro, ss = _retention(ret, state_ret[l], nb=min(bs, 4), c=min(REF_CHUNK, ts), n_chunks=1)
        so = _sb_sample(sqh, sk, sv, swap(cache_sb_k[l]), swap(cache_sb_v[l]), tk=256,
                        window=min(512, past_len), ns=min(bs, 4))
        xs, sc = _post(xs, ro, so, cmk_b.reshape(mem_shape), cmv_b.reshape(mem_shape),
                       state_ffn_conv[l], lw, nb=min(bs, 8), tl=ts, n_groups=1, alpha=alpha)
        for lst, val in zip(outs, (swap(pk_t), swap(pv_t), ps, pc, mk, mv, sk, sv, ss, sc)):
            lst.append(val)
    return (xp, xs) + tuple(jnp.stack(o) for o in outs)
```

```python
import functools
import math

import jax
import jax.numpy as jnp
import numpy as np
from jax import lax
from jax.experimental import pallas as pl
from jax.experimental.pallas import tpu as pltpu

F32 = jnp.float32
BF16 = jnp.bfloat16

RET_HEADS = 4
RET_D = 128
SB_HEADS = 8
SB_DH = 64
MEM_HEADS = 4
CONV_W = 3
ROPE_BASE = 10000.0
LN_EPS = 1e-5
RMS_EPS = 1e-6
LOG2E = math.log2(math.e)
REF_CHUNK = 64

VMEM_LIMIT = 56 * 1024 * 1024

PROJ_ROWS = 1024
RET_CHUNK = 256
SB_TILE = 256
SB_WINDOW = 512
POST_ROWS = 512
POST_GROUPS = 2
SAMPLE_STREAMS = 4
SAMPLE_POST_STREAMS = 8


def _const_spec(shape):
    nd = len(shape)
    return pl.BlockSpec(shape, lambda *_: (0,) * nd, pipeline_mode=pl.Buffered(1))


def _ln(x, g, b):
    mu = jnp.mean(x, -1, keepdims=True)
    xc = x - mu
    var = jnp.mean(xc * xc, -1, keepdims=True)
    return xc * lax.rsqrt(var + LN_EPS) * g + b


def _silu(x):
    return x * (1.0 / (1.0 + jnp.exp(-x)))


def _dot(a, b):
    return jnp.dot(a, b, preferred_element_type=F32)


def _dot_nt(a, b):
    return lax.dot_general(a, b, (((1,), (1,)), ((), ())), preferred_element_type=F32)


def _ret_log_gamma(h):
    return math.log1p(-(2.0 ** (-5.0 - h)))


def _ret_init(s_scr, dmask_scr, s0, c):
    s_scr[...] = jnp.zeros_like(s_scr) if s0 is None else s0
    row = lax.broadcasted_iota(jnp.int32, (c, c), 0)
    col = lax.broadcasted_iota(jnp.int32, (c, c), 1)
    diff = (row - col).astype(F32)
    for h in range(RET_HEADS):
        dmask_scr[h] = jnp.where(diff >= 0, jnp.exp(_ret_log_gamma(h) * jnp.maximum(diff, 0.0)), 0.0)


def _ret_chunk(q, k, v, g, h, s_scr, dmask_scr):
    c = q.shape[0]
    log_g = _ret_log_gamma(h)
    idx = lax.broadcasted_iota(jnp.int32, (c, 1), 0).astype(F32)
    scores = _dot_nt(q, k) * dmask_scr[h]
    o = _dot(scores.astype(BF16), v)
    s_prev = s_scr[h]
    qd = (q.astype(F32) * jnp.exp(log_g * (idx + 1.0))).astype(BF16)
    o = o + _dot(qd, s_prev.astype(BF16))
    kd = (k.astype(F32) * jnp.exp(log_g * (c - 1.0 - idx))).astype(BF16)
    s_scr[h] = math.exp(log_g * c) * s_prev + _dot(kd.T, v)
    o = o * lax.rsqrt(jnp.mean(o * o, -1, keepdims=True) + RMS_EPS)
    return (o * _silu(g.astype(F32))).astype(BF16)


def _side_cast(weights, n_steps, step_of):
    in_specs, out_specs, out_shapes = [], [], []
    for w in weights:
        rows, cols = w.shape
        n_chunks = max(n for n in range(1, n_steps + 1) if (rows // 16) % n == 0)
        spec = pl.BlockSpec((rows // n_chunks, cols),
                            lambda *idx, last=n_chunks - 1: (jnp.minimum(step_of(*idx), last), 0))
        in_specs.append(spec)
        out_specs.append(spec)
        out_shapes.append(jax.ShapeDtypeStruct(w.shape, BF16))
    return in_specs, out_specs, out_shapes


def _proj_kernel(x_ref, w_ref, cos_ref, sin_ref, *refs, nb, tl, d_model, fused_chunk, n_cast):
    cast_in, refs = refs[:n_cast], refs[n_cast:]
    if fused_chunk is None:
        ret_ref, sqh_ref, sk_ref, sv_ref = refs[:4]
        cast_out = refs[4:4 + n_cast]
    else:
        ro_ref, sf_ref, sq_ref, skb_ref, svb_ref, sk_ref, sv_ref = refs[:7]
        cast_out = refs[7:7 + n_cast]
        s_scr, dmask_scr = refs[7 + n_cast:]
    for src, dst in zip(cast_in, cast_out):
        dst[...] = src[...].astype(BF16)
    m = nb * tl
    xb = x_ref[...].reshape(m, d_model).astype(BF16)
    cos = jnp.broadcast_to(cos_ref[...][None], (nb, tl, RET_D)).reshape(m, RET_D)
    sin = jnp.broadcast_to(sin_ref[...][None], (nb, tl, RET_D)).reshape(m, RET_D)
    hw = RET_HEADS * RET_D

    def seg(i):
        return _dot(xb, w_ref[:, i * hw:(i + 1) * hw])

    def rope(r, scale):
        outs = []
        for h in range(RET_HEADS):
            xh = r[:, h * RET_D:(h + 1) * RET_D]
            o = xh * cos + pltpu.roll(xh, RET_D // 2, axis=1) * sin
            outs.append(o * scale if scale != 1.0 else o)
        return jnp.concatenate(outs, axis=-1)

    if fused_chunk is not None:
        @pl.when(pl.program_id(1) == 0)
        def _():
            _ret_init(s_scr, dmask_scr, None, fused_chunk)

    ret_in = [rope(seg(0), 1.0).astype(BF16), rope(seg(1), RET_D ** -0.5).astype(BF16),
              seg(2).astype(BF16), seg(3).astype(BF16)]
    def put_heads(ref, val):
        for h in range(SB_HEADS):
            ref[:, h, :, :] = val[:, h * SB_DH:(h + 1) * SB_DH].astype(ref.dtype).reshape(nb, tl, SB_DH)

    if fused_chunk is None:
        for i, val in enumerate(ret_in):
            ret_ref[:, :, i * hw:(i + 1) * hw] = val.reshape(nb, tl, hw)
        put_heads(sqh_ref, seg(4) * (SB_DH ** -0.5))
        put_heads(sk_ref, seg(5))
        put_heads(sv_ref, seg(6))
        return

    def ret_unit(c0, h):
        blk = [val[c0:c0 + fused_chunk, h * RET_D:(h + 1) * RET_D] for val in ret_in]
        ro_ref[0, c0:c0 + fused_chunk, h * RET_D:(h + 1) * RET_D] = _ret_chunk(*blk, h, s_scr, dmask_scr)

    def sb_q():
        sq_ref[...] = (seg(4) * (SB_DH ** -0.5)).astype(BF16).reshape(nb, tl, hw)

    def sb_kv(i, b_ref, t_ref):
        r = seg(i)
        b_ref[...] = r.astype(BF16).reshape(nb, tl, hw)
        t_ref[0] = r.T.reshape(SB_HEADS, SB_DH, tl)

    ret_units = [functools.partial(ret_unit, c0, h) for c0 in range(0, tl, fused_chunk) for h in range(RET_HEADS)]
    mxu_units = [sb_q, functools.partial(sb_kv, 5, skb_ref, sk_ref), functools.partial(sb_kv, 6, svb_ref, sv_ref)]
    per = -(-len(ret_units) // len(mxu_units))
    for i, unit in enumerate(mxu_units):
        unit()
        for r in ret_units[i * per:(i + 1) * per]:
            r()

    @pl.when(pl.program_id(1) == pl.num_programs(1) - 1)
    def _():
        sf_ref[0] = s_scr[...]


def _proj(x, w_in_b, cos, sin, *, nb, tl, fused_chunk, cast=()):
    b, t, d = x.shape
    hw = RET_HEADS * RET_D
    grid = (b // nb, t // tl)
    cast_in, cast_out, cast_shapes = _side_cast(cast, grid[0] * grid[1], lambda i, j: i * grid[1] + j)
    row_spec = lambda w: pl.BlockSpec((nb, tl, w), lambda i, j: (i, j, 0))
    scratch = []
    if fused_chunk is None:
        hs = lambda dt: jax.ShapeDtypeStruct((b, SB_HEADS, t, SB_DH), dt)
        hs_spec = pl.BlockSpec((nb, SB_HEADS, tl, SB_DH), lambda i, j: (i, 0, j, 0))
        out_shape = [jax.ShapeDtypeStruct((b, t, 4 * hw), BF16), hs(BF16), hs(F32), hs(F32)]
        out_specs = [row_spec(4 * hw)] + [hs_spec] * 3
    else:
        assert nb == 1 and tl % fused_chunk == 0
        state = (RET_HEADS, RET_D, RET_D)
        out_shape = ([jax.ShapeDtypeStruct((b, t, hw), BF16), jax.ShapeDtypeStruct((b,) + state, F32)]
                     + [jax.ShapeDtypeStruct((b, t, hw), BF16)] * 3
                     + [jax.ShapeDtypeStruct((b, SB_HEADS, SB_DH, t), F32)] * 2)
        out_specs = ([row_spec(hw), pl.BlockSpec((1,) + state, lambda i, j: (i, 0, 0, 0))] + [row_spec(hw)] * 3
                     + [pl.BlockSpec((1, SB_HEADS, SB_DH, tl), lambda i, j: (i, 0, 0, j))] * 2)
        scratch = [pltpu.VMEM(state, F32), pltpu.VMEM((RET_HEADS, fused_chunk, fused_chunk), F32)]
    return pl.pallas_call(
        functools.partial(_proj_kernel, nb=nb, tl=tl, d_model=d, fused_chunk=fused_chunk, n_cast=len(cast)),
        grid=grid,
        in_specs=[row_spec(d), _const_spec(w_in_b.shape),
                  pl.BlockSpec((tl, RET_D), lambda i, j: (j, 0)),
                  pl.BlockSpec((tl, RET_D), lambda i, j: (j, 0))] + cast_in,
        out_specs=out_specs + cast_out, out_shape=out_shape + cast_shapes, scratch_shapes=scratch,
        compiler_params=pltpu.CompilerParams(
            dimension_semantics=("arbitrary", "arbitrary"), vmem_limit_bytes=VMEM_LIMIT),
        name="proj",
    )(x, w_in_b, cos, sin, *cast)


def _memkv_kernel(m_ref, wk_ref, wv_ref, *refs, dh, n_cast):
    cast_in = refs[:n_cast]
    mk_ref, mv_ref, mkb_ref, mvb_ref = refs[n_cast:n_cast + 4]
    cast_out = refs[n_cast + 4:]
    for src, dst in zip(cast_in, cast_out):
        dst[...] = src[...].astype(BF16)
    mb = m_ref[0].astype(BF16)
    for w_ref, o_ref, ob_ref in ((wk_ref, mk_ref, mkb_ref), (wv_ref, mv_ref, mvb_ref)):
        r = _dot(mb, w_ref[...].astype(BF16))
        for h in range(MEM_HEADS):
            rh = r[:, h * dh:(h + 1) * dh]
            o_ref[0, h] = rh
            ob_ref[0, h] = rh.astype(BF16)


def _memkv(mem, wk, wv, cast=()):
    b, n, d = mem.shape
    dh = d // MEM_HEADS
    spec = pl.BlockSpec((1, MEM_HEADS, n, dh), lambda i: (i, 0, 0, 0))
    cast_in, cast_out, cast_shapes = _side_cast(cast, b, lambda i: i)
    return pl.pallas_call(
        functools.partial(_memkv_kernel, dh=dh, n_cast=len(cast)),
        grid=(b,),
        in_specs=[pl.BlockSpec((1, n, d), lambda i: (i, 0, 0)), _const_spec(wk.shape), _const_spec(wv.shape)] + cast_in,
        out_specs=[spec] * 4 + cast_out,
        out_shape=[jax.ShapeDtypeStruct((b, MEM_HEADS, n, dh), F32)] * 2
        + [jax.ShapeDtypeStruct((b, MEM_HEADS, n, dh), BF16)] * 2 + cast_shapes,
        compiler_params=pltpu.CompilerParams(dimension_semantics=("arbitrary",), vmem_limit_bytes=VMEM_LIMIT),
        name="memkv",
    )(mem, wk, wv, *cast)


def _ret_kernel(ret_ref, s0_ref, ro_ref, sf_ref, s_scr, dmask_scr, *, c, n_chunks):
    t = pl.program_id(1)

    @pl.when(t == 0)
    def _():
        _ret_init(s_scr, dmask_scr, s0_ref[...], c)

    hw = RET_HEADS * RET_D
    for s in range(ret_ref.shape[0]):
        for ci in range(n_chunks):
            rows = slice(ci * c, (ci + 1) * c)
            for h in range(RET_HEADS):
                blk = [ret_ref[s, rows, i * hw + h * RET_D:i * hw + (h + 1) * RET_D] for i in range(4)]
                ro_ref[s, rows, h * RET_D:(h + 1) * RET_D] = _ret_chunk(*blk, h, s_scr.at[s], dmask_scr)

    @pl.when(t == pl.num_programs(1) - 1)
    def _():
        sf_ref[...] = s_scr[...]


def _retention(ret, s0, *, nb, c, n_chunks):
    b, t, w = ret.shape
    hw = RET_HEADS * RET_D
    tl = c * n_chunks
    state = (nb, RET_HEADS, RET_D, RET_D)
    st_spec = pl.BlockSpec(state, lambda i, j: (i, 0, 0, 0))
    return pl.pallas_call(
        functools.partial(_ret_kernel, c=c, n_chunks=n_chunks),
        grid=(b // nb, t // tl),
        in_specs=[pl.BlockSpec((nb, tl, w), lambda i, j: (i, j, 0)), st_spec],
        out_specs=[pl.BlockSpec((nb, tl, hw), lambda i, j: (i, j, 0)), st_spec],
        out_shape=[jax.ShapeDtypeStruct((b, t, hw), BF16),
                   jax.ShapeDtypeStruct((b, RET_HEADS, RET_D, RET_D), F32)],
        scratch_shapes=[pltpu.VMEM(state, F32), pltpu.VMEM((RET_HEADS, c, c), F32)],
        compiler_params=pltpu.CompilerParams(
            dimension_semantics=("parallel", "arbitrary"), vmem_limit_bytes=VMEM_LIMIT),
        name="retention",
    )(ret, s0)


SB_SKIP_ABOVE = 106.0


def _suffix_ones(n):
    j = lax.broadcasted_iota(jnp.int32, (n, n), 0)
    s = lax.broadcasted_iota(jnp.int32, (n, n), 1)
    return jnp.where(j >= s, 1.0, 0.0).astype(BF16)


def _sb_weights(z, carry, valid, u):
    sp = jnp.maximum(z, 0.0) + jnp.log(1.0 + jnp.exp2(jnp.abs(z) * -LOG2E))
    if valid is not None:
        sp = jnp.where(valid, sp, 0.0)
    run = _dot(sp.astype(BF16), u) + carry
    a = jnp.exp(z - run)
    if valid is not None:
        a = jnp.where(valid, a, 0.0)
    return a.astype(BF16), run[:, 0:1]


def _sb_live(carry):
    return (jnp.min(carry) < SB_SKIP_ABOVE).astype(jnp.int32)


def _sb_prompt_kernel(q_ref, k_ref, v_ref, *refs, tq, n_cast):
    o_ref = refs[n_cast]
    for src, dst in zip(refs[:n_cast], refs[n_cast + 1:]):
        dst[...] = src[...].astype(BF16)
    qi = pl.program_id(1)
    lanes = 2 * SB_DH
    n_pairs = q_ref.shape[-1] // lanes
    pair = lambda p: slice(p * lanes, (p + 1) * lanes)
    lo_half = lax.broadcasted_iota(jnp.int32, (1, lanes), 1) < SB_DH
    q2 = []
    for p in range(n_pairs):
        q = q_ref[0, :, pair(p)]
        zero = jnp.zeros_like(q)
        q2.append(jnp.concatenate([jnp.where(lo_half, q, zero), jnp.where(lo_half, zero, q)], axis=0))
    m = 2 * n_pairs * tq
    row = lax.broadcasted_iota(jnp.int32, (m, tq), 0) & (tq - 1)
    col = lax.broadcasted_iota(jnp.int32, (m, tq), 1)
    u = _suffix_ones(tq)

    def tile(kt, acc, carry, valid):
        ks = pl.multiple_of(kt * tq, tq)
        z = jnp.concatenate([_dot_nt(q2[p], k_ref[0, pl.ds(ks, tq), pair(p)]) for p in range(n_pairs)], axis=0)
        a, carry = _sb_weights(z, carry, valid, u)
        pv = jnp.concatenate([_dot(a[p * 2 * tq:(p + 1) * 2 * tq], v_ref[0, pl.ds(ks, tq), pair(p)])
                              for p in range(n_pairs)], axis=0)
        return acc + pv, carry

    acc = jnp.zeros((m, lanes), F32)
    carry = jnp.zeros((m, 1), F32)
    acc, carry = tile(qi, acc, carry, col < row)
    no_tile = jnp.where(qi == 0, 1e30, 0.0).astype(F32)
    acc, carry = tile(jnp.maximum(qi - 1, 0), acc, carry + no_tile, None)

    def cond(c):
        return jnp.logical_and(c[0] < qi, c[3] > 0)

    def body(c):
        acc, carry = tile(qi - 1 - c[0], c[1], c[2], None)
        return c[0] + 1, acc, carry, _sb_live(carry)

    _, acc, _, _ = lax.while_loop(cond, body, (jnp.int32(1), acc, carry, _sb_live(carry)))
    o_ref[0] = jnp.concatenate(
        [jnp.where(lo_half, acc[2 * p * tq:(2 * p + 1) * tq], acc[(2 * p + 1) * tq:(2 * p + 2) * tq])
         for p in range(n_pairs)], axis=-1).astype(BF16)


def _sb_prompt(sq, sk, sv, *, tq, cast=()):
    b, t, w = sq.shape
    grid = (b, t // tq)
    q_spec = pl.BlockSpec((1, tq, w), lambda i, j: (i, j, 0))
    kv_spec = pl.BlockSpec((1, t, w), lambda i, j: (i, 0, 0))
    cast_in, cast_out, cast_shapes = _side_cast(cast, grid[0] * grid[1], lambda i, j: i * grid[1] + j)
    return pl.pallas_call(
        functools.partial(_sb_prompt_kernel, tq=tq, n_cast=len(cast)),
        grid=grid,
        in_specs=[q_spec, kv_spec, kv_spec] + cast_in,
        out_specs=[q_spec] + cast_out,
        out_shape=[jax.ShapeDtypeStruct((b, t, w), BF16)] + cast_shapes,
        compiler_params=pltpu.CompilerParams(
            dimension_semantics=("arbitrary", "arbitrary"), vmem_limit_bytes=VMEM_LIMIT),
        name="sb_prompt",
    )(sq, sk, sv, *cast)


def _sb_sample_kernel(q_ref, kn_ref, vn_ref, kct_ref, vct_ref, o_ref, carry_ref, acc_scr, carry_scr, *, tl, tk):
    ns, nh = q_ref.shape[:2]
    n_tiles = kct_ref.shape[3] // tk
    m = ns * nh * tl
    units = [(s, h) for s in range(ns) for h in range(nh)]

    def per_unit(fn):
        return jnp.concatenate([fn(i, s, h) for i, (s, h) in enumerate(units)], axis=0)

    row = lax.broadcasted_iota(jnp.int32, (m, tl), 0) & (tl - 1)
    col = lax.broadcasted_iota(jnp.int32, (m, tl), 1)
    z = per_unit(lambda i, s, h: _dot_nt(q_ref[s, h], kn_ref[s, h].astype(BF16)))
    a, carry = _sb_weights(z, jnp.zeros((m, 1), F32), col < row, _suffix_ones(tl))
    acc_scr[...] = per_unit(lambda i, s, h: _dot(a[i * tl:(i + 1) * tl], vn_ref[s, h].astype(BF16)))
    carry_scr[...] = carry
    u = _suffix_ones(tk)
    for j in reversed(range(n_tiles)):
        @pl.when(jnp.min(carry_scr[...]) < SB_SKIP_ABOVE)
        def _(j=j):
            cols = slice(j * tk, (j + 1) * tk)
            z = per_unit(lambda i, s, h: _dot(q_ref[s, h], kct_ref[s, h, :, cols].astype(BF16)))
            a, carry = _sb_weights(z, carry_scr[...], None, u)
            acc_scr[...] += per_unit(
                lambda i, s, h: _dot_nt(a[i * tl:(i + 1) * tl], vct_ref[s, h, :, cols].astype(BF16)))
            carry_scr[...] = carry

    acc = acc_scr[...]
    for s in range(ns):
        o_ref[s] = jnp.concatenate([acc[(s * nh + h) * tl:(s * nh + h + 1) * tl] for h in range(nh)],
                                   axis=-1).astype(BF16)
    carry_ref[...] = jnp.full(carry_ref.shape, jnp.min(carry_scr[...]), F32)


def _sb_sample_window(sqh, sk_new, sv_new, k_cache_t, v_cache_t, *, tk, window, ns):
    b, nh, tl, dh = sqh.shape
    past = k_cache_t.shape[3]
    assert past % window == 0 and window % tk == 0 and b % ns == 0
    new_spec = pl.BlockSpec((ns, nh, tl, dh), lambda i: (i, 0, 0, 0))
    cache_spec = pl.BlockSpec((ns, nh, dh, window), lambda i: (i, 0, 0, past // window - 1))
    flag_spec = pl.BlockSpec((1, 8, 128), lambda i: (i, 0, 0))
    return pl.pallas_call(
        functools.partial(_sb_sample_kernel, tl=tl, tk=tk),
        grid=(b // ns,),
        in_specs=[new_spec, new_spec, new_spec, cache_spec, cache_spec],
        out_specs=[pl.BlockSpec((ns, tl, nh * dh), lambda i: (i, 0, 0)), flag_spec],
        out_shape=[jax.ShapeDtypeStruct((b, tl, nh * dh), BF16), jax.ShapeDtypeStruct((b // ns, 8, 128), F32)],
        scratch_shapes=[pltpu.VMEM((ns * nh * tl, dh), F32), pltpu.VMEM((ns * nh * tl, 1), F32)],
        compiler_params=pltpu.CompilerParams(dimension_semantics=("parallel",), vmem_limit_bytes=VMEM_LIMIT),
        name="sb_sample",
    )(sqh, sk_new, sv_new, k_cache_t, v_cache_t)


def _sb_sample(sqh, sk_new, sv_new, k_cache_t, v_cache_t, *, tk, window, ns):
    so, carry = _sb_sample_window(sqh, sk_new, sv_new, k_cache_t, v_cache_t, tk=tk, window=window, ns=ns)
    past = k_cache_t.shape[3]
    if window == past:
        return so
    return lax.cond(jnp.min(carry) < SB_SKIP_ABOVE,
                    lambda: _sb_sample_window(sqh, sk_new, sv_new, k_cache_t, v_cache_t, tk=tk, window=past, ns=1)[0],
                    lambda: so)


def _post_kernel(x_ref, ro_ref, so_ref, mk_ref, mv_ref, cpast_ref,
                 wo_ref, ln1g_ref, ln1b_ref, wq_ref, wom_ref, ln2g_ref, ln2b_ref,
                 wup_ref, cw_ref, cb_ref, wdn_ref, ln3g_ref, ln3b_ref,
                 y_ref, cst_ref, prev_scr, act_scr, *, nb, tl, alpha, fb, n_groups):
    t = pl.program_id(1)
    m = nb * tl
    sub = CONV_W - 1

    @pl.when(t == 0)
    def _():
        prev_scr[...] = jnp.zeros_like(prev_scr)
        prev_scr[:, 8 - sub:8, :] = cpast_ref[...]

    d = x_ref.shape[-1]
    hw = ro_ref.shape[-1]
    dh = mk_ref.shape[3]
    d_ff = wdn_ref.shape[0]
    if nb == 1:
        r = tl // n_groups
        groups = [[(0, i * r, r)] for i in range(n_groups)]
    else:
        per = nb // n_groups
        groups = [[(b, b * tl, tl) for b in range(i * per, (i + 1) * per)] for i in range(n_groups)]
    x_all = x_ref[...].reshape(m, d)
    ro_all = ro_ref[...].reshape(m, hw)
    so_all = so_ref[...].reshape(m, hw)

    def mixer_and_memory(segs):
        g0, g1 = segs[0][1], segs[-1][1] + segs[-1][2]
        mix_in = jnp.concatenate([ro_all[g0:g1], so_all[g0:g1]], axis=-1)
        x1 = _ln(alpha * x_all[g0:g1] + _dot(mix_in, wo_ref[...]), ln1g_ref[...], ln1b_ref[...])
        q = _dot(x1.astype(BF16), wq_ref[...])
        qb = q.astype(BF16)
        s = jnp.concatenate([_dot_nt(qb[s0 - g0:s0 - g0 + n, h * dh:(h + 1) * dh], mk_ref[b, h])
                             for b, s0, n in segs for h in range(MEM_HEADS)], axis=0) * (dh ** -0.5)
        e = jnp.exp(s - jnp.max(s, -1, keepdims=True))
        p = (e * (1.0 / jnp.sum(e, -1, keepdims=True))).astype(BF16)
        outs, r0 = [], 0
        for b, s0, n in segs:
            heads = []
            for h in range(MEM_HEADS):
                heads.append(_dot(p[r0:r0 + n], mv_ref[b, h]).astype(BF16))
                r0 += n
            outs.append(jnp.concatenate(heads, axis=-1))
        att_in = outs[0] if len(outs) == 1 else jnp.concatenate(outs, axis=0)
        return _ln(alpha * x1 + _dot(att_in, wom_ref[...]), ln2g_ref[...], ln2b_ref[...])

    x2 = [mixer_and_memory(segs) for segs in groups]
    x2b = [v.astype(BF16) for v in x2]

    sub_idx = lax.broadcasted_iota(jnp.int32, (1, 8, fb), 1)

    def conv(u, prev8, c0):
        g = u.shape[0] // 8
        u3 = u.reshape(g, 8, fb)
        ext = jnp.concatenate([prev8[None], u3], axis=0)
        r1 = pltpu.roll(ext, 1, axis=1)
        r2 = pltpu.roll(r1, 1, axis=1)
        u1 = jnp.where(sub_idx < 1, r1[:-1], r1[1:])
        u2 = jnp.where(sub_idx < 2, r2[:-1], r2[1:])
        cols = slice(c0, c0 + fb)
        c = (cb_ref[:, cols][None] + cw_ref[0:1, cols][None] * u2
             + cw_ref[1:2, cols][None] * u1 + cw_ref[2:3, cols][None] * u3)
        return c.reshape(u.shape), u3[g - 1]

    def conv_group(u, segs, g0, c0, chain):
        parts = []
        for b, s0, n in segs:
            prev8 = chain.get(b)
            if prev8 is None:
                prev8 = prev_scr[b, :, c0:c0 + fb]
            c, chain[b] = conv(u[s0 - g0:s0 - g0 + n], prev8, c0)
            parts.append(c)
        return parts[0] if len(parts) == 1 else jnp.concatenate(parts, axis=0)

    for blk in range(d_ff // fb):
        ca, cg = blk * fb, d_ff + blk * fb
        chain_a, chain_g = {}, {}
        for segs, xg in zip(groups, x2b):
            g0, g1 = segs[0][1], segs[-1][1] + segs[-1][2]
            a = conv_group(_dot(xg, wup_ref[:, ca:ca + fb]), segs, g0, ca, chain_a)
            g = conv_group(_dot(xg, wup_ref[:, cg:cg + fb]), segs, g0, cg, chain_g)
            act_scr[g0:g1, ca:ca + fb] = (_silu(a) * g).astype(BF16)
        for b in range(nb):
            prev_scr[b, :, ca:ca + fb] = chain_a[b]
            prev_scr[b, :, cg:cg + fb] = chain_g[b]

    for segs, x2g in zip(groups, x2):
        g0, g1 = segs[0][1], segs[-1][1] + segs[-1][2]
        f = _dot(act_scr[g0:g1, :], wdn_ref[...])
        y = _ln(alpha * x2g + f, ln3g_ref[...], ln3b_ref[...])
        for b, s0, n in segs:
            y_ref[b, s0 - b * tl:s0 - b * tl + n, :] = y[s0 - g0:s0 - g0 + n]
    cst_ref[...] = prev_scr[:, 8 - sub:8, :]


def _post(x, ro, so, mk_b, mv_b, conv_past, lw, *, nb, tl, n_groups, alpha):
    b, t, d = x.shape
    assert nb == 1 or tl == t
    hw = ro.shape[-1]
    two_ff = lw["w_up"].shape[1]
    row = lambda w: pl.BlockSpec((nb, tl, w), lambda i, j: (i, j, 0))
    mem_spec = pl.BlockSpec((nb,) + mk_b.shape[1:], lambda i, j: (i, 0, 0, 0),
                            pipeline_mode=pl.Buffered(1) if nb == b else None)
    cst_spec = pl.BlockSpec((nb, CONV_W - 1, two_ff), lambda i, j: (i, 0, 0))
    names = ("w_o", "ln1_g", "ln1_b", "w_q_mem", "w_o_mem", "ln2_g", "ln2_b",
             "w_up", "conv_w", "conv_b", "w_down", "ln3_g", "ln3_b")
    weights = [lw[n] for n in names]
    assert (tl if nb == 1 else nb) % n_groups == 0 and tl % 8 == 0
    return pl.pallas_call(
        functools.partial(_post_kernel, nb=nb, tl=tl, alpha=alpha, fb=256, n_groups=n_groups),
        grid=(b // nb, t // tl),
        in_specs=[row(d), row(hw), row(hw), mem_spec, mem_spec, cst_spec] + [_const_spec(w.shape) for w in weights],
        out_specs=[row(d), cst_spec],
        out_shape=[jax.ShapeDtypeStruct((b, t, d), F32), jax.ShapeDtypeStruct((b, CONV_W - 1, two_ff), F32)],
        scratch_shapes=[pltpu.VMEM((nb, 8, two_ff), F32), pltpu.VMEM((nb * tl, two_ff // 2), BF16)],
        compiler_params=pltpu.CompilerParams(
            dimension_semantics=("parallel", "arbitrary"), vmem_limit_bytes=VMEM_LIMIT),
        name="post",
    )(x, ro, so, mk_b, mv_b, conv_past, *weights)


def _rope_tables(first, n):
    half = RET_D // 2
    inv = 1.0 / (ROPE_BASE ** (np.arange(half, dtype=np.float64) / half))
    ang = np.arange(first, first + n, dtype=np.float64)[:, None] * inv[None, :]
    c, s = np.cos(ang).astype(np.float32), np.sin(ang).astype(np.float32)
    return jnp.asarray(np.concatenate([c, c], axis=-1)), jnp.asarray(np.concatenate([-s, s], axis=-1))


def kernel(x_prompt, x_sample, cache_sb_k, cache_sb_v, state_ret, state_ffn_conv, cache_mem_k, cache_mem_v, mem_prompt, w_in, w_o, ln1_g, ln1_b, w_q_mem, w_k_mem, w_v_mem, w_o_mem, ln2_g, ln2_b, w_up, conv_w, conv_b, w_down, ln3_g, ln3_b):
    depth = w_in.shape[0]
    alpha = (2.0 * depth) ** 0.25
    xp, xs = x_prompt, x_sample
    bp, tp, _ = xp.shape
    bs, ts, _ = xs.shape
    past_len = cache_sb_k.shape[3]
    two_ff = w_up.shape[2]
    cos_p, sin_p = _rope_tables(0, tp)
    cos_s, sin_s = _rope_tables(past_len, ts)
    row2 = lambda a: a.reshape(1, -1)
    swap = lambda a: jnp.swapaxes(a, -1, -2)
    outs = [[] for _ in range(10)]
    for l in range(depth):
        lw = {"ln1_g": row2(ln1_g[l]), "ln1_b": row2(ln1_b[l]), "ln2_g": row2(ln2_g[l]), "ln2_b": row2(ln2_b[l]),
              "conv_w": conv_w[l], "conv_b": row2(conv_b[l]), "ln3_g": row2(ln3_g[l]), "ln3_b": row2(ln3_b[l])}

        mk, mv, mk_b, mv_b, w_in_b = _memkv(mem_prompt, w_k_mem[l], w_v_mem[l], cast=(w_in[l],))
        late = ("w_o", "w_q_mem", "w_o_mem", "w_up", "w_down")
        ro, ps, sq, skb, svb, pk_t, pv_t, *late_b = _proj(
            xp, w_in_b, cos_p, sin_p, nb=1, tl=min(tp, PROJ_ROWS), fused_chunk=RET_CHUNK,
            cast=(w_o[l], w_q_mem[l], w_o_mem[l], w_up[l], w_down[l]))
        lw.update(zip(late, late_b))
        mem_shape = cache_mem_k.shape[1:]
        flat = lambda a: a.reshape(-1, mem_shape[-1])
        so, cmk_b, cmv_b = _sb_prompt(sq, skb, svb, tq=SB_TILE, cast=(flat(cache_mem_k[l]), flat(cache_mem_v[l])))
        xp, pc = _post(xp, ro, so, mk_b, mv_b, jnp.zeros((bp, CONV_W - 1, two_ff), F32), lw,
                       nb=1, tl=min(tp, POST_ROWS), n_groups=POST_GROUPS, alpha=alpha)

        ret, sqh, sk, sv = _proj(xs, w_in_b, cos_s, sin_s, nb=bs, tl=ts, fused_chunk=None)
        ro, ss = _retention(ret, state_ret[l], nb=min(bs, SAMPLE_STREAMS), c=min(REF_CHUNK, ts), n_chunks=1)
        so = _sb_sample(sqh, sk, sv, swap(cache_sb_k[l]), swap(cache_sb_v[l]), tk=SB_TILE,
                        window=min(SB_WINDOW, past_len), ns=min(bs, SAMPLE_STREAMS))
        xs, sc = _post(xs, ro, so, cmk_b.reshape(mem_shape), cmv_b.reshape(mem_shape),
                       state_ffn_conv[l], lw, nb=min(bs, SAMPLE_POST_STREAMS), tl=ts, n_groups=1, alpha=alpha)
        for lst, val in zip(outs, (swap(pk_t), swap(pv_t), ps, pc, mk, mv, sk, sv, ss, sc)):
            lst.append(val)
    return (xp, xs) + tuple(jnp.stack(o) for o in outs)
```

```python
import functools
import math

import jax
import jax.numpy as jnp
import numpy as np
from jax import lax
from jax.experimental import pallas as pl
from jax.experimental.pallas import tpu as pltpu

F32 = jnp.float32
BF16 = jnp.bfloat16

RET_HEADS = 4
RET_D = 128
SB_HEADS = 8
SB_DH = 64
MEM_HEADS = 4
CONV_W = 3
ROPE_BASE = 10000.0
LN_EPS = 1e-5
RMS_EPS = 1e-6
LOG2E = math.log2(math.e)
REF_CHUNK = 64

VMEM_LIMIT = 56 * 1024 * 1024

PROJ_ROWS = 1024
RET_CHUNK = 256
SB_TILE = 256
SB_WINDOW = 512
POST_ROWS = 512
POST_GROUPS = 2
SAMPLE_STREAMS = 4
SAMPLE_POST_STREAMS = 8


def _const_spec(shape):
    nd = len(shape)
    return pl.BlockSpec(shape, lambda *_: (0,) * nd, pipeline_mode=pl.Buffered(1))


def _ln(x, g, b):
    mu = jnp.mean(x, -1, keepdims=True)
    xc = x - mu
    var = jnp.mean(xc * xc, -1, keepdims=True)
    return xc * lax.rsqrt(var + LN_EPS) * g + b


def _silu(x):
    return x * (1.0 / (1.0 + jnp.exp(-x)))


def _dot(a, b):
    return jnp.dot(a, b, preferred_element_type=F32)


def _dot_nt(a, b):
    return lax.dot_general(a, b, (((1,), (1,)), ((), ())), preferred_element_type=F32)


def _ret_log_gamma(h):
    return math.log1p(-(2.0 ** (-5.0 - h)))


def _ret_init(s_scr, dmask_scr, s0, c):
    s_scr[...] = jnp.zeros_like(s_scr) if s0 is None else s0
    row = lax.broadcasted_iota(jnp.int32, (c, c), 0)
    col = lax.broadcasted_iota(jnp.int32, (c, c), 1)
    diff = (row - col).astype(F32)
    for h in range(RET_HEADS):
        dmask_scr[h] = jnp.where(diff >= 0, jnp.exp(_ret_log_gamma(h) * jnp.maximum(diff, 0.0)), 0.0)


def _ret_chunk(q, k, v, g, h, s_scr, dmask_scr):
    c = q.shape[0]
    log_g = _ret_log_gamma(h)
    idx = lax.broadcasted_iota(jnp.int32, (c, 1), 0).astype(F32)
    scores = _dot_nt(q, k) * dmask_scr[h]
    o = _dot(scores.astype(BF16), v)
    s_prev = s_scr[h]
    qd = (q.astype(F32) * jnp.exp(log_g * (idx + 1.0))).astype(BF16)
    o = o + _dot(qd, s_prev.astype(BF16))
    kd = (k.astype(F32) * jnp.exp(log_g * (c - 1.0 - idx))).astype(BF16)
    s_scr[h] = math.exp(log_g * c) * s_prev + _dot(kd.T, v)
    o = o * lax.rsqrt(jnp.mean(o * o, -1, keepdims=True) + RMS_EPS)
    return (o * _silu(g.astype(F32))).astype(BF16)


def _side_cast(weights, n_steps, step_of):
    in_specs, out_specs, out_shapes = [], [], []
    for w in weights:
        rows, cols = w.shape
        n_chunks = max(n for n in range(1, n_steps + 1) if (rows // 16) % n == 0)
        spec = pl.BlockSpec((rows // n_chunks, cols),
                            lambda *idx, last=n_chunks - 1: (jnp.minimum(step_of(*idx), last), 0))
        in_specs.append(spec)
        out_specs.append(spec)
        out_shapes.append(jax.ShapeDtypeStruct(w.shape, BF16))
    return in_specs, out_specs, out_shapes


def _proj_kernel(x_ref, w_ref, cos_ref, sin_ref, *refs, nb, tl, d_model, fused_chunk, n_cast):
    cast_in, refs = refs[:n_cast], refs[n_cast:]
    if fused_chunk is None:
        ret_ref, sqh_ref, sk_ref, sv_ref = refs[:4]
        cast_out = refs[4:4 + n_cast]
    else:
        ro_ref, sf_ref, sq_ref, skb_ref, svb_ref, sk_ref, sv_ref = refs[:7]
        cast_out = refs[7:7 + n_cast]
        s_scr, dmask_scr = refs[7 + n_cast:]
    for src, dst in zip(cast_in, cast_out):
        dst[...] = src[...].astype(BF16)
    m = nb * tl
    xb = x_ref[...].reshape(m, d_model).astype(BF16)
    cos = jnp.broadcast_to(cos_ref[...][None], (nb, tl, RET_D)).reshape(m, RET_D)
    sin = jnp.broadcast_to(sin_ref[...][None], (nb, tl, RET_D)).reshape(m, RET_D)
    hw = RET_HEADS * RET_D

    def seg(i):
        return _dot(xb, w_ref[:, i * hw:(i + 1) * hw])

    def rope(r, scale):
        outs = []
        for h in range(RET_HEADS):
            xh = r[:, h * RET_D:(h + 1) * RET_D]
            o = xh * cos + pltpu.roll(xh, RET_D // 2, axis=1) * sin
            outs.append(o * scale if scale != 1.0 else o)
        return jnp.concatenate(outs, axis=-1)

    if fused_chunk is not None:
        @pl.when(pl.program_id(1) == 0)
        def _():
            _ret_init(s_scr, dmask_scr, None, fused_chunk)

    ret_in = [rope(seg(0), 1.0).astype(BF16), rope(seg(1), RET_D ** -0.5).astype(BF16),
              seg(2).astype(BF16), seg(3).astype(BF16)]
    def put_heads(ref, val):
        for h in range(SB_HEADS):
            ref[:, h, :, :] = val[:, h * SB_DH:(h + 1) * SB_DH].astype(ref.dtype).reshape(nb, tl, SB_DH)

    if fused_chunk is None:
        for i, val in enumerate(ret_in):
            ret_ref[:, :, i * hw:(i + 1) * hw] = val.reshape(nb, tl, hw)
        put_heads(sqh_ref, seg(4) * (SB_DH ** -0.5))
        put_heads(sk_ref, seg(5))
        put_heads(sv_ref, seg(6))
        return

    def ret_unit(c0, h):
        blk = [val[c0:c0 + fused_chunk, h * RET_D:(h + 1) * RET_D] for val in ret_in]
        ro_ref[0, c0:c0 + fused_chunk, h * RET_D:(h + 1) * RET_D] = _ret_chunk(*blk, h, s_scr, dmask_scr)

    def sb_q():
        sq_ref[...] = (seg(4) * (SB_DH ** -0.5)).astype(BF16).reshape(nb, tl, hw)

    def sb_kv(i, b_ref, t_ref):
        r = seg(i)
        b_ref[...] = r.astype(BF16).reshape(nb, tl, hw)
        t_ref[0] = r.T.reshape(SB_HEADS, SB_DH, tl)

    ret_units = [functools.partial(ret_unit, c0, h) for c0 in range(0, tl, fused_chunk) for h in range(RET_HEADS)]
    mxu_units = [sb_q, functools.partial(sb_kv, 5, skb_ref, sk_ref), functools.partial(sb_kv, 6, svb_ref, sv_ref)]
    per = -(-len(ret_units) // len(mxu_units))
    for i, unit in enumerate(mxu_units):
        unit()
        for r in ret_units[i * per:(i + 1) * per]:
            r()

    @pl.when(pl.program_id(1) == pl.num_programs(1) - 1)
    def _():
        sf_ref[0] = s_scr[...]


def _proj(x, w_in_b, cos, sin, *, nb, tl, fused_chunk, cast=()):
    b, t, d = x.shape
    hw = RET_HEADS * RET_D
    grid = (b // nb, t // tl)
    cast_in, cast_out, cast_shapes = _side_cast(cast, grid[0] * grid[1], lambda i, j: i * grid[1] + j)
    row_spec = lambda w: pl.BlockSpec((nb, tl, w), lambda i, j: (i, j, 0))
    scratch = []
    if fused_chunk is None:
        hs = lambda dt: jax.ShapeDtypeStruct((b, SB_HEADS, t, SB_DH), dt)
        hs_spec = pl.BlockSpec((nb, SB_HEADS, tl, SB_DH), lambda i, j: (i, 0, j, 0))
        out_shape = [jax.ShapeDtypeStruct((b, t, 4 * hw), BF16), hs(BF16), hs(F32), hs(F32)]
        out_specs = [row_spec(4 * hw)] + [hs_spec] * 3
    else:
        assert nb == 1 and tl % fused_chunk == 0
        state = (RET_HEADS, RET_D, RET_D)
        out_shape = ([jax.ShapeDtypeStruct((b, t, hw), BF16), jax.ShapeDtypeStruct((b,) + state, F32)]
                     + [jax.ShapeDtypeStruct((b, t, hw), BF16)] * 3
                     + [jax.ShapeDtypeStruct((b, SB_HEADS, SB_DH, t), F32)] * 2)
        out_specs = ([row_spec(hw), pl.BlockSpec((1,) + state, lambda i, j: (i, 0, 0, 0))] + [row_spec(hw)] * 3
                     + [pl.BlockSpec((1, SB_HEADS, SB_DH, tl), lambda i, j: (i, 0, 0, j))] * 2)
        scratch = [pltpu.VMEM(state, F32), pltpu.VMEM((RET_HEADS, fused_chunk, fused_chunk), F32)]
    return pl.pallas_call(
        functools.partial(_proj_kernel, nb=nb, tl=tl, d_model=d, fused_chunk=fused_chunk, n_cast=len(cast)),
        grid=grid,
        in_specs=[row_spec(d), _const_spec(w_in_b.shape),
                  pl.BlockSpec((tl, RET_D), lambda i, j: (j, 0)),
                  pl.BlockSpec((tl, RET_D), lambda i, j: (j, 0))] + cast_in,
        out_specs=out_specs + cast_out, out_shape=out_shape + cast_shapes, scratch_shapes=scratch,
        compiler_params=pltpu.CompilerParams(
            dimension_semantics=("arbitrary", "arbitrary"), vmem_limit_bytes=VMEM_LIMIT),
        name="proj",
    )(x, w_in_b, cos, sin, *cast)


def _memkv_kernel(m_ref, wk_ref, wv_ref, *refs, dh, n_cast):
    cast_in = refs[:n_cast]
    mk_ref, mv_ref, mkb_ref, mvb_ref = refs[n_cast:n_cast + 4]
    cast_out = refs[n_cast + 4:]
    for src, dst in zip(cast_in, cast_out):
        dst[...] = src[...].astype(BF16)
    mb = m_ref[0].astype(BF16)
    for w_ref, o_ref, ob_ref in ((wk_ref, mk_ref, mkb_ref), (wv_ref, mv_ref, mvb_ref)):
        r = _dot(mb, w_ref[...].astype(BF16))
        for h in range(MEM_HEADS):
            rh = r[:, h * dh:(h + 1) * dh]
            o_ref[0, h] = rh
            ob_ref[0, h] = rh.astype(BF16)


def _memkv(mem, wk, wv, cast=()):
    b, n, d = mem.shape
    dh = d // MEM_HEADS
    spec = pl.BlockSpec((1, MEM_HEADS, n, dh), lambda i: (i, 0, 0, 0))
    cast_in, cast_out, cast_shapes = _side_cast(cast, b, lambda i: i)
    return pl.pallas_call(
        functools.partial(_memkv_kernel, dh=dh, n_cast=len(cast)),
        grid=(b,),
        in_specs=[pl.BlockSpec((1, n, d), lambda i: (i, 0, 0)), _const_spec(wk.shape), _const_spec(wv.shape)] + cast_in,
        out_specs=[spec] * 4 + cast_out,
        out_shape=[jax.ShapeDtypeStruct((b, MEM_HEADS, n, dh), F32)] * 2
        + [jax.ShapeDtypeStruct((b, MEM_HEADS, n, dh), BF16)] * 2 + cast_shapes,
        compiler_params=pltpu.CompilerParams(dimension_semantics=("arbitrary",), vmem_limit_bytes=VMEM_LIMIT),
        name="memkv",
    )(mem, wk, wv, *cast)


def _ret_kernel(ret_ref, s0_ref, ro_ref, sf_ref, s_scr, dmask_scr, *, c, n_chunks):
    t = pl.program_id(1)

    @pl.when(t == 0)
    def _():
        _ret_init(s_scr, dmask_scr, s0_ref[...], c)

    hw = RET_HEADS * RET_D
    for s in range(ret_ref.shape[0]):
        for ci in range(n_chunks):
            rows = slice(ci * c, (ci + 1) * c)
            for h in range(RET_HEADS):
                blk = [ret_ref[s, rows, i * hw + h * RET_D:i * hw + (h + 1) * RET_D] for i in range(4)]
                ro_ref[s, rows, h * RET_D:(h + 1) * RET_D] = _ret_chunk(*blk, h, s_scr.at[s], dmask_scr)

    @pl.when(t == pl.num_programs(1) - 1)
    def _():
        sf_ref[...] = s_scr[...]


def _retention(ret, s0, *, nb, c, n_chunks):
    b, t, w = ret.shape
    hw = RET_HEADS * RET_D
    tl = c * n_chunks
    state = (nb, RET_HEADS, RET_D, RET_D)
    st_spec = pl.BlockSpec(state, lambda i, j: (i, 0, 0, 0))
    return pl.pallas_call(
        functools.partial(_ret_kernel, c=c, n_chunks=n_chunks),
        grid=(b // nb, t // tl),
        in_specs=[pl.BlockSpec((nb, tl, w), lambda i, j: (i, j, 0)), st_spec],
        out_specs=[pl.BlockSpec((nb, tl, hw), lambda i, j: (i, j, 0)), st_spec],
        out_shape=[jax.ShapeDtypeStruct((b, t, hw), BF16),
                   jax.ShapeDtypeStruct((b, RET_HEADS, RET_D, RET_D), F32)],
        scratch_shapes=[pltpu.VMEM(state, F32), pltpu.VMEM((RET_HEADS, c, c), F32)],
        compiler_params=pltpu.CompilerParams(
            dimension_semantics=("parallel", "arbitrary"), vmem_limit_bytes=VMEM_LIMIT),
        name="retention",
    )(ret, s0)


SB_SKIP_ABOVE = 106.0


def _suffix_ones(n):
    j = lax.broadcasted_iota(jnp.int32, (n, n), 0)
    s = lax.broadcasted_iota(jnp.int32, (n, n), 1)
    return jnp.where(j > s, 1.0, 0.0).astype(BF16)


def _sb_weights(z, carry, valid, u):
    sp = jnp.maximum(z, 0.0) + jnp.log(1.0 + jnp.exp2(jnp.abs(z) * -LOG2E))
    if valid is not None:
        sp = jnp.where(valid, sp, 0.0)
    later = _dot(sp.astype(BF16), u) + carry
    a = jnp.exp((z - sp) - later)
    if valid is not None:
        a = jnp.where(valid, a, 0.0)
    return a.astype(BF16), later[:, 0:1] + sp[:, 0:1]


def _sb_live(carry):
    return (jnp.min(carry) < SB_SKIP_ABOVE).astype(jnp.int32)


def _sb_prompt_kernel(q_ref, k_ref, v_ref, *refs, tq, n_cast):
    o_ref = refs[n_cast]
    for src, dst in zip(refs[:n_cast], refs[n_cast + 1:]):
        dst[...] = src[...].astype(BF16)
    qi = pl.program_id(1)
    lanes = 2 * SB_DH
    n_pairs = q_ref.shape[-1] // lanes
    pair = lambda p: slice(p * lanes, (p + 1) * lanes)
    lo_half = lax.broadcasted_iota(jnp.int32, (1, lanes), 1) < SB_DH
    q2 = []
    for p in range(n_pairs):
        q = q_ref[0, :, pair(p)]
        zero = jnp.zeros_like(q)
        q2.append(jnp.concatenate([jnp.where(lo_half, q, zero), jnp.where(lo_half, zero, q)], axis=0))
    m = 2 * n_pairs * tq
    row = lax.broadcasted_iota(jnp.int32, (m, tq), 0) & (tq - 1)
    col = lax.broadcasted_iota(jnp.int32, (m, tq), 1)
    u = _suffix_ones(tq)

    def tile(kt, acc, carry, valid):
        ks = pl.multiple_of(kt * tq, tq)
        z = jnp.concatenate([_dot_nt(q2[p], k_ref[0, pl.ds(ks, tq), pair(p)]) for p in range(n_pairs)], axis=0)
        a, carry = _sb_weights(z, carry, valid, u)
        pv = jnp.concatenate([_dot(a[p * 2 * tq:(p + 1) * 2 * tq], v_ref[0, pl.ds(ks, tq), pair(p)])
                              for p in range(n_pairs)], axis=0)
        return acc + pv, carry

    acc = jnp.zeros((m, lanes), F32)
    carry = jnp.zeros((m, 1), F32)
    acc, carry = tile(qi, acc, carry, col < row)
    no_tile = jnp.where(qi == 0, 1e30, 0.0).astype(F32)
    acc, carry = tile(jnp.maximum(qi - 1, 0), acc, carry + no_tile, None)

    def cond(c):
        return jnp.logical_and(c[0] < qi, c[3] > 0)

    def body(c):
        acc, carry = tile(qi - 1 - c[0], c[1], c[2], None)
        return c[0] + 1, acc, carry, _sb_live(carry)

    _, acc, _, _ = lax.while_loop(cond, body, (jnp.int32(1), acc, carry, _sb_live(carry)))
    o_ref[0] = jnp.concatenate(
        [jnp.where(lo_half, acc[2 * p * tq:(2 * p + 1) * tq], acc[(2 * p + 1) * tq:(2 * p + 2) * tq])
         for p in range(n_pairs)], axis=-1).astype(BF16)


def _sb_prompt(sq, sk, sv, *, tq, cast=()):
    b, t, w = sq.shape
    grid = (b, t // tq)
    q_spec = pl.BlockSpec((1, tq, w), lambda i, j: (i, j, 0))
    kv_spec = pl.BlockSpec((1, t, w), lambda i, j: (i, 0, 0))
    cast_in, cast_out, cast_shapes = _side_cast(cast, grid[0] * grid[1], lambda i, j: i * grid[1] + j)
    return pl.pallas_call(
        functools.partial(_sb_prompt_kernel, tq=tq, n_cast=len(cast)),
        grid=grid,
        in_specs=[q_spec, kv_spec, kv_spec] + cast_in,
        out_specs=[q_spec] + cast_out,
        out_shape=[jax.ShapeDtypeStruct((b, t, w), BF16)] + cast_shapes,
        compiler_params=pltpu.CompilerParams(
            dimension_semantics=("arbitrary", "arbitrary"), vmem_limit_bytes=VMEM_LIMIT),
        name="sb_prompt",
    )(sq, sk, sv, *cast)


def _sb_sample_kernel(q_ref, kn_ref, vn_ref, kct_ref, vct_ref, o_ref, carry_ref, acc_scr, carry_scr, *, tl, tk):
    ns, nh = q_ref.shape[:2]
    n_tiles = kct_ref.shape[3] // tk
    m = ns * nh * tl
    units = [(s, h) for s in range(ns) for h in range(nh)]

    def per_unit(fn):
        return jnp.concatenate([fn(i, s, h) for i, (s, h) in enumerate(units)], axis=0)

    row = lax.broadcasted_iota(jnp.int32, (m, tl), 0) & (tl - 1)
    col = lax.broadcasted_iota(jnp.int32, (m, tl), 1)
    z = per_unit(lambda i, s, h: _dot_nt(q_ref[s, h], kn_ref[s, h].astype(BF16)))
    a, carry = _sb_weights(z, jnp.zeros((m, 1), F32), col < row, _suffix_ones(tl))
    acc_scr[...] = per_unit(lambda i, s, h: _dot(a[i * tl:(i + 1) * tl], vn_ref[s, h].astype(BF16)))
    carry_scr[...] = carry
    u = _suffix_ones(tk)
    for j in reversed(range(n_tiles)):
        @pl.when(jnp.min(carry_scr[...]) < SB_SKIP_ABOVE)
        def _(j=j):
            cols = slice(j * tk, (j + 1) * tk)
            z = per_unit(lambda i, s, h: _dot(q_ref[s, h], kct_ref[s, h, :, cols].astype(BF16)))
            a, carry = _sb_weights(z, carry_scr[...], None, u)
            acc_scr[...] += per_unit(
                lambda i, s, h: _dot_nt(a[i * tl:(i + 1) * tl], vct_ref[s, h, :, cols].astype(BF16)))
            carry_scr[...] = carry

    acc = acc_scr[...]
    for s in range(ns):
        o_ref[s] = jnp.concatenate([acc[(s * nh + h) * tl:(s * nh + h + 1) * tl] for h in range(nh)],
                                   axis=-1).astype(BF16)
    carry_ref[...] = jnp.full(carry_ref.shape, jnp.min(carry_scr[...]), F32)


def _sb_sample_window(sqh, sk_new, sv_new, k_cache_t, v_cache_t, *, tk, window, ns):
    b, nh, tl, dh = sqh.shape
    past = k_cache_t.shape[3]
    assert past % window == 0 and window % tk == 0 and b % ns == 0
    new_spec = pl.BlockSpec((ns, nh, tl, dh), lambda i: (i, 0, 0, 0))
    cache_spec = pl.BlockSpec((ns, nh, dh, window), lambda i: (i, 0, 0, past // window - 1))
    flag_spec = pl.BlockSpec((1, 8, 128), lambda i: (i, 0, 0))
    return pl.pallas_call(
        functools.partial(_sb_sample_kernel, tl=tl, tk=tk),
        grid=(b // ns,),
        in_specs=[new_spec, new_spec, new_spec, cache_spec, cache_spec],
        out_specs=[pl.BlockSpec((ns, tl, nh * dh), lambda i: (i, 0, 0)), flag_spec],
        out_shape=[jax.ShapeDtypeStruct((b, tl, nh * dh), BF16), jax.ShapeDtypeStruct((b // ns, 8, 128), F32)],
        scratch_shapes=[pltpu.VMEM((ns * nh * tl, dh), F32), pltpu.VMEM((ns * nh * tl, 1), F32)],
        compiler_params=pltpu.CompilerParams(dimension_semantics=("parallel",), vmem_limit_bytes=VMEM_LIMIT),
        name="sb_sample",
    )(sqh, sk_new, sv_new, k_cache_t, v_cache_t)


def _sb_sample(sqh, sk_new, sv_new, k_cache_t, v_cache_t, *, tk, window, ns):
    so, carry = _sb_sample_window(sqh, sk_new, sv_new, k_cache_t, v_cache_t, tk=tk, window=window, ns=ns)
    past = k_cache_t.shape[3]
    if window == past:
        return so
    return lax.cond(jnp.min(carry) < SB_SKIP_ABOVE,
                    lambda: _sb_sample_window(sqh, sk_new, sv_new, k_cache_t, v_cache_t, tk=tk, window=past, ns=1)[0],
                    lambda: so)


def _post_kernel(x_ref, ro_ref, so_ref, mk_ref, mv_ref, cpast_ref,
                 wo_ref, ln1g_ref, ln1b_ref, wq_ref, wom_ref, ln2g_ref, ln2b_ref,
                 wup_ref, cw_ref, cb_ref, wdn_ref, ln3g_ref, ln3b_ref,
                 y_ref, cst_ref, prev_scr, act_scr, *, nb, tl, alpha, fb, n_groups):
    t = pl.program_id(1)
    m = nb * tl
    sub = CONV_W - 1

    @pl.when(t == 0)
    def _():
        prev_scr[...] = jnp.zeros_like(prev_scr)
        prev_scr[:, 8 - sub:8, :] = cpast_ref[...]

    d = x_ref.shape[-1]
    hw = ro_ref.shape[-1]
    dh = mk_ref.shape[3]
    d_ff = wdn_ref.shape[0]
    if nb == 1:
        r = tl // n_groups
        groups = [[(0, i * r, r)] for i in range(n_groups)]
    else:
        per = nb // n_groups
        groups = [[(b, b * tl, tl) for b in range(i * per, (i + 1) * per)] for i in range(n_groups)]
    x_all = x_ref[...].reshape(m, d)
    ro_all = ro_ref[...].reshape(m, hw)
    so_all = so_ref[...].reshape(m, hw)

    def mixer_and_memory(segs):
        g0, g1 = segs[0][1], segs[-1][1] + segs[-1][2]
        mix_in = jnp.concatenate([ro_all[g0:g1], so_all[g0:g1]], axis=-1)
        x1 = _ln(alpha * x_all[g0:g1] + _dot(mix_in, wo_ref[...]), ln1g_ref[...], ln1b_ref[...])
        q = _dot(x1.astype(BF16), wq_ref[...])
        qb = q.astype(BF16)
        s = jnp.concatenate([_dot_nt(qb[s0 - g0:s0 - g0 + n, h * dh:(h + 1) * dh], mk_ref[b, h])
                             for b, s0, n in segs for h in range(MEM_HEADS)], axis=0) * (dh ** -0.5)
        e = jnp.exp(s - jnp.max(s, -1, keepdims=True))
        p = (e * (1.0 / jnp.sum(e, -1, keepdims=True))).astype(BF16)
        outs, r0 = [], 0
        for b, s0, n in segs:
            heads = []
            for h in range(MEM_HEADS):
                heads.append(_dot(p[r0:r0 + n], mv_ref[b, h]).astype(BF16))
                r0 += n
            outs.append(jnp.concatenate(heads, axis=-1))
        att_in = outs[0] if len(outs) == 1 else jnp.concatenate(outs, axis=0)
        return _ln(alpha * x1 + _dot(att_in, wom_ref[...]), ln2g_ref[...], ln2b_ref[...])

    x2 = [mixer_and_memory(segs) for segs in groups]
    x2b = [v.astype(BF16) for v in x2]

    sub_idx = lax.broadcasted_iota(jnp.int32, (1, 8, fb), 1)

    def conv(u, prev8, c0):
        g = u.shape[0] // 8
        u3 = u.reshape(g, 8, fb)
        ext = jnp.concatenate([prev8[None], u3], axis=0)
        r1 = pltpu.roll(ext, 1, axis=1)
        r2 = pltpu.roll(r1, 1, axis=1)
        u1 = jnp.where(sub_idx < 1, r1[:-1], r1[1:])
        u2 = jnp.where(sub_idx < 2, r2[:-1], r2[1:])
        cols = slice(c0, c0 + fb)
        c = (cb_ref[:, cols][None] + cw_ref[0:1, cols][None] * u2
             + cw_ref[1:2, cols][None] * u1 + cw_ref[2:3, cols][None] * u3)
        return c.reshape(u.shape), u3[g - 1]

    def conv_group(u, segs, g0, c0, chain):
        parts = []
        for b, s0, n in segs:
            prev8 = chain.get(b)
            if prev8 is None:
                prev8 = prev_scr[b, :, c0:c0 + fb]
            c, chain[b] = conv(u[s0 - g0:s0 - g0 + n], prev8, c0)
            parts.append(c)
        return parts[0] if len(parts) == 1 else jnp.concatenate(parts, axis=0)

    for blk in range(d_ff // fb):
        ca, cg = blk * fb, d_ff + blk * fb
        chain_a, chain_g = {}, {}
        for segs, xg in zip(groups, x2b):
            g0, g1 = segs[0][1], segs[-1][1] + segs[-1][2]
            a = conv_group(_dot(xg, wup_ref[:, ca:ca + fb]), segs, g0, ca, chain_a)
            g = conv_group(_dot(xg, wup_ref[:, cg:cg + fb]), segs, g0, cg, chain_g)
            act_scr[g0:g1, ca:ca + fb] = (_silu(a) * g).astype(BF16)
        for b in range(nb):
            prev_scr[b, :, ca:ca + fb] = chain_a[b]
            prev_scr[b, :, cg:cg + fb] = chain_g[b]

    for segs, x2g in zip(groups, x2):
        g0, g1 = segs[0][1], segs[-1][1] + segs[-1][2]
        f = _dot(act_scr[g0:g1, :], wdn_ref[...])
        y = _ln(alpha * x2g + f, ln3g_ref[...], ln3b_ref[...])
        for b, s0, n in segs:
            y_ref[b, s0 - b * tl:s0 - b * tl + n, :] = y[s0 - g0:s0 - g0 + n]
    cst_ref[...] = prev_scr[:, 8 - sub:8, :]


def _post(x, ro, so, mk_b, mv_b, conv_past, lw, *, nb, tl, n_groups, alpha):
    b, t, d = x.shape
    assert nb == 1 or tl == t
    hw = ro.shape[-1]
    two_ff = lw["w_up"].shape[1]
    row = lambda w: pl.BlockSpec((nb, tl, w), lambda i, j: (i, j, 0))
    mem_spec = pl.BlockSpec((nb,) + mk_b.shape[1:], lambda i, j: (i, 0, 0, 0),
                            pipeline_mode=pl.Buffered(1) if nb == b else None)
    cst_spec = pl.BlockSpec((nb, CONV_W - 1, two_ff), lambda i, j: (i, 0, 0))
    names = ("w_o", "ln1_g", "ln1_b", "w_q_mem", "w_o_mem", "ln2_g", "ln2_b",
             "w_up", "conv_w", "conv_b", "w_down", "ln3_g", "ln3_b")
    weights = [lw[n] for n in names]
    assert (tl if nb == 1 else nb) % n_groups == 0 and tl % 8 == 0
    return pl.pallas_call(
        functools.partial(_post_kernel, nb=nb, tl=tl, alpha=alpha, fb=256, n_groups=n_groups),
        grid=(b // nb, t // tl),
        in_specs=[row(d), row(hw), row(hw), mem_spec, mem_spec, cst_spec] + [_const_spec(w.shape) for w in weights],
        out_specs=[row(d), cst_spec],
        out_shape=[jax.ShapeDtypeStruct((b, t, d), F32), jax.ShapeDtypeStruct((b, CONV_W - 1, two_ff), F32)],
        scratch_shapes=[pltpu.VMEM((nb, 8, two_ff), F32), pltpu.VMEM((nb * tl, two_ff // 2), BF16)],
        compiler_params=pltpu.CompilerParams(
            dimension_semantics=("parallel", "arbitrary"), vmem_limit_bytes=VMEM_LIMIT),
        name="post",
    )(x, ro, so, mk_b, mv_b, conv_past, *weights)


def _rope_tables(first, n):
    half = RET_D // 2
    inv = 1.0 / (ROPE_BASE ** (np.arange(half, dtype=np.float64) / half))
    ang = np.arange(first, first + n, dtype=np.float64)[:, None] * inv[None, :]
    c, s = np.cos(ang).astype(np.float32), np.sin(ang).astype(np.float32)
    return jnp.asarray(np.concatenate([c, c], axis=-1)), jnp.asarray(np.concatenate([-s, s], axis=-1))


def kernel(x_prompt, x_sample, cache_sb_k, cache_sb_v, state_ret, state_ffn_conv, cache_mem_k, cache_mem_v, mem_prompt, w_in, w_o, ln1_g, ln1_b, w_q_mem, w_k_mem, w_v_mem, w_o_mem, ln2_g, ln2_b, w_up, conv_w, conv_b, w_down, ln3_g, ln3_b):
    depth = w_in.shape[0]
    alpha = (2.0 * depth) ** 0.25
    xp, xs = x_prompt, x_sample
    bp, tp, _ = xp.shape
    bs, ts, _ = xs.shape
    past_len = cache_sb_k.shape[3]
    two_ff = w_up.shape[2]
    cos_p, sin_p = _rope_tables(0, tp)
    cos_s, sin_s = _rope_tables(past_len, ts)
    row2 = lambda a: a.reshape(1, -1)
    swap = lambda a: jnp.swapaxes(a, -1, -2)
    outs = [[] for _ in range(10)]
    for l in range(depth):
        lw = {"ln1_g": row2(ln1_g[l]), "ln1_b": row2(ln1_b[l]), "ln2_g": row2(ln2_g[l]), "ln2_b": row2(ln2_b[l]),
              "conv_w": conv_w[l], "conv_b": row2(conv_b[l]), "ln3_g": row2(ln3_g[l]), "ln3_b": row2(ln3_b[l])}

        mk, mv, mk_b, mv_b, w_in_b = _memkv(mem_prompt, w_k_mem[l], w_v_mem[l], cast=(w_in[l],))
        late = ("w_o", "w_q_mem", "w_o_mem", "w_up", "w_down")
        ro, ps, sq, skb, svb, pk_t, pv_t, *late_b = _proj(
            xp, w_in_b, cos_p, sin_p, nb=1, tl=min(tp, PROJ_ROWS), fused_chunk=RET_CHUNK,
            cast=(w_o[l], w_q_mem[l], w_o_mem[l], w_up[l], w_down[l]))
        lw.update(zip(late, late_b))
        mem_shape = cache_mem_k.shape[1:]
        flat = lambda a: a.reshape(-1, mem_shape[-1])
        so, cmk_b, cmv_b = _sb_prompt(sq, skb, svb, tq=SB_TILE, cast=(flat(cache_mem_k[l]), flat(cache_mem_v[l])))
        xp, pc = _post(xp, ro, so, mk_b, mv_b, jnp.zeros((bp, CONV_W - 1, two_ff), F32), lw,
                       nb=1, tl=min(tp, POST_ROWS), n_groups=POST_GROUPS, alpha=alpha)

        ret, sqh, sk, sv = _proj(xs, w_in_b, cos_s, sin_s, nb=bs, tl=ts, fused_chunk=None)
        ro, ss = _retention(ret, state_ret[l], nb=min(bs, SAMPLE_STREAMS), c=min(REF_CHUNK, ts), n_chunks=1)
        so = _sb_sample(sqh, sk, sv, swap(cache_sb_k[l]), swap(cache_sb_v[l]), tk=SB_TILE,
                        window=min(SB_WINDOW, past_len), ns=min(bs, SAMPLE_STREAMS))
        xs, sc = _post(xs, ro, so, cmk_b.reshape(mem_shape), cmv_b.reshape(mem_shape),
                       state_ffn_conv[l], lw, nb=min(bs, SAMPLE_POST_STREAMS), tl=ts, n_groups=1, alpha=alpha)
        for lst, val in zip(outs, (swap(pk_t), swap(pv_t), ps, pc, mk, mv, sk, sv, ss, sc)):
            lst.append(val)
    return (xp, xs) + tuple(jnp.stack(o) for o in outs)
```

```python
import functools
import math

import jax
import jax.numpy as jnp
import numpy as np
from jax import lax
from jax.experimental import pallas as pl
from jax.experimental.pallas import tpu as pltpu

F32 = jnp.float32
BF16 = jnp.bfloat16

RET_HEADS = 4
RET_D = 128
SB_HEADS = 8
SB_DH = 64
MEM_HEADS = 4
CONV_W = 3
ROPE_BASE = 10000.0
LN_EPS = 1e-5
RMS_EPS = 1e-6
LOG2E = math.log2(math.e)
REF_CHUNK = 64

VMEM_LIMIT = 56 * 1024 * 1024

PROJ_ROWS = 1024
RET_CHUNK = 256
SB_TILE = 256
SB_WINDOW = 512
POST_ROWS = 512
POST_GROUPS = 2
SAMPLE_STREAMS = 4
SAMPLE_POST_STREAMS = 8


def _const_spec(shape):
    nd = len(shape)
    return pl.BlockSpec(shape, lambda *_: (0,) * nd, pipeline_mode=pl.Buffered(1))


def _ln(x, g, b):
    mu = jnp.mean(x, -1, keepdims=True)
    xc = x - mu
    var = jnp.mean(xc * xc, -1, keepdims=True)
    return xc * lax.rsqrt(var + LN_EPS) * g + b


def _silu(x):
    return x * (1.0 / (1.0 + jnp.exp(-x)))


def _dot(a, b):
    return jnp.dot(a, b, preferred_element_type=F32)


def _dot_nt(a, b):
    return lax.dot_general(a, b, (((1,), (1,)), ((), ())), preferred_element_type=F32)


def _ret_log_gamma(h):
    return math.log1p(-(2.0 ** (-5.0 - h)))


def _ret_init(s_scr, dmask_scr, s0, c):
    s_scr[...] = jnp.zeros_like(s_scr) if s0 is None else s0
    row = lax.broadcasted_iota(jnp.int32, (c, c), 0)
    col = lax.broadcasted_iota(jnp.int32, (c, c), 1)
    diff = (row - col).astype(F32)
    for h in range(RET_HEADS):
        dmask_scr[h] = jnp.where(diff >= 0, jnp.exp(_ret_log_gamma(h) * jnp.maximum(diff, 0.0)), 0.0)


def _ret_chunk(q, k, v, g, h, s_scr, dmask_scr):
    c = q.shape[0]
    log_g = _ret_log_gamma(h)
    idx = lax.broadcasted_iota(jnp.int32, (c, 1), 0).astype(F32)
    scores = _dot_nt(q, k) * dmask_scr[h]
    o = _dot(scores.astype(BF16), v)
    s_prev = s_scr[h]
    qd = (q.astype(F32) * jnp.exp(log_g * (idx + 1.0))).astype(BF16)
    o = o + _dot(qd, s_prev.astype(BF16))
    kd = (k.astype(F32) * jnp.exp(log_g * (c - 1.0 - idx))).astype(BF16)
    s_scr[h] = math.exp(log_g * c) * s_prev + _dot(kd.T, v)
    o = o * lax.rsqrt(jnp.mean(o * o, -1, keepdims=True) + RMS_EPS)
    return (o * _silu(g.astype(F32))).astype(BF16)


def _side_cast(weights, n_steps, step_of):
    in_specs, out_specs, out_shapes = [], [], []
    for w in weights:
        rows, cols = w.shape
        n_chunks = max(n for n in range(1, n_steps + 1) if (rows // 16) % n == 0)
        spec = pl.BlockSpec((rows // n_chunks, cols),
                            lambda *idx, last=n_chunks - 1: (jnp.minimum(step_of(*idx), last), 0))
        in_specs.append(spec)
        out_specs.append(spec)
        out_shapes.append(jax.ShapeDtypeStruct(w.shape, BF16))
    return in_specs, out_specs, out_shapes


def _proj_kernel(x_ref, w_ref, cos_ref, sin_ref, *refs, nb, tl, d_model, fused_chunk, n_cast):
    cast_in, refs = refs[:n_cast], refs[n_cast:]
    if fused_chunk is None:
        ret_ref, sqh_ref, sk_ref, sv_ref = refs[:4]
        cast_out = refs[4:4 + n_cast]
    else:
        ro_ref, sf_ref, sq_ref, skb_ref, svb_ref, sk_ref, sv_ref = refs[:7]
        cast_out = refs[7:7 + n_cast]
        s_scr, dmask_scr = refs[7 + n_cast:]
    for src, dst in zip(cast_in, cast_out):
        dst[...] = src[...].astype(BF16)
    m = nb * tl
    xb = x_ref[...].reshape(m, d_model).astype(BF16)
    cos = jnp.broadcast_to(cos_ref[...][None], (nb, tl, RET_D)).reshape(m, RET_D)
    sin = jnp.broadcast_to(sin_ref[...][None], (nb, tl, RET_D)).reshape(m, RET_D)
    hw = RET_HEADS * RET_D

    def seg(i):
        return _dot(xb, w_ref[:, i * hw:(i + 1) * hw])

    def rope(r, scale):
        outs = []
        for h in range(RET_HEADS):
            xh = r[:, h * RET_D:(h + 1) * RET_D]
            o = xh * cos + pltpu.roll(xh, RET_D // 2, axis=1) * sin
            outs.append(o * scale if scale != 1.0 else o)
        return jnp.concatenate(outs, axis=-1)

    if fused_chunk is not None:
        @pl.when(pl.program_id(1) == 0)
        def _():
            _ret_init(s_scr, dmask_scr, None, fused_chunk)

    ret_in = [rope(seg(0), 1.0).astype(BF16), rope(seg(1), RET_D ** -0.5).astype(BF16),
              seg(2).astype(BF16), seg(3).astype(BF16)]
    def put_heads(ref, val):
        for h in range(SB_HEADS):
            ref[:, h, :, :] = val[:, h * SB_DH:(h + 1) * SB_DH].astype(ref.dtype).reshape(nb, tl, SB_DH)

    if fused_chunk is None:
        for i, val in enumerate(ret_in):
            ret_ref[:, :, i * hw:(i + 1) * hw] = val.reshape(nb, tl, hw)
        put_heads(sqh_ref, seg(4) * (SB_DH ** -0.5))
        put_heads(sk_ref, seg(5))
        put_heads(sv_ref, seg(6))
        return

    def ret_unit(c0, h):
        blk = [val[c0:c0 + fused_chunk, h * RET_D:(h + 1) * RET_D] for val in ret_in]
        ro_ref[0, c0:c0 + fused_chunk, h * RET_D:(h + 1) * RET_D] = _ret_chunk(*blk, h, s_scr, dmask_scr)

    def sb_q():
        sq_ref[...] = (seg(4) * (SB_DH ** -0.5)).astype(BF16).reshape(nb, tl, hw)

    def sb_kv(i, b_ref, t_ref):
        r = seg(i)
        b_ref[...] = r.astype(BF16).reshape(nb, tl, hw)
        t_ref[0] = r.T.reshape(SB_HEADS, SB_DH, tl)

    ret_units = [functools.partial(ret_unit, c0, h) for c0 in range(0, tl, fused_chunk) for h in range(RET_HEADS)]
    mxu_units = [sb_q, functools.partial(sb_kv, 5, skb_ref, sk_ref), functools.partial(sb_kv, 6, svb_ref, sv_ref)]
    per = -(-len(ret_units) // len(mxu_units))
    for i, unit in enumerate(mxu_units):
        unit()
        for r in ret_units[i * per:(i + 1) * per]:
            r()

    @pl.when(pl.program_id(1) == pl.num_programs(1) - 1)
    def _():
        sf_ref[0] = s_scr[...]


def _proj(x, w_in_b, cos, sin, *, nb, tl, fused_chunk, cast=()):
    b, t, d = x.shape
    hw = RET_HEADS * RET_D
    grid = (b // nb, t // tl)
    cast_in, cast_out, cast_shapes = _side_cast(cast, grid[0] * grid[1], lambda i, j: i * grid[1] + j)
    row_spec = lambda w: pl.BlockSpec((nb, tl, w), lambda i, j: (i, j, 0))
    scratch = []
    if fused_chunk is None:
        hs = lambda dt: jax.ShapeDtypeStruct((b, SB_HEADS, t, SB_DH), dt)
        hs_spec = pl.BlockSpec((nb, SB_HEADS, tl, SB_DH), lambda i, j: (i, 0, j, 0))
        out_shape = [jax.ShapeDtypeStruct((b, t, 4 * hw), BF16), hs(BF16), hs(F32), hs(F32)]
        out_specs = [row_spec(4 * hw)] + [hs_spec] * 3
    else:
        assert nb == 1 and tl % fused_chunk == 0
        state = (RET_HEADS, RET_D, RET_D)
        out_shape = ([jax.ShapeDtypeStruct((b, t, hw), BF16), jax.ShapeDtypeStruct((b,) + state, F32)]
                     + [jax.ShapeDtypeStruct((b, t, hw), BF16)] * 3
                     + [jax.ShapeDtypeStruct((b, SB_HEADS, SB_DH, t), F32)] * 2)
        out_specs = ([row_spec(hw), pl.BlockSpec((1,) + state, lambda i, j: (i, 0, 0, 0))] + [row_spec(hw)] * 3
                     + [pl.BlockSpec((1, SB_HEADS, SB_DH, tl), lambda i, j: (i, 0, 0, j))] * 2)
        scratch = [pltpu.VMEM(state, F32), pltpu.VMEM((RET_HEADS, fused_chunk, fused_chunk), F32)]
    return pl.pallas_call(
        functools.partial(_proj_kernel, nb=nb, tl=tl, d_model=d, fused_chunk=fused_chunk, n_cast=len(cast)),
        grid=grid,
        in_specs=[row_spec(d), _const_spec(w_in_b.shape),
                  pl.BlockSpec((tl, RET_D), lambda i, j: (j, 0)),
                  pl.BlockSpec((tl, RET_D), lambda i, j: (j, 0))] + cast_in,
        out_specs=out_specs + cast_out, out_shape=out_shape + cast_shapes, scratch_shapes=scratch,
        compiler_params=pltpu.CompilerParams(
            dimension_semantics=("arbitrary", "arbitrary"), vmem_limit_bytes=VMEM_LIMIT),
        name="proj",
    )(x, w_in_b, cos, sin, *cast)


def _memkv_kernel(m_ref, wk_ref, wv_ref, *refs, dh, n_cast):
    cast_in = refs[:n_cast]
    mk_ref, mv_ref, mkb_ref, mvb_ref = refs[n_cast:n_cast + 4]
    cast_out = refs[n_cast + 4:]
    for src, dst in zip(cast_in, cast_out):
        dst[...] = src[...].astype(BF16)
    mb = m_ref[0].astype(BF16)
    for w_ref, o_ref, ob_ref in ((wk_ref, mk_ref, mkb_ref), (wv_ref, mv_ref, mvb_ref)):
        r = _dot(mb, w_ref[...].astype(BF16))
        for h in range(MEM_HEADS):
            rh = r[:, h * dh:(h + 1) * dh]
            o_ref[0, h] = rh
            ob_ref[0, h] = rh.astype(BF16)


def _memkv(mem, wk, wv, cast=()):
    b, n, d = mem.shape
    dh = d // MEM_HEADS
    spec = pl.BlockSpec((1, MEM_HEADS, n, dh), lambda i: (i, 0, 0, 0))
    cast_in, cast_out, cast_shapes = _side_cast(cast, b, lambda i: i)
    return pl.pallas_call(
        functools.partial(_memkv_kernel, dh=dh, n_cast=len(cast)),
        grid=(b,),
        in_specs=[pl.BlockSpec((1, n, d), lambda i: (i, 0, 0)), _const_spec(wk.shape), _const_spec(wv.shape)] + cast_in,
        out_specs=[spec] * 4 + cast_out,
        out_shape=[jax.ShapeDtypeStruct((b, MEM_HEADS, n, dh), F32)] * 2
        + [jax.ShapeDtypeStruct((b, MEM_HEADS, n, dh), BF16)] * 2 + cast_shapes,
        compiler_params=pltpu.CompilerParams(dimension_semantics=("arbitrary",), vmem_limit_bytes=VMEM_LIMIT),
        name="memkv",
    )(mem, wk, wv, *cast)


def _ret_kernel(ret_ref, s0_ref, ro_ref, sf_ref, s_scr, dmask_scr, *, c, n_chunks):
    t = pl.program_id(1)

    @pl.when(t == 0)
    def _():
        _ret_init(s_scr, dmask_scr, s0_ref[...], c)

    hw = RET_HEADS * RET_D
    for s in range(ret_ref.shape[0]):
        for ci in range(n_chunks):
            rows = slice(ci * c, (ci + 1) * c)
            for h in range(RET_HEADS):
                blk = [ret_ref[s, rows, i * hw + h * RET_D:i * hw + (h + 1) * RET_D] for i in range(4)]
                ro_ref[s, rows, h * RET_D:(h + 1) * RET_D] = _ret_chunk(*blk, h, s_scr.at[s], dmask_scr)

    @pl.when(t == pl.num_programs(1) - 1)
    def _():
        sf_ref[...] = s_scr[...]


def _retention(ret, s0, *, nb, c, n_chunks):
    b, t, w = ret.shape
    hw = RET_HEADS * RET_D
    tl = c * n_chunks
    state = (nb, RET_HEADS, RET_D, RET_D)
    st_spec = pl.BlockSpec(state, lambda i, j: (i, 0, 0, 0))
    return pl.pallas_call(
        functools.partial(_ret_kernel, c=c, n_chunks=n_chunks),
        grid=(b // nb, t // tl),
        in_specs=[pl.BlockSpec((nb, tl, w), lambda i, j: (i, j, 0)), st_spec],
        out_specs=[pl.BlockSpec((nb, tl, hw), lambda i, j: (i, j, 0)), st_spec],
        out_shape=[jax.ShapeDtypeStruct((b, t, hw), BF16),
                   jax.ShapeDtypeStruct((b, RET_HEADS, RET_D, RET_D), F32)],
        scratch_shapes=[pltpu.VMEM(state, F32), pltpu.VMEM((RET_HEADS, c, c), F32)],
        compiler_params=pltpu.CompilerParams(
            dimension_semantics=("parallel", "arbitrary"), vmem_limit_bytes=VMEM_LIMIT),
        name="retention",
    )(ret, s0)


SB_SKIP_ABOVE = 106.0


def _suffix_ones(n):
    j = lax.broadcasted_iota(jnp.int32, (n, n), 0)
    s = lax.broadcasted_iota(jnp.int32, (n, n), 1)
    return jnp.where(j > s, 1.0, 0.0).astype(BF16)


def _sb_weights(z, carry, valid, u):
    def masked(x):
        if valid is None:
            return x
        blocks = x.reshape((x.shape[0] // valid.shape[0],) + valid.shape)
        return jnp.where(valid[None], blocks, 0.0).reshape(x.shape)

    sp = masked(jnp.maximum(z, 0.0) + jnp.log(1.0 + jnp.exp2(jnp.abs(z) * -LOG2E)))
    later = _dot(sp.astype(BF16), u) + carry
    a = masked(jnp.exp((z - sp) - later))
    return a.astype(BF16), later[:, 0:1] + sp[:, 0:1]


def _sb_live(carry):
    return (jnp.min(carry) < SB_SKIP_ABOVE).astype(jnp.int32)


def _sb_prompt_kernel(q_ref, k_ref, v_ref, *refs, tq, n_cast):
    o_ref = refs[n_cast]
    for src, dst in zip(refs[:n_cast], refs[n_cast + 1:]):
        dst[...] = src[...].astype(BF16)
    qi = pl.program_id(1)
    lanes = 2 * SB_DH
    n_pairs = q_ref.shape[-1] // lanes
    pair = lambda p: slice(p * lanes, (p + 1) * lanes)
    lo_half = lax.broadcasted_iota(jnp.int32, (1, lanes), 1) < SB_DH
    q2 = []
    for p in range(n_pairs):
        q = q_ref[0, :, pair(p)]
        zero = jnp.zeros_like(q)
        q2.append(jnp.concatenate([jnp.where(lo_half, q, zero), jnp.where(lo_half, zero, q)], axis=0))
    m = 2 * n_pairs * tq
    row = lax.broadcasted_iota(jnp.int32, (tq, tq), 0)
    col = lax.broadcasted_iota(jnp.int32, (tq, tq), 1)
    u = _suffix_ones(tq)

    def tile(kt, acc, carry, valid):
        ks = pl.multiple_of(kt * tq, tq)
        z = jnp.concatenate([_dot_nt(q2[p], k_ref[0, pl.ds(ks, tq), pair(p)]) for p in range(n_pairs)], axis=0)
        a, carry = _sb_weights(z, carry, valid, u)
        pv = jnp.concatenate([_dot(a[p * 2 * tq:(p + 1) * 2 * tq], v_ref[0, pl.ds(ks, tq), pair(p)])
                              for p in range(n_pairs)], axis=0)
        return acc + pv, carry

    acc = jnp.zeros((m, lanes), F32)
    carry = jnp.zeros((m, 1), F32)
    acc, carry = tile(qi, acc, carry, col < row)
    no_tile = jnp.where(qi == 0, 1e30, 0.0).astype(F32)
    acc, carry = tile(jnp.maximum(qi - 1, 0), acc, carry + no_tile, None)

    def cond(c):
        return jnp.logical_and(c[0] < qi, c[3] > 0)

    def body(c):
        acc, carry = tile(qi - 1 - c[0], c[1], c[2], None)
        return c[0] + 1, acc, carry, _sb_live(carry)

    _, acc, _, _ = lax.while_loop(cond, body, (jnp.int32(1), acc, carry, _sb_live(carry)))
    o_ref[0] = jnp.concatenate(
        [jnp.where(lo_half, acc[2 * p * tq:(2 * p + 1) * tq], acc[(2 * p + 1) * tq:(2 * p + 2) * tq])
         for p in range(n_pairs)], axis=-1).astype(BF16)


def _sb_prompt(sq, sk, sv, *, tq, cast=()):
    b, t, w = sq.shape
    grid = (b, t // tq)
    q_spec = pl.BlockSpec((1, tq, w), lambda i, j: (i, j, 0))
    kv_spec = pl.BlockSpec((1, t, w), lambda i, j: (i, 0, 0))
    cast_in, cast_out, cast_shapes = _side_cast(cast, grid[0] * grid[1], lambda i, j: i * grid[1] + j)
    return pl.pallas_call(
        functools.partial(_sb_prompt_kernel, tq=tq, n_cast=len(cast)),
        grid=grid,
        in_specs=[q_spec, kv_spec, kv_spec] + cast_in,
        out_specs=[q_spec] + cast_out,
        out_shape=[jax.ShapeDtypeStruct((b, t, w), BF16)] + cast_shapes,
        compiler_params=pltpu.CompilerParams(
            dimension_semantics=("arbitrary", "arbitrary"), vmem_limit_bytes=VMEM_LIMIT),
        name="sb_prompt",
    )(sq, sk, sv, *cast)


def _sb_sample_kernel(q_ref, kn_ref, vn_ref, kct_ref, vct_ref, o_ref, carry_ref, acc_scr, carry_scr, *, tl, tk):
    ns, nh = q_ref.shape[:2]
    n_tiles = kct_ref.shape[3] // tk
    m = ns * nh * tl
    units = [(s, h) for s in range(ns) for h in range(nh)]

    def per_unit(fn):
        return jnp.concatenate([fn(i, s, h) for i, (s, h) in enumerate(units)], axis=0)

    row = lax.broadcasted_iota(jnp.int32, (tl, tl), 0)
    col = lax.broadcasted_iota(jnp.int32, (tl, tl), 1)
    z = per_unit(lambda i, s, h: _dot_nt(q_ref[s, h], kn_ref[s, h].astype(BF16)))
    a, carry = _sb_weights(z, jnp.zeros((m, 1), F32), col < row, _suffix_ones(tl))
    acc_scr[...] = per_unit(lambda i, s, h: _dot(a[i * tl:(i + 1) * tl], vn_ref[s, h].astype(BF16)))
    carry_scr[...] = carry
    u = _suffix_ones(tk)
    for j in reversed(range(n_tiles)):
        @pl.when(jnp.min(carry_scr[...]) < SB_SKIP_ABOVE)
        def _(j=j):
            cols = slice(j * tk, (j + 1) * tk)
            z = per_unit(lambda i, s, h: _dot(q_ref[s, h], kct_ref[s, h, :, cols].astype(BF16)))
            a, carry = _sb_weights(z, carry_scr[...], None, u)
            acc_scr[...] += per_unit(
                lambda i, s, h: _dot_nt(a[i * tl:(i + 1) * tl], vct_ref[s, h, :, cols].astype(BF16)))
            carry_scr[...] = carry

    acc = acc_scr[...]
    for s in range(ns):
        o_ref[s] = jnp.concatenate([acc[(s * nh + h) * tl:(s * nh + h + 1) * tl] for h in range(nh)],
                                   axis=-1).astype(BF16)
    carry_ref[...] = jnp.full(carry_ref.shape, jnp.min(carry_scr[...]), F32)


def _sb_sample_window(sqh, sk_new, sv_new, k_cache_t, v_cache_t, *, tk, window, ns):
    b, nh, tl, dh = sqh.shape
    past = k_cache_t.shape[3]
    assert past % window == 0 and window % tk == 0 and b % ns == 0
    new_spec = pl.BlockSpec((ns, nh, tl, dh), lambda i: (i, 0, 0, 0))
    cache_spec = pl.BlockSpec((ns, nh, dh, window), lambda i: (i, 0, 0, past // window - 1))
    flag_spec = pl.BlockSpec((1, 8, 128), lambda i: (i, 0, 0))
    return pl.pallas_call(
        functools.partial(_sb_sample_kernel, tl=tl, tk=tk),
        grid=(b // ns,),
        in_specs=[new_spec, new_spec, new_spec, cache_spec, cache_spec],
        out_specs=[pl.BlockSpec((ns, tl, nh * dh), lambda i: (i, 0, 0)), flag_spec],
        out_shape=[jax.ShapeDtypeStruct((b, tl, nh * dh), BF16), jax.ShapeDtypeStruct((b // ns, 8, 128), F32)],
        scratch_shapes=[pltpu.VMEM((ns * nh * tl, dh), F32), pltpu.VMEM((ns * nh * tl, 1), F32)],
        compiler_params=pltpu.CompilerParams(dimension_semantics=("parallel",), vmem_limit_bytes=VMEM_LIMIT),
        name="sb_sample",
    )(sqh, sk_new, sv_new, k_cache_t, v_cache_t)


def _sb_sample(sqh, sk_new, sv_new, k_cache_t, v_cache_t, *, tk, window, ns):
    so, carry = _sb_sample_window(sqh, sk_new, sv_new, k_cache_t, v_cache_t, tk=tk, window=window, ns=ns)
    past = k_cache_t.shape[3]
    if window == past:
        return so
    return lax.cond(jnp.min(carry) < SB_SKIP_ABOVE,
                    lambda: _sb_sample_window(sqh, sk_new, sv_new, k_cache_t, v_cache_t, tk=tk, window=past, ns=1)[0],
                    lambda: so)


def _post_kernel(x_ref, ro_ref, so_ref, mk_ref, mv_ref, cpast_ref,
                 wo_ref, ln1g_ref, ln1b_ref, wq_ref, wom_ref, ln2g_ref, ln2b_ref,
                 wup_ref, cw_ref, cb_ref, wdn_ref, ln3g_ref, ln3b_ref,
                 y_ref, cst_ref, prev_scr, act_scr, *, nb, tl, alpha, fb, n_groups):
    t = pl.program_id(1)
    m = nb * tl
    sub = CONV_W - 1

    @pl.when(t == 0)
    def _():
        prev_scr[...] = jnp.zeros_like(prev_scr)
        prev_scr[:, 8 - sub:8, :] = cpast_ref[...]

    d = x_ref.shape[-1]
    hw = ro_ref.shape[-1]
    dh = mk_ref.shape[3]
    d_ff = wdn_ref.shape[0]
    if nb == 1:
        r = tl // n_groups
        groups = [[(0, i * r, r)] for i in range(n_groups)]
    else:
        per = nb // n_groups
        groups = [[(b, b * tl, tl) for b in range(i * per, (i + 1) * per)] for i in range(n_groups)]
    x_all = x_ref[...].reshape(m, d)
    ro_all = ro_ref[...].reshape(m, hw)
    so_all = so_ref[...].reshape(m, hw)

    def mixer_and_memory(segs):
        g0, g1 = segs[0][1], segs[-1][1] + segs[-1][2]
        mix_in = jnp.concatenate([ro_all[g0:g1], so_all[g0:g1]], axis=-1)
        x1 = _ln(alpha * x_all[g0:g1] + _dot(mix_in, wo_ref[...]), ln1g_ref[...], ln1b_ref[...])
        q = _dot(x1.astype(BF16), wq_ref[...])
        qb = q.astype(BF16)
        s = jnp.concatenate([_dot_nt(qb[s0 - g0:s0 - g0 + n, h * dh:(h + 1) * dh], mk_ref[b, h])
                             for b, s0, n in segs for h in range(MEM_HEADS)], axis=0) * (dh ** -0.5)
        e = jnp.exp(s - jnp.max(s, -1, keepdims=True))
        p = (e * (1.0 / jnp.sum(e, -1, keepdims=True))).astype(BF16)
        outs, r0 = [], 0
        for b, s0, n in segs:
            heads = []
            for h in range(MEM_HEADS):
                heads.append(_dot(p[r0:r0 + n], mv_ref[b, h]).astype(BF16))
                r0 += n
            outs.append(jnp.concatenate(heads, axis=-1))
        att_in = outs[0] if len(outs) == 1 else jnp.concatenate(outs, axis=0)
        return _ln(alpha * x1 + _dot(att_in, wom_ref[...]), ln2g_ref[...], ln2b_ref[...])

    x2 = [mixer_and_memory(segs) for segs in groups]
    x2b = [v.astype(BF16) for v in x2]

    sub_idx = lax.broadcasted_iota(jnp.int32, (1, 8, fb), 1)

    def conv(u, prev8, c0):
        g = u.shape[0] // 8
        u3 = u.reshape(g, 8, fb)
        ext = jnp.concatenate([prev8[None], u3], axis=0)
        r1 = pltpu.roll(ext, 1, axis=1)
        r2 = pltpu.roll(r1, 1, axis=1)
        u1 = jnp.where(sub_idx < 1, r1[:-1], r1[1:])
        u2 = jnp.where(sub_idx < 2, r2[:-1], r2[1:])
        cols = slice(c0, c0 + fb)
        c = (cb_ref[:, cols][None] + cw_ref[0:1, cols][None] * u2
             + cw_ref[1:2, cols][None] * u1 + cw_ref[2:3, cols][None] * u3)
        return c.reshape(u.shape), u3[g - 1]

    def conv_group(u, segs, g0, c0, chain):
        parts = []
        for b, s0, n in segs:
            prev8 = chain.get(b)
            if prev8 is None:
                prev8 = prev_scr[b, :, c0:c0 + fb]
            c, chain[b] = conv(u[s0 - g0:s0 - g0 + n], prev8, c0)
            parts.append(c)
        return parts[0] if len(parts) == 1 else jnp.concatenate(parts, axis=0)

    for blk in range(d_ff // fb):
        ca, cg = blk * fb, d_ff + blk * fb
        chain_a, chain_g = {}, {}
        for segs, xg in zip(groups, x2b):
            g0, g1 = segs[0][1], segs[-1][1] + segs[-1][2]
            a = conv_group(_dot(xg, wup_ref[:, ca:ca + fb]), segs, g0, ca, chain_a)
            g = conv_group(_dot(xg, wup_ref[:, cg:cg + fb]), segs, g0, cg, chain_g)
            act_scr[g0:g1, ca:ca + fb] = (_silu(a) * g).astype(BF16)
        for b in range(nb):
            prev_scr[b, :, ca:ca + fb] = chain_a[b]
            prev_scr[b, :, cg:cg + fb] = chain_g[b]

    for segs, x2g in zip(groups, x2):
        g0, g1 = segs[0][1], segs[-1][1] + segs[-1][2]
        f = _dot(act_scr[g0:g1, :], wdn_ref[...])
        y = _ln(alpha * x2g + f, ln3g_ref[...], ln3b_ref[...])
        for b, s0, n in segs:
            y_ref[b, s0 - b * tl:s0 - b * tl + n, :] = y[s0 - g0:s0 - g0 + n]
    cst_ref[...] = prev_scr[:, 8 - sub:8, :]


def _post(x, ro, so, mk_b, mv_b, conv_past, lw, *, nb, tl, n_groups, alpha):
    b, t, d = x.shape
    assert nb == 1 or tl == t
    hw = ro.shape[-1]
    two_ff = lw["w_up"].shape[1]
    row = lambda w: pl.BlockSpec((nb, tl, w), lambda i, j: (i, j, 0))
    mem_spec = pl.BlockSpec((nb,) + mk_b.shape[1:], lambda i, j: (i, 0, 0, 0),
                            pipeline_mode=pl.Buffered(1) if nb == b else None)
    cst_spec = pl.BlockSpec((nb, CONV_W - 1, two_ff), lambda i, j: (i, 0, 0))
    names = ("w_o", "ln1_g", "ln1_b", "w_q_mem", "w_o_mem", "ln2_g", "ln2_b",
             "w_up", "conv_w", "conv_b", "w_down", "ln3_g", "ln3_b")
    weights = [lw[n] for n in names]
    assert (tl if nb == 1 else nb) % n_groups == 0 and tl % 8 == 0
    return pl.pallas_call(
        functools.partial(_post_kernel, nb=nb, tl=tl, alpha=alpha, fb=256, n_groups=n_groups),
        grid=(b // nb, t // tl),
        in_specs=[row(d), row(hw), row(hw), mem_spec, mem_spec, cst_spec] + [_const_spec(w.shape) for w in weights],
        out_specs=[row(d), cst_spec],
        out_shape=[jax.ShapeDtypeStruct((b, t, d), F32), jax.ShapeDtypeStruct((b, CONV_W - 1, two_ff), F32)],
        scratch_shapes=[pltpu.VMEM((nb, 8, two_ff), F32), pltpu.VMEM((nb * tl, two_ff // 2), BF16)],
        compiler_params=pltpu.CompilerParams(
            dimension_semantics=("parallel", "arbitrary"), vmem_limit_bytes=VMEM_LIMIT),
        name="post",
    )(x, ro, so, mk_b, mv_b, conv_past, *weights)


def _rope_tables(first, n):
    half = RET_D // 2
    inv = 1.0 / (ROPE_BASE ** (np.arange(half, dtype=np.float64) / half))
    ang = np.arange(first, first + n, dtype=np.float64)[:, None] * inv[None, :]
    c, s = np.cos(ang).astype(np.float32), np.sin(ang).astype(np.float32)
    return jnp.asarray(np.concatenate([c, c], axis=-1)), jnp.asarray(np.concatenate([-s, s], axis=-1))


def kernel(x_prompt, x_sample, cache_sb_k, cache_sb_v, state_ret, state_ffn_conv, cache_mem_k, cache_mem_v, mem_prompt, w_in, w_o, ln1_g, ln1_b, w_q_mem, w_k_mem, w_v_mem, w_o_mem, ln2_g, ln2_b, w_up, conv_w, conv_b, w_down, ln3_g, ln3_b):
    depth = w_in.shape[0]
    alpha = (2.0 * depth) ** 0.25
    xp, xs = x_prompt, x_sample
    bp, tp, _ = xp.shape
    bs, ts, _ = xs.shape
    past_len = cache_sb_k.shape[3]
    two_ff = w_up.shape[2]
    cos_p, sin_p = _rope_tables(0, tp)
    cos_s, sin_s = _rope_tables(past_len, ts)
    row2 = lambda a: a.reshape(1, -1)
    swap = lambda a: jnp.swapaxes(a, -1, -2)
    outs = [[] for _ in range(10)]
    for l in range(depth):
        lw = {"ln1_g": row2(ln1_g[l]), "ln1_b": row2(ln1_b[l]), "ln2_g": row2(ln2_g[l]), "ln2_b": row2(ln2_b[l]),
              "conv_w": conv_w[l], "conv_b": row2(conv_b[l]), "ln3_g": row2(ln3_g[l]), "ln3_b": row2(ln3_b[l])}

        mk, mv, mk_b, mv_b, w_in_b = _memkv(mem_prompt, w_k_mem[l], w_v_mem[l], cast=(w_in[l],))
        late = ("w_o", "w_q_mem", "w_o_mem", "w_up", "w_down")
        ro, ps, sq, skb, svb, pk_t, pv_t, *late_b = _proj(
            xp, w_in_b, cos_p, sin_p, nb=1, tl=min(tp, PROJ_ROWS), fused_chunk=RET_CHUNK,
            cast=(w_o[l], w_q_mem[l], w_o_mem[l], w_up[l], w_down[l]))
        lw.update(zip(late, late_b))
        mem_shape = cache_mem_k.shape[1:]
        flat = lambda a: a.reshape(-1, mem_shape[-1])
        so, cmk_b, cmv_b = _sb_prompt(sq, skb, svb, tq=SB_TILE, cast=(flat(cache_mem_k[l]), flat(cache_mem_v[l])))
        xp, pc = _post(xp, ro, so, mk_b, mv_b, jnp.zeros((bp, CONV_W - 1, two_ff), F32), lw,
                       nb=1, tl=min(tp, POST_ROWS), n_groups=POST_GROUPS, alpha=alpha)

        ret, sqh, sk, sv = _proj(xs, w_in_b, cos_s, sin_s, nb=bs, tl=ts, fused_chunk=None)
        ro, ss = _retention(ret, state_ret[l], nb=min(bs, SAMPLE_STREAMS), c=min(REF_CHUNK, ts), n_chunks=1)
        so = _sb_sample(sqh, sk, sv, swap(cache_sb_k[l]), swap(cache_sb_v[l]), tk=SB_TILE,
                        window=min(SB_WINDOW, past_len), ns=min(bs, SAMPLE_STREAMS))
        xs, sc = _post(xs, ro, so, cmk_b.reshape(mem_shape), cmv_b.reshape(mem_shape),
                       state_ffn_conv[l], lw, nb=min(bs, SAMPLE_POST_STREAMS), tl=ts, n_groups=1, alpha=alpha)
        for lst, val in zip(outs, (swap(pk_t), swap(pv_t), ps, pc, mk, mv, sk, sv, ss, sc)):
            lst.append(val)
    return (xp, xs) + tuple(jnp.stack(o) for o in outs)
```

```python
import functools
import math

import jax
import jax.numpy as jnp
import numpy as np
from jax import lax
from jax.experimental import pallas as pl
from jax.experimental.pallas import tpu as pltpu

F32 = jnp.float32
BF16 = jnp.bfloat16

RET_HEADS = 4
RET_D = 128
SB_HEADS = 8
SB_DH = 64
MEM_HEADS = 4
CONV_W = 3
ROPE_BASE = 10000.0
LN_EPS = 1e-5
RMS_EPS = 1e-6
LOG2E = math.log2(math.e)
REF_CHUNK = 64

VMEM_LIMIT = 56 * 1024 * 1024

PROJ_ROWS = 1024
RET_CHUNK = 256
SB_TILE = 256
SB_PAIR_GROUPS = 2
SB_WINDOW = 512
POST_ROWS = 512
POST_GROUPS = 2
SAMPLE_STREAMS = 4
SAMPLE_POST_STREAMS = 8


def _const_spec(shape):
    nd = len(shape)
    return pl.BlockSpec(shape, lambda *_: (0,) * nd, pipeline_mode=pl.Buffered(1))


def _ln(x, g, b):
    mu = jnp.mean(x, -1, keepdims=True)
    xc = x - mu
    var = jnp.mean(xc * xc, -1, keepdims=True)
    return xc * lax.rsqrt(var + LN_EPS) * g + b


def _silu(x):
    return x * (1.0 / (1.0 + jnp.exp(-x)))


def _dot(a, b):
    return jnp.dot(a, b, preferred_element_type=F32)


def _dot_nt(a, b):
    return lax.dot_general(a, b, (((1,), (1,)), ((), ())), preferred_element_type=F32)


def _ret_log_gamma(h):
    return math.log1p(-(2.0 ** (-5.0 - h)))


def _ret_init(s_scr, dmask_scr, s0, c):
    s_scr[...] = jnp.zeros_like(s_scr) if s0 is None else s0
    row = lax.broadcasted_iota(jnp.int32, (c, c), 0)
    col = lax.broadcasted_iota(jnp.int32, (c, c), 1)
    diff = (row - col).astype(F32)
    for h in range(RET_HEADS):
        dmask_scr[h] = jnp.where(diff >= 0, jnp.exp(_ret_log_gamma(h) * jnp.maximum(diff, 0.0)), 0.0)


def _ret_chunk(q, k, v, g, h, s_scr, dmask_scr):
    c = q.shape[0]
    log_g = _ret_log_gamma(h)
    idx = lax.broadcasted_iota(jnp.int32, (c, 1), 0).astype(F32)
    scores = _dot_nt(q, k) * dmask_scr[h]
    o = _dot(scores.astype(BF16), v)
    s_prev = s_scr[h]
    qd = (q.astype(F32) * jnp.exp(log_g * (idx + 1.0))).astype(BF16)
    o = o + _dot(qd, s_prev.astype(BF16))
    kd = (k.astype(F32) * jnp.exp(log_g * (c - 1.0 - idx))).astype(BF16)
    s_scr[h] = math.exp(log_g * c) * s_prev + _dot(kd.T, v)
    o = o * lax.rsqrt(jnp.mean(o * o, -1, keepdims=True) + RMS_EPS)
    return (o * _silu(g.astype(F32))).astype(BF16)


def _side_cast(weights, n_steps, step_of):
    in_specs, out_specs, out_shapes = [], [], []
    for w in weights:
        rows, cols = w.shape
        n_chunks = max(n for n in range(1, n_steps + 1) if (rows // 16) % n == 0)
        spec = pl.BlockSpec((rows // n_chunks, cols),
                            lambda *idx, last=n_chunks - 1: (jnp.minimum(step_of(*idx), last), 0))
        in_specs.append(spec)
        out_specs.append(spec)
        out_shapes.append(jax.ShapeDtypeStruct(w.shape, BF16))
    return in_specs, out_specs, out_shapes


def _proj_kernel(x_ref, w_ref, cos_ref, sin_ref, *refs, nb, tl, d_model, fused_chunk, n_cast):
    cast_in, refs = refs[:n_cast], refs[n_cast:]
    if fused_chunk is None:
        ret_ref, sqh_ref, sk_ref, sv_ref = refs[:4]
        cast_out = refs[4:4 + n_cast]
    else:
        ro_ref, sf_ref, sq_ref, skb_ref, svb_ref, sk_ref, sv_ref = refs[:7]
        cast_out = refs[7:7 + n_cast]
        s_scr, dmask_scr = refs[7 + n_cast:]
    for src, dst in zip(cast_in, cast_out):
        dst[...] = src[...].astype(BF16)
    m = nb * tl
    xb = x_ref[...].reshape(m, d_model).astype(BF16)
    cos = jnp.broadcast_to(cos_ref[...][None], (nb, tl, RET_D)).reshape(m, RET_D)
    sin = jnp.broadcast_to(sin_ref[...][None], (nb, tl, RET_D)).reshape(m, RET_D)
    hw = RET_HEADS * RET_D

    def seg(i):
        return _dot(xb, w_ref[:, i * hw:(i + 1) * hw])

    def rope(r, scale):
        outs = []
        for h in range(RET_HEADS):
            xh = r[:, h * RET_D:(h + 1) * RET_D]
            o = xh * cos + pltpu.roll(xh, RET_D // 2, axis=1) * sin
            outs.append(o * scale if scale != 1.0 else o)
        return jnp.concatenate(outs, axis=-1)

    if fused_chunk is not None:
        @pl.when(pl.program_id(1) == 0)
        def _():
            _ret_init(s_scr, dmask_scr, None, fused_chunk)

    ret_in = [rope(seg(0), 1.0).astype(BF16), rope(seg(1), RET_D ** -0.5).astype(BF16),
              seg(2).astype(BF16), seg(3).astype(BF16)]
    def put_heads(ref, val):
        for h in range(SB_HEADS):
            ref[:, h, :, :] = val[:, h * SB_DH:(h + 1) * SB_DH].astype(ref.dtype).reshape(nb, tl, SB_DH)

    if fused_chunk is None:
        for i, val in enumerate(ret_in):
            ret_ref[:, :, i * hw:(i + 1) * hw] = val.reshape(nb, tl, hw)
        put_heads(sqh_ref, seg(4) * (SB_DH ** -0.5))
        put_heads(sk_ref, seg(5))
        put_heads(sv_ref, seg(6))
        return

    def ret_unit(c0, h):
        blk = [val[c0:c0 + fused_chunk, h * RET_D:(h + 1) * RET_D] for val in ret_in]
        ro_ref[0, c0:c0 + fused_chunk, h * RET_D:(h + 1) * RET_D] = _ret_chunk(*blk, h, s_scr, dmask_scr)

    def sb_q():
        sq_ref[...] = (seg(4) * (SB_DH ** -0.5)).astype(BF16).reshape(nb, tl, hw)

    def sb_kv(i, b_ref, t_ref):
        r = seg(i)
        b_ref[...] = r.astype(BF16).reshape(nb, tl, hw)
        t_ref[0] = r.T.reshape(SB_HEADS, SB_DH, tl)

    ret_units = [functools.partial(ret_unit, c0, h) for c0 in range(0, tl, fused_chunk) for h in range(RET_HEADS)]
    mxu_units = [sb_q, functools.partial(sb_kv, 5, skb_ref, sk_ref), functools.partial(sb_kv, 6, svb_ref, sv_ref)]
    per = -(-len(ret_units) // len(mxu_units))
    for i, unit in enumerate(mxu_units):
        unit()
        for r in ret_units[i * per:(i + 1) * per]:
            r()

    @pl.when(pl.program_id(1) == pl.num_programs(1) - 1)
    def _():
        sf_ref[0] = s_scr[...]


def _proj(x, w_in_b, cos, sin, *, nb, tl, fused_chunk, cast=()):
    b, t, d = x.shape
    hw = RET_HEADS * RET_D
    grid = (b // nb, t // tl)
    cast_in, cast_out, cast_shapes = _side_cast(cast, grid[0] * grid[1], lambda i, j: i * grid[1] + j)
    row_spec = lambda w: pl.BlockSpec((nb, tl, w), lambda i, j: (i, j, 0))
    scratch = []
    if fused_chunk is None:
        hs = lambda dt: jax.ShapeDtypeStruct((b, SB_HEADS, t, SB_DH), dt)
        hs_spec = pl.BlockSpec((nb, SB_HEADS, tl, SB_DH), lambda i, j: (i, 0, j, 0))
        out_shape = [jax.ShapeDtypeStruct((b, t, 4 * hw), BF16), hs(BF16), hs(F32), hs(F32)]
        out_specs = [row_spec(4 * hw)] + [hs_spec] * 3
    else:
        assert nb == 1 and tl % fused_chunk == 0
        state = (RET_HEADS, RET_D, RET_D)
        out_shape = ([jax.ShapeDtypeStruct((b, t, hw), BF16), jax.ShapeDtypeStruct((b,) + state, F32)]
                     + [jax.ShapeDtypeStruct((b, t, hw), BF16)] * 3
                     + [jax.ShapeDtypeStruct((b, SB_HEADS, SB_DH, t), F32)] * 2)
        out_specs = ([row_spec(hw), pl.BlockSpec((1,) + state, lambda i, j: (i, 0, 0, 0))] + [row_spec(hw)] * 3
                     + [pl.BlockSpec((1, SB_HEADS, SB_DH, tl), lambda i, j: (i, 0, 0, j))] * 2)
        scratch = [pltpu.VMEM(state, F32), pltpu.VMEM((RET_HEADS, fused_chunk, fused_chunk), F32)]
    return pl.pallas_call(
        functools.partial(_proj_kernel, nb=nb, tl=tl, d_model=d, fused_chunk=fused_chunk, n_cast=len(cast)),
        grid=grid,
        in_specs=[row_spec(d), _const_spec(w_in_b.shape),
                  pl.BlockSpec((tl, RET_D), lambda i, j: (j, 0)),
                  pl.BlockSpec((tl, RET_D), lambda i, j: (j, 0))] + cast_in,
        out_specs=out_specs + cast_out, out_shape=out_shape + cast_shapes, scratch_shapes=scratch,
        compiler_params=pltpu.CompilerParams(
            dimension_semantics=("arbitrary", "arbitrary"), vmem_limit_bytes=VMEM_LIMIT),
        name="proj",
    )(x, w_in_b, cos, sin, *cast)


def _memkv_kernel(m_ref, wk_ref, wv_ref, *refs, dh, n_cast):
    cast_in = refs[:n_cast]
    mk_ref, mv_ref, mkb_ref, mvb_ref = refs[n_cast:n_cast + 4]
    cast_out = refs[n_cast + 4:]
    for src, dst in zip(cast_in, cast_out):
        dst[...] = src[...].astype(BF16)
    mb = m_ref[0].astype(BF16)
    for w_ref, o_ref, ob_ref in ((wk_ref, mk_ref, mkb_ref), (wv_ref, mv_ref, mvb_ref)):
        r = _dot(mb, w_ref[...].astype(BF16))
        for h in range(MEM_HEADS):
            rh = r[:, h * dh:(h + 1) * dh]
            o_ref[0, h] = rh
            ob_ref[0, h] = rh.astype(BF16)


def _memkv(mem, wk, wv, cast=()):
    b, n, d = mem.shape
    dh = d // MEM_HEADS
    spec = pl.BlockSpec((1, MEM_HEADS, n, dh), lambda i: (i, 0, 0, 0))
    cast_in, cast_out, cast_shapes = _side_cast(cast, b, lambda i: i)
    return pl.pallas_call(
        functools.partial(_memkv_kernel, dh=dh, n_cast=len(cast)),
        grid=(b,),
        in_specs=[pl.BlockSpec((1, n, d), lambda i: (i, 0, 0)), _const_spec(wk.shape), _const_spec(wv.shape)] + cast_in,
        out_specs=[spec] * 4 + cast_out,
        out_shape=[jax.ShapeDtypeStruct((b, MEM_HEADS, n, dh), F32)] * 2
        + [jax.ShapeDtypeStruct((b, MEM_HEADS, n, dh), BF16)] * 2 + cast_shapes,
        compiler_params=pltpu.CompilerParams(dimension_semantics=("arbitrary",), vmem_limit_bytes=VMEM_LIMIT),
        name="memkv",
    )(mem, wk, wv, *cast)


def _ret_kernel(ret_ref, s0_ref, ro_ref, sf_ref, s_scr, dmask_scr, *, c, n_chunks):
    t = pl.program_id(1)

    @pl.when(t == 0)
    def _():
        _ret_init(s_scr, dmask_scr, s0_ref[...], c)

    hw = RET_HEADS * RET_D
    for s in range(ret_ref.shape[0]):
        for ci in range(n_chunks):
            rows = slice(ci * c, (ci + 1) * c)
            for h in range(RET_HEADS):
                blk = [ret_ref[s, rows, i * hw + h * RET_D:i * hw + (h + 1) * RET_D] for i in range(4)]
                ro_ref[s, rows, h * RET_D:(h + 1) * RET_D] = _ret_chunk(*blk, h, s_scr.at[s], dmask_scr)

    @pl.when(t == pl.num_programs(1) - 1)
    def _():
        sf_ref[...] = s_scr[...]


def _retention(ret, s0, *, nb, c, n_chunks):
    b, t, w = ret.shape
    hw = RET_HEADS * RET_D
    tl = c * n_chunks
    state = (nb, RET_HEADS, RET_D, RET_D)
    st_spec = pl.BlockSpec(state, lambda i, j: (i, 0, 0, 0))
    return pl.pallas_call(
        functools.partial(_ret_kernel, c=c, n_chunks=n_chunks),
        grid=(b // nb, t // tl),
        in_specs=[pl.BlockSpec((nb, tl, w), lambda i, j: (i, j, 0)), st_spec],
        out_specs=[pl.BlockSpec((nb, tl, hw), lambda i, j: (i, j, 0)), st_spec],
        out_shape=[jax.ShapeDtypeStruct((b, t, hw), BF16),
                   jax.ShapeDtypeStruct((b, RET_HEADS, RET_D, RET_D), F32)],
        scratch_shapes=[pltpu.VMEM(state, F32), pltpu.VMEM((RET_HEADS, c, c), F32)],
        compiler_params=pltpu.CompilerParams(
            dimension_semantics=("parallel", "arbitrary"), vmem_limit_bytes=VMEM_LIMIT),
        name="retention",
    )(ret, s0)


SB_SKIP_ABOVE = 106.0


def _suffix_ones(n):
    j = lax.broadcasted_iota(jnp.int32, (n, n), 0)
    s = lax.broadcasted_iota(jnp.int32, (n, n), 1)
    return jnp.where(j > s, 1.0, 0.0).astype(BF16)


def _sb_weights(z, carry, valid, u):
    def masked(x):
        if valid is None:
            return x
        blocks = x.reshape((x.shape[0] // valid.shape[0],) + valid.shape)
        return jnp.where(valid[None], blocks, 0.0).reshape(x.shape)

    sp = masked(jnp.maximum(z, 0.0) + jnp.log(1.0 + jnp.exp2(jnp.abs(z) * -LOG2E)))
    later = _dot(sp.astype(BF16), u) + carry
    a = masked(jnp.exp((z - sp) - later))
    return a.astype(BF16), later[:, 0:1] + sp[:, 0:1]


def _sb_live(carry):
    return (jnp.min(carry) < SB_SKIP_ABOVE).astype(jnp.int32)


def _sb_prompt_kernel(q_ref, k_ref, v_ref, *refs, tq, n_cast):
    o_ref = refs[n_cast]
    for src, dst in zip(refs[:n_cast], refs[n_cast + 1:]):
        dst[...] = src[...].astype(BF16)
    qi = pl.program_id(1)
    lanes = 2 * SB_DH
    n_pairs = q_ref.shape[-1] // lanes
    pair = lambda p: slice(p * lanes, (p + 1) * lanes)
    lo_half = lax.broadcasted_iota(jnp.int32, (1, lanes), 1) < SB_DH
    q2 = []
    for p in range(n_pairs):
        q = q_ref[0, :, pair(p)]
        zero = jnp.zeros_like(q)
        q2.append(jnp.concatenate([jnp.where(lo_half, q, zero), jnp.where(lo_half, zero, q)], axis=0))
    m = 2 * n_pairs * tq
    row = lax.broadcasted_iota(jnp.int32, (tq, tq), 0)
    col = lax.broadcasted_iota(jnp.int32, (tq, tq), 1)
    u = _suffix_ones(tq)

    def tile(kt, acc, carry, valid):
        ks = pl.multiple_of(kt * tq, tq)
        per = n_pairs // SB_PAIR_GROUPS
        accs, carries = [], []
        for g in range(SB_PAIR_GROUPS):
            prs = range(g * per, (g + 1) * per)
            rows = slice(g * per * 2 * tq, (g + 1) * per * 2 * tq)
            z = jnp.concatenate([_dot_nt(q2[p], k_ref[0, pl.ds(ks, tq), pair(p)]) for p in prs], axis=0)
            a, c = _sb_weights(z, carry[rows], valid, u)
            pv = jnp.concatenate([_dot(a[i * 2 * tq:(i + 1) * 2 * tq], v_ref[0, pl.ds(ks, tq), pair(p)])
                                  for i, p in enumerate(prs)], axis=0)
            accs.append(acc[rows] + pv)
            carries.append(c)
        return jnp.concatenate(accs, axis=0), jnp.concatenate(carries, axis=0)

    acc = jnp.zeros((m, lanes), F32)
    carry = jnp.zeros((m, 1), F32)
    acc, carry = tile(qi, acc, carry, col < row)
    no_tile = jnp.where(qi == 0, 1e30, 0.0).astype(F32)
    acc, carry = tile(jnp.maximum(qi - 1, 0), acc, carry + no_tile, None)

    def cond(c):
        return jnp.logical_and(c[0] < qi, c[3] > 0)

    def body(c):
        acc, carry = tile(qi - 1 - c[0], c[1], c[2], None)
        return c[0] + 1, acc, carry, _sb_live(carry)

    _, acc, _, _ = lax.while_loop(cond, body, (jnp.int32(1), acc, carry, _sb_live(carry)))
    o_ref[0] = jnp.concatenate(
        [jnp.where(lo_half, acc[2 * p * tq:(2 * p + 1) * tq], acc[(2 * p + 1) * tq:(2 * p + 2) * tq])
         for p in range(n_pairs)], axis=-1).astype(BF16)


def _sb_prompt(sq, sk, sv, *, tq, cast=()):
    b, t, w = sq.shape
    grid = (b, t // tq)
    q_spec = pl.BlockSpec((1, tq, w), lambda i, j: (i, j, 0))
    kv_spec = pl.BlockSpec((1, t, w), lambda i, j: (i, 0, 0))
    cast_in, cast_out, cast_shapes = _side_cast(cast, grid[0] * grid[1], lambda i, j: i * grid[1] + j)
    return pl.pallas_call(
        functools.partial(_sb_prompt_kernel, tq=tq, n_cast=len(cast)),
        grid=grid,
        in_specs=[q_spec, kv_spec, kv_spec] + cast_in,
        out_specs=[q_spec] + cast_out,
        out_shape=[jax.ShapeDtypeStruct((b, t, w), BF16)] + cast_shapes,
        compiler_params=pltpu.CompilerParams(
            dimension_semantics=("arbitrary", "arbitrary"), vmem_limit_bytes=VMEM_LIMIT),
        name="sb_prompt",
    )(sq, sk, sv, *cast)


def _sb_sample_kernel(q_ref, kn_ref, vn_ref, kct_ref, vct_ref, o_ref, carry_ref, acc_scr, carry_scr, *, tl, tk):
    ns, nh = q_ref.shape[:2]
    n_tiles = kct_ref.shape[3] // tk
    m = ns * nh * tl
    units = [(s, h) for s in range(ns) for h in range(nh)]

    def per_unit(fn):
        return jnp.concatenate([fn(i, s, h) for i, (s, h) in enumerate(units)], axis=0)

    row = lax.broadcasted_iota(jnp.int32, (tl, tl), 0)
    col = lax.broadcasted_iota(jnp.int32, (tl, tl), 1)
    z = per_unit(lambda i, s, h: _dot_nt(q_ref[s, h], kn_ref[s, h].astype(BF16)))
    a, carry = _sb_weights(z, jnp.zeros((m, 1), F32), col < row, _suffix_ones(tl))
    acc_scr[...] = per_unit(lambda i, s, h: _dot(a[i * tl:(i + 1) * tl], vn_ref[s, h].astype(BF16)))
    carry_scr[...] = carry
    u = _suffix_ones(tk)
    for j in reversed(range(n_tiles)):
        @pl.when(jnp.min(carry_scr[...]) < SB_SKIP_ABOVE)
        def _(j=j):
            cols = slice(j * tk, (j + 1) * tk)
            z = per_unit(lambda i, s, h: _dot(q_ref[s, h], kct_ref[s, h, :, cols].astype(BF16)))
            a, carry = _sb_weights(z, carry_scr[...], None, u)
            acc_scr[...] += per_unit(
                lambda i, s, h: _dot_nt(a[i * tl:(i + 1) * tl], vct_ref[s, h, :, cols].astype(BF16)))
            carry_scr[...] = carry

    acc = acc_scr[...]
    for s in range(ns):
        o_ref[s] = jnp.concatenate([acc[(s * nh + h) * tl:(s * nh + h + 1) * tl] for h in range(nh)],
                                   axis=-1).astype(BF16)
    carry_ref[...] = jnp.full(carry_ref.shape, jnp.min(carry_scr[...]), F32)


def _sb_sample_window(sqh, sk_new, sv_new, k_cache_t, v_cache_t, *, tk, window, ns):
    b, nh, tl, dh = sqh.shape
    past = k_cache_t.shape[3]
    assert past % window == 0 and window % tk == 0 and b % ns == 0
    new_spec = pl.BlockSpec((ns, nh, tl, dh), lambda i: (i, 0, 0, 0))
    cache_spec = pl.BlockSpec((ns, nh, dh, window), lambda i: (i, 0, 0, past // window - 1))
    flag_spec = pl.BlockSpec((1, 8, 128), lambda i: (i, 0, 0))
    return pl.pallas_call(
        functools.partial(_sb_sample_kernel, tl=tl, tk=tk),
        grid=(b // ns,),
        in_specs=[new_spec, new_spec, new_spec, cache_spec, cache_spec],
        out_specs=[pl.BlockSpec((ns, tl, nh * dh), lambda i: (i, 0, 0)), flag_spec],
        out_shape=[jax.ShapeDtypeStruct((b, tl, nh * dh), BF16), jax.ShapeDtypeStruct((b // ns, 8, 128), F32)],
        scratch_shapes=[pltpu.VMEM((ns * nh * tl, dh), F32), pltpu.VMEM((ns * nh * tl, 1), F32)],
        compiler_params=pltpu.CompilerParams(dimension_semantics=("parallel",), vmem_limit_bytes=VMEM_LIMIT),
        name="sb_sample",
    )(sqh, sk_new, sv_new, k_cache_t, v_cache_t)


def _sb_sample(sqh, sk_new, sv_new, k_cache_t, v_cache_t, *, tk, window, ns):
    so, carry = _sb_sample_window(sqh, sk_new, sv_new, k_cache_t, v_cache_t, tk=tk, window=window, ns=ns)
    past = k_cache_t.shape[3]
    if window == past:
        return so
    return lax.cond(jnp.min(carry) < SB_SKIP_ABOVE,
                    lambda: _sb_sample_window(sqh, sk_new, sv_new, k_cache_t, v_cache_t, tk=tk, window=past, ns=1)[0],
                    lambda: so)


def _post_kernel(x_ref, ro_ref, so_ref, mk_ref, mv_ref, cpast_ref,
                 wo_ref, ln1g_ref, ln1b_ref, wq_ref, wom_ref, ln2g_ref, ln2b_ref,
                 wup_ref, cw_ref, cb_ref, wdn_ref, ln3g_ref, ln3b_ref,
                 y_ref, cst_ref, prev_scr, act_scr, *, nb, tl, alpha, fb, n_groups):
    t = pl.program_id(1)
    m = nb * tl
    sub = CONV_W - 1

    @pl.when(t == 0)
    def _():
        prev_scr[...] = jnp.zeros_like(prev_scr)
        prev_scr[:, 8 - sub:8, :] = cpast_ref[...]

    d = x_ref.shape[-1]
    hw = ro_ref.shape[-1]
    dh = mk_ref.shape[3]
    d_ff = wdn_ref.shape[0]
    if nb == 1:
        r = tl // n_groups
        groups = [[(0, i * r, r)] for i in range(n_groups)]
    else:
        per = nb // n_groups
        groups = [[(b, b * tl, tl) for b in range(i * per, (i + 1) * per)] for i in range(n_groups)]
    x_all = x_ref[...].reshape(m, d)
    ro_all = ro_ref[...].reshape(m, hw)
    so_all = so_ref[...].reshape(m, hw)

    def mixer_and_memory(segs):
        g0, g1 = segs[0][1], segs[-1][1] + segs[-1][2]
        mix_in = jnp.concatenate([ro_all[g0:g1], so_all[g0:g1]], axis=-1)
        x1 = _ln(alpha * x_all[g0:g1] + _dot(mix_in, wo_ref[...]), ln1g_ref[...], ln1b_ref[...])
        q = _dot(x1.astype(BF16), wq_ref[...])
        qb = q.astype(BF16)
        s = jnp.concatenate([_dot_nt(qb[s0 - g0:s0 - g0 + n, h * dh:(h + 1) * dh], mk_ref[b, h])
                             for b, s0, n in segs for h in range(MEM_HEADS)], axis=0) * (dh ** -0.5)
        e = jnp.exp(s - jnp.max(s, -1, keepdims=True))
        p = (e * (1.0 / jnp.sum(e, -1, keepdims=True))).astype(BF16)
        outs, r0 = [], 0
        for b, s0, n in segs:
            heads = []
            for h in range(MEM_HEADS):
                heads.append(_dot(p[r0:r0 + n], mv_ref[b, h]).astype(BF16))
                r0 += n
            outs.append(jnp.concatenate(heads, axis=-1))
        att_in = outs[0] if len(outs) == 1 else jnp.concatenate(outs, axis=0)
        return _ln(alpha * x1 + _dot(att_in, wom_ref[...]), ln2g_ref[...], ln2b_ref[...])

    x2 = [mixer_and_memory(segs) for segs in groups]
    x2b = [v.astype(BF16) for v in x2]

    sub_idx = lax.broadcasted_iota(jnp.int32, (1, 8, fb), 1)

    def conv(u, prev8, c0):
        g = u.shape[0] // 8
        u3 = u.reshape(g, 8, fb)
        ext = jnp.concatenate([prev8[None], u3], axis=0)
        r1 = pltpu.roll(ext, 1, axis=1)
        r2 = pltpu.roll(r1, 1, axis=1)
        u1 = jnp.where(sub_idx < 1, r1[:-1], r1[1:])
        u2 = jnp.where(sub_idx < 2, r2[:-1], r2[1:])
        cols = slice(c0, c0 + fb)
        c = (cb_ref[:, cols][None] + cw_ref[0:1, cols][None] * u2
             + cw_ref[1:2, cols][None] * u1 + cw_ref[2:3, cols][None] * u3)
        return c.reshape(u.shape), u3[g - 1]

    def conv_group(u, segs, g0, c0, chain):
        parts = []
        for b, s0, n in segs:
            prev8 = chain.get(b)
            if prev8 is None:
                prev8 = prev_scr[b, :, c0:c0 + fb]
            c, chain[b] = conv(u[s0 - g0:s0 - g0 + n], prev8, c0)
            parts.append(c)
        return parts[0] if len(parts) == 1 else jnp.concatenate(parts, axis=0)

    for blk in range(d_ff // fb):
        ca, cg = blk * fb, d_ff + blk * fb
        chain_a, chain_g = {}, {}
        for segs, xg in zip(groups, x2b):
            g0, g1 = segs[0][1], segs[-1][1] + segs[-1][2]
            a = conv_group(_dot(xg, wup_ref[:, ca:ca + fb]), segs, g0, ca, chain_a)
            g = conv_group(_dot(xg, wup_ref[:, cg:cg + fb]), segs, g0, cg, chain_g)
            act_scr[g0:g1, ca:ca + fb] = (_silu(a) * g).astype(BF16)
        for b in range(nb):
            prev_scr[b, :, ca:ca + fb] = chain_a[b]
            prev_scr[b, :, cg:cg + fb] = chain_g[b]

    for segs, x2g in zip(groups, x2):
        g0, g1 = segs[0][1], segs[-1][1] + segs[-1][2]
        f = _dot(act_scr[g0:g1, :], wdn_ref[...])
        y = _ln(alpha * x2g + f, ln3g_ref[...], ln3b_ref[...])
        for b, s0, n in segs:
            y_ref[b, s0 - b * tl:s0 - b * tl + n, :] = y[s0 - g0:s0 - g0 + n]
    cst_ref[...] = prev_scr[:, 8 - sub:8, :]


def _post(x, ro, so, mk_b, mv_b, conv_past, lw, *, nb, tl, n_groups, alpha):
    b, t, d = x.shape
    assert nb == 1 or tl == t
    hw = ro.shape[-1]
    two_ff = lw["w_up"].shape[1]
    row = lambda w: pl.BlockSpec((nb, tl, w), lambda i, j: (i, j, 0))
    mem_spec = pl.BlockSpec((nb,) + mk_b.shape[1:], lambda i, j: (i, 0, 0, 0),
                            pipeline_mode=pl.Buffered(1) if nb == b else None)
    cst_spec = pl.BlockSpec((nb, CONV_W - 1, two_ff), lambda i, j: (i, 0, 0))
    names = ("w_o", "ln1_g", "ln1_b", "w_q_mem", "w_o_mem", "ln2_g", "ln2_b",
             "w_up", "conv_w", "conv_b", "w_down", "ln3_g", "ln3_b")
    weights = [lw[n] for n in names]
    assert (tl if nb == 1 else nb) % n_groups == 0 and tl % 8 == 0
    return pl.pallas_call(
        functools.partial(_post_kernel, nb=nb, tl=tl, alpha=alpha, fb=256, n_groups=n_groups),
        grid=(b // nb, t // tl),
        in_specs=[row(d), row(hw), row(hw), mem_spec, mem_spec, cst_spec] + [_const_spec(w.shape) for w in weights],
        out_specs=[row(d), cst_spec],
        out_shape=[jax.ShapeDtypeStruct((b, t, d), F32), jax.ShapeDtypeStruct((b, CONV_W - 1, two_ff), F32)],
        scratch_shapes=[pltpu.VMEM((nb, 8, two_ff), F32), pltpu.VMEM((nb * tl, two_ff // 2), BF16)],
        compiler_params=pltpu.CompilerParams(
            dimension_semantics=("parallel", "arbitrary"), vmem_limit_bytes=VMEM_LIMIT),
        name="post",
    )(x, ro, so, mk_b, mv_b, conv_past, *weights)


def _rope_tables(first, n):
    half = RET_D // 2
    inv = 1.0 / (ROPE_BASE ** (np.arange(half, dtype=np.float64) / half))
    ang = np.arange(first, first + n, dtype=np.float64)[:, None] * inv[None, :]
    c, s = np.cos(ang).astype(np.float32), np.sin(ang).astype(np.float32)
    return jnp.asarray(np.concatenate([c, c], axis=-1)), jnp.asarray(np.concatenate([-s, s], axis=-1))


def kernel(x_prompt, x_sample, cache_sb_k, cache_sb_v, state_ret, state_ffn_conv, cache_mem_k, cache_mem_v, mem_prompt, w_in, w_o, ln1_g, ln1_b, w_q_mem, w_k_mem, w_v_mem, w_o_mem, ln2_g, ln2_b, w_up, conv_w, conv_b, w_down, ln3_g, ln3_b):
    depth = w_in.shape[0]
    alpha = (2.0 * depth) ** 0.25
    xp, xs = x_prompt, x_sample
    bp, tp, _ = xp.shape
    bs, ts, _ = xs.shape
    past_len = cache_sb_k.shape[3]
    two_ff = w_up.shape[2]
    cos_p, sin_p = _rope_tables(0, tp)
    cos_s, sin_s = _rope_tables(past_len, ts)
    row2 = lambda a: a.reshape(1, -1)
    swap = lambda a: jnp.swapaxes(a, -1, -2)
    outs = [[] for _ in range(10)]
    for l in range(depth):
        lw = {"ln1_g": row2(ln1_g[l]), "ln1_b": row2(ln1_b[l]), "ln2_g": row2(ln2_g[l]), "ln2_b": row2(ln2_b[l]),
              "conv_w": conv_w[l], "conv_b": row2(conv_b[l]), "ln3_g": row2(ln3_g[l]), "ln3_b": row2(ln3_b[l])}

        mk, mv, mk_b, mv_b, w_in_b = _memkv(mem_prompt, w_k_mem[l], w_v_mem[l], cast=(w_in[l],))
        late = ("w_o", "w_q_mem", "w_o_mem", "w_up", "w_down")
        ro, ps, sq, skb, svb, pk_t, pv_t, *late_b = _proj(
            xp, w_in_b, cos_p, sin_p, nb=1, tl=min(tp, PROJ_ROWS), fused_chunk=RET_CHUNK,
            cast=(w_o[l], w_q_mem[l], w_o_mem[l], w_up[l], w_down[l]))
        lw.update(zip(late, late_b))
        mem_shape = cache_mem_k.shape[1:]
        flat = lambda a: a.reshape(-1, mem_shape[-1])
        so, cmk_b, cmv_b = _sb_prompt(sq, skb, svb, tq=SB_TILE, cast=(flat(cache_mem_k[l]), flat(cache_mem_v[l])))
        xp, pc = _post(xp, ro, so, mk_b, mv_b, jnp.zeros((bp, CONV_W - 1, two_ff), F32), lw,
                       nb=1, tl=min(tp, POST_ROWS), n_groups=POST_GROUPS, alpha=alpha)

        ret, sqh, sk, sv = _proj(xs, w_in_b, cos_s, sin_s, nb=bs, tl=ts, fused_chunk=None)
        ro, ss = _retention(ret, state_ret[l], nb=min(bs, SAMPLE_STREAMS), c=min(REF_CHUNK, ts), n_chunks=1)
        so = _sb_sample(sqh, sk, sv, swap(cache_sb_k[l]), swap(cache_sb_v[l]), tk=SB_TILE,
                        window=min(SB_WINDOW, past_len), ns=min(bs, SAMPLE_STREAMS))
        xs, sc = _post(xs, ro, so, cmk_b.reshape(mem_shape), cmv_b.reshape(mem_shape),
                       state_ffn_conv[l], lw, nb=min(bs, SAMPLE_POST_STREAMS), tl=ts, n_groups=1, alpha=alpha)
        for lst, val in zip(outs, (swap(pk_t), swap(pv_t), ps, pc, mk, mv, sk, sv, ss, sc)):
            lst.append(val)
    return (xp, xs) + tuple(jnp.stack(o) for o in outs)
```

```python
import functools
import math

import jax
import jax.numpy as jnp
import numpy as np
from jax import lax
from jax.experimental import pallas as pl
from jax.experimental.pallas import tpu as pltpu

F32 = jnp.float32
BF16 = jnp.bfloat16

RET_HEADS = 4
RET_D = 128
SB_HEADS = 8
SB_DH = 64
MEM_HEADS = 4
CONV_W = 3
ROPE_BASE = 10000.0
LN_EPS = 1e-5
RMS_EPS = 1e-6
LOG2E = math.log2(math.e)
REF_CHUNK = 64

VMEM_LIMIT = 56 * 1024 * 1024

PROJ_ROWS = 1024
RET_CHUNK = 256
SB_TILE = 256
SB_PAIR_GROUPS = 2
SB_WINDOW = 512
POST_ROWS = 512
POST_GROUPS = 2
SAMPLE_STREAMS = 8
SAMPLE_POST_STREAMS = 8


def _const_spec(shape):
    nd = len(shape)
    return pl.BlockSpec(shape, lambda *_: (0,) * nd, pipeline_mode=pl.Buffered(1))


def _ln(x, g, b):
    mu = jnp.mean(x, -1, keepdims=True)
    xc = x - mu
    var = jnp.mean(xc * xc, -1, keepdims=True)
    return xc * lax.rsqrt(var + LN_EPS) * g + b


def _silu(x):
    return x * (1.0 / (1.0 + jnp.exp(-x)))


def _dot(a, b):
    return jnp.dot(a, b, preferred_element_type=F32)


def _dot_nt(a, b):
    return lax.dot_general(a, b, (((1,), (1,)), ((), ())), preferred_element_type=F32)


def _ret_log_gamma(h):
    return math.log1p(-(2.0 ** (-5.0 - h)))


def _ret_init(s_scr, dmask_scr, s0, c):
    s_scr[...] = jnp.zeros_like(s_scr) if s0 is None else s0
    row = lax.broadcasted_iota(jnp.int32, (c, c), 0)
    col = lax.broadcasted_iota(jnp.int32, (c, c), 1)
    diff = (row - col).astype(F32)
    for h in range(RET_HEADS):
        dmask_scr[h] = jnp.where(diff >= 0, jnp.exp(_ret_log_gamma(h) * jnp.maximum(diff, 0.0)), 0.0)


def _ret_chunk(q, k, v, g, h, s_scr, dmask_scr):
    c = q.shape[0]
    log_g = _ret_log_gamma(h)
    idx = lax.broadcasted_iota(jnp.int32, (c, 1), 0).astype(F32)
    scores = _dot_nt(q, k) * dmask_scr[h]
    o = _dot(scores.astype(BF16), v)
    s_prev = s_scr[h]
    qd = (q.astype(F32) * jnp.exp(log_g * (idx + 1.0))).astype(BF16)
    o = o + _dot(qd, s_prev.astype(BF16))
    kd = (k.astype(F32) * jnp.exp(log_g * (c - 1.0 - idx))).astype(BF16)
    s_scr[h] = math.exp(log_g * c) * s_prev + _dot(kd.T, v)
    o = o * lax.rsqrt(jnp.mean(o * o, -1, keepdims=True) + RMS_EPS)
    return (o * _silu(g.astype(F32))).astype(BF16)


def _side_cast(weights, n_steps, step_of):
    in_specs, out_specs, out_shapes = [], [], []
    for w in weights:
        rows, cols = w.shape
        n_chunks = max(n for n in range(1, n_steps + 1) if (rows // 16) % n == 0)
        spec = pl.BlockSpec((rows // n_chunks, cols),
                            lambda *idx, last=n_chunks - 1: (jnp.minimum(step_of(*idx), last), 0))
        in_specs.append(spec)
        out_specs.append(spec)
        out_shapes.append(jax.ShapeDtypeStruct(w.shape, BF16))
    return in_specs, out_specs, out_shapes


def _proj_kernel(x_ref, w_ref, cos_ref, sin_ref, *refs, nb, tl, d_model, fused_chunk, n_cast):
    cast_in, refs = refs[:n_cast], refs[n_cast:]
    if fused_chunk is None:
        ret_ref, sqh_ref, sk_ref, sv_ref = refs[:4]
        cast_out = refs[4:4 + n_cast]
    else:
        ro_ref, sf_ref, sq_ref, skb_ref, svb_ref, sk_ref, sv_ref = refs[:7]
        cast_out = refs[7:7 + n_cast]
        s_scr, dmask_scr = refs[7 + n_cast:]
    for src, dst in zip(cast_in, cast_out):
        dst[...] = src[...].astype(BF16)
    m = nb * tl
    xb = x_ref[...].reshape(m, d_model).astype(BF16)
    cos = jnp.broadcast_to(cos_ref[...][None], (nb, tl, RET_D)).reshape(m, RET_D)
    sin = jnp.broadcast_to(sin_ref[...][None], (nb, tl, RET_D)).reshape(m, RET_D)
    hw = RET_HEADS * RET_D

    def seg(i):
        return _dot(xb, w_ref[:, i * hw:(i + 1) * hw])

    def rope(r, scale):
        outs = []
        for h in range(RET_HEADS):
            xh = r[:, h * RET_D:(h + 1) * RET_D]
            o = xh * cos + pltpu.roll(xh, RET_D // 2, axis=1) * sin
            outs.append(o * scale if scale != 1.0 else o)
        return jnp.concatenate(outs, axis=-1)

    if fused_chunk is not None:
        @pl.when(pl.program_id(1) == 0)
        def _():
            _ret_init(s_scr, dmask_scr, None, fused_chunk)

    ret_in = [rope(seg(0), 1.0).astype(BF16), rope(seg(1), RET_D ** -0.5).astype(BF16),
              seg(2).astype(BF16), seg(3).astype(BF16)]
    def put_heads(ref, val):
        for h in range(SB_HEADS):
            ref[:, h, :, :] = val[:, h * SB_DH:(h + 1) * SB_DH].astype(ref.dtype).reshape(nb, tl, SB_DH)

    if fused_chunk is None:
        for i, val in enumerate(ret_in):
            ret_ref[:, :, i * hw:(i + 1) * hw] = val.reshape(nb, tl, hw)
        put_heads(sqh_ref, seg(4) * (SB_DH ** -0.5))
        put_heads(sk_ref, seg(5))
        put_heads(sv_ref, seg(6))
        return

    def ret_unit(c0, h):
        blk = [val[c0:c0 + fused_chunk, h * RET_D:(h + 1) * RET_D] for val in ret_in]
        ro_ref[0, c0:c0 + fused_chunk, h * RET_D:(h + 1) * RET_D] = _ret_chunk(*blk, h, s_scr, dmask_scr)

    def sb_q():
        sq_ref[...] = (seg(4) * (SB_DH ** -0.5)).astype(BF16).reshape(nb, tl, hw)

    def sb_kv(i, b_ref, t_ref):
        r = seg(i)
        b_ref[...] = r.astype(BF16).reshape(nb, tl, hw)
        t_ref[0] = r.T.reshape(SB_HEADS, SB_DH, tl)

    ret_units = [functools.partial(ret_unit, c0, h) for c0 in range(0, tl, fused_chunk) for h in range(RET_HEADS)]
    mxu_units = [sb_q, functools.partial(sb_kv, 5, skb_ref, sk_ref), functools.partial(sb_kv, 6, svb_ref, sv_ref)]
    per = -(-len(ret_units) // len(mxu_units))
    for i, unit in enumerate(mxu_units):
        unit()
        for r in ret_units[i * per:(i + 1) * per]:
            r()

    @pl.when(pl.program_id(1) == pl.num_programs(1) - 1)
    def _():
        sf_ref[0] = s_scr[...]


def _proj(x, w_in_b, cos, sin, *, nb, tl, fused_chunk, cast=()):
    b, t, d = x.shape
    hw = RET_HEADS * RET_D
    grid = (b // nb, t // tl)
    cast_in, cast_out, cast_shapes = _side_cast(cast, grid[0] * grid[1], lambda i, j: i * grid[1] + j)
    row_spec = lambda w: pl.BlockSpec((nb, tl, w), lambda i, j: (i, j, 0))
    scratch = []
    if fused_chunk is None:
        hs = lambda dt: jax.ShapeDtypeStruct((b, SB_HEADS, t, SB_DH), dt)
        hs_spec = pl.BlockSpec((nb, SB_HEADS, tl, SB_DH), lambda i, j: (i, 0, j, 0))
        out_shape = [jax.ShapeDtypeStruct((b, t, 4 * hw), BF16), hs(BF16), hs(F32), hs(F32)]
        out_specs = [row_spec(4 * hw)] + [hs_spec] * 3
    else:
        assert nb == 1 and tl % fused_chunk == 0
        state = (RET_HEADS, RET_D, RET_D)
        out_shape = ([jax.ShapeDtypeStruct((b, t, hw), BF16), jax.ShapeDtypeStruct((b,) + state, F32)]
                     + [jax.ShapeDtypeStruct((b, t, hw), BF16)] * 3
                     + [jax.ShapeDtypeStruct((b, SB_HEADS, SB_DH, t), F32)] * 2)
        out_specs = ([row_spec(hw), pl.BlockSpec((1,) + state, lambda i, j: (i, 0, 0, 0))] + [row_spec(hw)] * 3
                     + [pl.BlockSpec((1, SB_HEADS, SB_DH, tl), lambda i, j: (i, 0, 0, j))] * 2)
        scratch = [pltpu.VMEM(state, F32), pltpu.VMEM((RET_HEADS, fused_chunk, fused_chunk), F32)]
    return pl.pallas_call(
        functools.partial(_proj_kernel, nb=nb, tl=tl, d_model=d, fused_chunk=fused_chunk, n_cast=len(cast)),
        grid=grid,
        in_specs=[row_spec(d), _const_spec(w_in_b.shape),
                  pl.BlockSpec((tl, RET_D), lambda i, j: (j, 0)),
                  pl.BlockSpec((tl, RET_D), lambda i, j: (j, 0))] + cast_in,
        out_specs=out_specs + cast_out, out_shape=out_shape + cast_shapes, scratch_shapes=scratch,
        compiler_params=pltpu.CompilerParams(
            dimension_semantics=("arbitrary", "arbitrary"), vmem_limit_bytes=VMEM_LIMIT),
        name="proj",
    )(x, w_in_b, cos, sin, *cast)


def _memkv_kernel(m_ref, wk_ref, wv_ref, *refs, dh, n_cast):
    cast_in = refs[:n_cast]
    mk_ref, mv_ref, mkb_ref, mvb_ref = refs[n_cast:n_cast + 4]
    cast_out = refs[n_cast + 4:]
    for src, dst in zip(cast_in, cast_out):
        dst[...] = src[...].astype(BF16)
    mb = m_ref[0].astype(BF16)
    for w_ref, o_ref, ob_ref in ((wk_ref, mk_ref, mkb_ref), (wv_ref, mv_ref, mvb_ref)):
        r = _dot(mb, w_ref[...].astype(BF16))
        for h in range(MEM_HEADS):
            rh = r[:, h * dh:(h + 1) * dh]
            o_ref[0, h] = rh
            ob_ref[0, h] = rh.astype(BF16)


def _memkv(mem, wk, wv, cast=()):
    b, n, d = mem.shape
    dh = d // MEM_HEADS
    spec = pl.BlockSpec((1, MEM_HEADS, n, dh), lambda i: (i, 0, 0, 0))
    cast_in, cast_out, cast_shapes = _side_cast(cast, b, lambda i: i)
    return pl.pallas_call(
        functools.partial(_memkv_kernel, dh=dh, n_cast=len(cast)),
        grid=(b,),
        in_specs=[pl.BlockSpec((1, n, d), lambda i: (i, 0, 0)), _const_spec(wk.shape), _const_spec(wv.shape)] + cast_in,
        out_specs=[spec] * 4 + cast_out,
        out_shape=[jax.ShapeDtypeStruct((b, MEM_HEADS, n, dh), F32)] * 2
        + [jax.ShapeDtypeStruct((b, MEM_HEADS, n, dh), BF16)] * 2 + cast_shapes,
        compiler_params=pltpu.CompilerParams(dimension_semantics=("arbitrary",), vmem_limit_bytes=VMEM_LIMIT),
        name="memkv",
    )(mem, wk, wv, *cast)


def _ret_kernel(ret_ref, s0_ref, ro_ref, sf_ref, s_scr, dmask_scr, *, c, n_chunks):
    t = pl.program_id(1)

    @pl.when(t == 0)
    def _():
        _ret_init(s_scr, dmask_scr, s0_ref[...], c)

    hw = RET_HEADS * RET_D
    for s in range(ret_ref.shape[0]):
        for ci in range(n_chunks):
            rows = slice(ci * c, (ci + 1) * c)
            for h in range(RET_HEADS):
                blk = [ret_ref[s, rows, i * hw + h * RET_D:i * hw + (h + 1) * RET_D] for i in range(4)]
                ro_ref[s, rows, h * RET_D:(h + 1) * RET_D] = _ret_chunk(*blk, h, s_scr.at[s], dmask_scr)

    @pl.when(t == pl.num_programs(1) - 1)
    def _():
        sf_ref[...] = s_scr[...]


def _retention(ret, s0, *, nb, c, n_chunks):
    b, t, w = ret.shape
    hw = RET_HEADS * RET_D
    tl = c * n_chunks
    state = (nb, RET_HEADS, RET_D, RET_D)
    st_spec = pl.BlockSpec(state, lambda i, j: (i, 0, 0, 0))
    return pl.pallas_call(
        functools.partial(_ret_kernel, c=c, n_chunks=n_chunks),
        grid=(b // nb, t // tl),
        in_specs=[pl.BlockSpec((nb, tl, w), lambda i, j: (i, j, 0)), st_spec],
        out_specs=[pl.BlockSpec((nb, tl, hw), lambda i, j: (i, j, 0)), st_spec],
        out_shape=[jax.ShapeDtypeStruct((b, t, hw), BF16),
                   jax.ShapeDtypeStruct((b, RET_HEADS, RET_D, RET_D), F32)],
        scratch_shapes=[pltpu.VMEM(state, F32), pltpu.VMEM((RET_HEADS, c, c), F32)],
        compiler_params=pltpu.CompilerParams(
            dimension_semantics=("parallel", "arbitrary"), vmem_limit_bytes=VMEM_LIMIT),
        name="retention",
    )(ret, s0)


SB_SKIP_ABOVE = 106.0


def _suffix_ones(n):
    j = lax.broadcasted_iota(jnp.int32, (n, n), 0)
    s = lax.broadcasted_iota(jnp.int32, (n, n), 1)
    return jnp.where(j > s, 1.0, 0.0).astype(BF16)


def _sb_weights(z, carry, valid, u):
    def masked(x):
        if valid is None:
            return x
        blocks = x.reshape((x.shape[0] // valid.shape[0],) + valid.shape)
        return jnp.where(valid[None], blocks, 0.0).reshape(x.shape)

    sp = masked(jnp.maximum(z, 0.0) + jnp.log(1.0 + jnp.exp2(jnp.abs(z) * -LOG2E)))
    later = _dot(sp.astype(BF16), u) + carry
    a = masked(jnp.exp((z - sp) - later))
    return a.astype(BF16), later[:, 0:1] + sp[:, 0:1]


def _sb_live(carry):
    return (jnp.min(carry) < SB_SKIP_ABOVE).astype(jnp.int32)


def _sb_prompt_kernel(q_ref, k_ref, v_ref, *refs, tq, n_cast):
    o_ref = refs[n_cast]
    for src, dst in zip(refs[:n_cast], refs[n_cast + 1:]):
        dst[...] = src[...].astype(BF16)
    qi = pl.program_id(1)
    lanes = 2 * SB_DH
    n_pairs = q_ref.shape[-1] // lanes
    pair = lambda p: slice(p * lanes, (p + 1) * lanes)
    lo_half = lax.broadcasted_iota(jnp.int32, (1, lanes), 1) < SB_DH
    q2 = []
    for p in range(n_pairs):
        q = q_ref[0, :, pair(p)]
        zero = jnp.zeros_like(q)
        q2.append(jnp.concatenate([jnp.where(lo_half, q, zero), jnp.where(lo_half, zero, q)], axis=0))
    m = 2 * n_pairs * tq
    row = lax.broadcasted_iota(jnp.int32, (tq, tq), 0)
    col = lax.broadcasted_iota(jnp.int32, (tq, tq), 1)
    u = _suffix_ones(tq)

    def tile(kt, acc, carry, valid):
        ks = pl.multiple_of(kt * tq, tq)
        per = n_pairs // SB_PAIR_GROUPS
        accs, carries = [], []
        for g in range(SB_PAIR_GROUPS):
            prs = range(g * per, (g + 1) * per)
            rows = slice(g * per * 2 * tq, (g + 1) * per * 2 * tq)
            z = jnp.concatenate([_dot_nt(q2[p], k_ref[0, pl.ds(ks, tq), pair(p)]) for p in prs], axis=0)
            a, c = _sb_weights(z, carry[rows], valid, u)
            pv = jnp.concatenate([_dot(a[i * 2 * tq:(i + 1) * 2 * tq], v_ref[0, pl.ds(ks, tq), pair(p)])
                                  for i, p in enumerate(prs)], axis=0)
            accs.append(acc[rows] + pv)
            carries.append(c)
        return jnp.concatenate(accs, axis=0), jnp.concatenate(carries, axis=0)

    acc = jnp.zeros((m, lanes), F32)
    carry = jnp.zeros((m, 1), F32)
    acc, carry = tile(qi, acc, carry, col < row)
    no_tile = jnp.where(qi == 0, 1e30, 0.0).astype(F32)
    acc, carry = tile(jnp.maximum(qi - 1, 0), acc, carry + no_tile, None)

    def cond(c):
        return jnp.logical_and(c[0] < qi, c[3] > 0)

    def body(c):
        acc, carry = tile(qi - 1 - c[0], c[1], c[2], None)
        return c[0] + 1, acc, carry, _sb_live(carry)

    _, acc, _, _ = lax.while_loop(cond, body, (jnp.int32(1), acc, carry, _sb_live(carry)))
    o_ref[0] = jnp.concatenate(
        [jnp.where(lo_half, acc[2 * p * tq:(2 * p + 1) * tq], acc[(2 * p + 1) * tq:(2 * p + 2) * tq])
         for p in range(n_pairs)], axis=-1).astype(BF16)


def _sb_prompt(sq, sk, sv, *, tq, cast=()):
    b, t, w = sq.shape
    grid = (b, t // tq)
    q_spec = pl.BlockSpec((1, tq, w), lambda i, j: (i, j, 0))
    kv_spec = pl.BlockSpec((1, t, w), lambda i, j: (i, 0, 0))
    cast_in, cast_out, cast_shapes = _side_cast(cast, grid[0] * grid[1], lambda i, j: i * grid[1] + j)
    return pl.pallas_call(
        functools.partial(_sb_prompt_kernel, tq=tq, n_cast=len(cast)),
        grid=grid,
        in_specs=[q_spec, kv_spec, kv_spec] + cast_in,
        out_specs=[q_spec] + cast_out,
        out_shape=[jax.ShapeDtypeStruct((b, t, w), BF16)] + cast_shapes,
        compiler_params=pltpu.CompilerParams(
            dimension_semantics=("arbitrary", "arbitrary"), vmem_limit_bytes=VMEM_LIMIT),
        name="sb_prompt",
    )(sq, sk, sv, *cast)


def _sb_sample_kernel(q_ref, kn_ref, vn_ref, kct_ref, vct_ref, o_ref, carry_ref, acc_scr, carry_scr, *, tl, tk):
    ns, nh = q_ref.shape[:2]
    n_tiles = kct_ref.shape[3] // tk
    m = ns * nh * tl
    units = [(s, h) for s in range(ns) for h in range(nh)]

    def per_unit(fn):
        return jnp.concatenate([fn(i, s, h) for i, (s, h) in enumerate(units)], axis=0)

    row = lax.broadcasted_iota(jnp.int32, (tl, tl), 0)
    col = lax.broadcasted_iota(jnp.int32, (tl, tl), 1)
    z = per_unit(lambda i, s, h: _dot_nt(q_ref[s, h], kn_ref[s, h].astype(BF16)))
    a, carry = _sb_weights(z, jnp.zeros((m, 1), F32), col < row, _suffix_ones(tl))
    acc_scr[...] = per_unit(lambda i, s, h: _dot(a[i * tl:(i + 1) * tl], vn_ref[s, h].astype(BF16)))
    carry_scr[...] = carry
    u = _suffix_ones(tk)
    for j in reversed(range(n_tiles)):
        @pl.when(jnp.min(carry_scr[...]) < SB_SKIP_ABOVE)
        def _(j=j):
            cols = slice(j * tk, (j + 1) * tk)
            z = per_unit(lambda i, s, h: _dot(q_ref[s, h], kct_ref[s, h, :, cols].astype(BF16)))
            a, carry = _sb_weights(z, carry_scr[...], None, u)
            acc_scr[...] += per_unit(
                lambda i, s, h: _dot_nt(a[i * tl:(i + 1) * tl], vct_ref[s, h, :, cols].astype(BF16)))
            carry_scr[...] = carry

    acc = acc_scr[...]
    for s in range(ns):
        o_ref[s] = jnp.concatenate([acc[(s * nh + h) * tl:(s * nh + h + 1) * tl] for h in range(nh)],
                                   axis=-1).astype(BF16)
    carry_ref[...] = jnp.full(carry_ref.shape, jnp.min(carry_scr[...]), F32)


def _sb_sample_window(sqh, sk_new, sv_new, k_cache_t, v_cache_t, *, tk, window, ns):
    b, nh, tl, dh = sqh.shape
    past = k_cache_t.shape[3]
    assert past % window == 0 and window % tk == 0 and b % ns == 0
    new_spec = pl.BlockSpec((ns, nh, tl, dh), lambda i: (i, 0, 0, 0))
    cache_spec = pl.BlockSpec((ns, nh, dh, window), lambda i: (i, 0, 0, past // window - 1))
    flag_spec = pl.BlockSpec((1, 8, 128), lambda i: (i, 0, 0))
    return pl.pallas_call(
        functools.partial(_sb_sample_kernel, tl=tl, tk=tk),
        grid=(b // ns,),
        in_specs=[new_spec, new_spec, new_spec, cache_spec, cache_spec],
        out_specs=[pl.BlockSpec((ns, tl, nh * dh), lambda i: (i, 0, 0)), flag_spec],
        out_shape=[jax.ShapeDtypeStruct((b, tl, nh * dh), BF16), jax.ShapeDtypeStruct((b // ns, 8, 128), F32)],
        scratch_shapes=[pltpu.VMEM((ns * nh * tl, dh), F32), pltpu.VMEM((ns * nh * tl, 1), F32)],
        compiler_params=pltpu.CompilerParams(dimension_semantics=("parallel",), vmem_limit_bytes=VMEM_LIMIT),
        name="sb_sample",
    )(sqh, sk_new, sv_new, k_cache_t, v_cache_t)


def _sb_sample(sqh, sk_new, sv_new, k_cache_t, v_cache_t, *, tk, window, ns):
    so, carry = _sb_sample_window(sqh, sk_new, sv_new, k_cache_t, v_cache_t, tk=tk, window=window, ns=ns)
    past = k_cache_t.shape[3]
    if window == past:
        return so
    return lax.cond(jnp.min(carry) < SB_SKIP_ABOVE,
                    lambda: _sb_sample_window(sqh, sk_new, sv_new, k_cache_t, v_cache_t, tk=tk, window=past, ns=1)[0],
                    lambda: so)


def _post_kernel(x_ref, ro_ref, so_ref, mk_ref, mv_ref, cpast_ref,
                 wo_ref, ln1g_ref, ln1b_ref, wq_ref, wom_ref, ln2g_ref, ln2b_ref,
                 wup_ref, cw_ref, cb_ref, wdn_ref, ln3g_ref, ln3b_ref,
                 y_ref, cst_ref, prev_scr, act_scr, *, nb, tl, alpha, fb, n_groups):
    t = pl.program_id(1)
    m = nb * tl
    sub = CONV_W - 1

    @pl.when(t == 0)
    def _():
        prev_scr[...] = jnp.zeros_like(prev_scr)
        prev_scr[:, 8 - sub:8, :] = cpast_ref[...]

    d = x_ref.shape[-1]
    hw = ro_ref.shape[-1]
    dh = mk_ref.shape[3]
    d_ff = wdn_ref.shape[0]
    if nb == 1:
        r = tl // n_groups
        groups = [[(0, i * r, r)] for i in range(n_groups)]
    else:
        per = nb // n_groups
        groups = [[(b, b * tl, tl) for b in range(i * per, (i + 1) * per)] for i in range(n_groups)]
    x_all = x_ref[...].reshape(m, d)
    ro_all = ro_ref[...].reshape(m, hw)
    so_all = so_ref[...].reshape(m, hw)

    def mixer_and_memory(segs):
        g0, g1 = segs[0][1], segs[-1][1] + segs[-1][2]
        mix_in = jnp.concatenate([ro_all[g0:g1], so_all[g0:g1]], axis=-1)
        x1 = _ln(alpha * x_all[g0:g1] + _dot(mix_in, wo_ref[...]), ln1g_ref[...], ln1b_ref[...])
        q = _dot(x1.astype(BF16), wq_ref[...])
        qb = q.astype(BF16)
        s = jnp.concatenate([_dot_nt(qb[s0 - g0:s0 - g0 + n, h * dh:(h + 1) * dh], mk_ref[b, h])
                             for b, s0, n in segs for h in range(MEM_HEADS)], axis=0) * (dh ** -0.5)
        e = jnp.exp(s - jnp.max(s, -1, keepdims=True))
        p = (e * (1.0 / jnp.sum(e, -1, keepdims=True))).astype(BF16)
        outs, r0 = [], 0
        for b, s0, n in segs:
            heads = []
            for h in range(MEM_HEADS):
                heads.append(_dot(p[r0:r0 + n], mv_ref[b, h]).astype(BF16))
                r0 += n
            outs.append(jnp.concatenate(heads, axis=-1))
        att_in = outs[0] if len(outs) == 1 else jnp.concatenate(outs, axis=0)
        return _ln(alpha * x1 + _dot(att_in, wom_ref[...]), ln2g_ref[...], ln2b_ref[...])

    if nb == 1:
        x2_full = mixer_and_memory([(0, 0, tl)])
        x2 = [x2_full[segs[0][1]:segs[-1][1] + segs[-1][2]] for segs in groups]
    else:
        x2 = [mixer_and_memory(segs) for segs in groups]
    x2b = [v.astype(BF16) for v in x2]

    sub_idx = lax.broadcasted_iota(jnp.int32, (1, 8, fb), 1)

    def conv(u, prev8, c0):
        g = u.shape[0] // 8
        u3 = u.reshape(g, 8, fb)
        ext = jnp.concatenate([prev8[None], u3], axis=0)
        r1 = pltpu.roll(ext, 1, axis=1)
        r2 = pltpu.roll(r1, 1, axis=1)
        u1 = jnp.where(sub_idx < 1, r1[:-1], r1[1:])
        u2 = jnp.where(sub_idx < 2, r2[:-1], r2[1:])
        cols = slice(c0, c0 + fb)
        c = (cb_ref[:, cols][None] + cw_ref[0:1, cols][None] * u2
             + cw_ref[1:2, cols][None] * u1 + cw_ref[2:3, cols][None] * u3)
        return c.reshape(u.shape), u3[g - 1]

    def conv_group(u, segs, g0, c0, chain):
        parts = []
        for b, s0, n in segs:
            prev8 = chain.get(b)
            if prev8 is None:
                prev8 = prev_scr[b, :, c0:c0 + fb]
            c, chain[b] = conv(u[s0 - g0:s0 - g0 + n], prev8, c0)
            parts.append(c)
        return parts[0] if len(parts) == 1 else jnp.concatenate(parts, axis=0)

    for blk in range(d_ff // fb):
        ca, cg = blk * fb, d_ff + blk * fb
        chain_a, chain_g = {}, {}
        for segs, xg in zip(groups, x2b):
            g0, g1 = segs[0][1], segs[-1][1] + segs[-1][2]
            a = conv_group(_dot(xg, wup_ref[:, ca:ca + fb]), segs, g0, ca, chain_a)
            g = conv_group(_dot(xg, wup_ref[:, cg:cg + fb]), segs, g0, cg, chain_g)
            act_scr[g0:g1, ca:ca + fb] = (_silu(a) * g).astype(BF16)
        for b in range(nb):
            prev_scr[b, :, ca:ca + fb] = chain_a[b]
            prev_scr[b, :, cg:cg + fb] = chain_g[b]

    for segs, x2g in zip(groups, x2):
        g0, g1 = segs[0][1], segs[-1][1] + segs[-1][2]
        f = _dot(act_scr[g0:g1, :], wdn_ref[...])
        y = _ln(alpha * x2g + f, ln3g_ref[...], ln3b_ref[...])
        for b, s0, n in segs:
            y_ref[b, s0 - b * tl:s0 - b * tl + n, :] = y[s0 - g0:s0 - g0 + n]
    cst_ref[...] = prev_scr[:, 8 - sub:8, :]


def _post(x, ro, so, mk_b, mv_b, conv_past, lw, *, nb, tl, n_groups, alpha):
    b, t, d = x.shape
    assert nb == 1 or tl == t
    hw = ro.shape[-1]
    two_ff = lw["w_up"].shape[1]
    row = lambda w: pl.BlockSpec((nb, tl, w), lambda i, j: (i, j, 0))
    mem_spec = pl.BlockSpec((nb,) + mk_b.shape[1:], lambda i, j: (i, 0, 0, 0),
                            pipeline_mode=pl.Buffered(1) if nb == b else None)
    cst_spec = pl.BlockSpec((nb, CONV_W - 1, two_ff), lambda i, j: (i, 0, 0))
    names = ("w_o", "ln1_g", "ln1_b", "w_q_mem", "w_o_mem", "ln2_g", "ln2_b",
             "w_up", "conv_w", "conv_b", "w_down", "ln3_g", "ln3_b")
    weights = [lw[n] for n in names]
    assert (tl if nb == 1 else nb) % n_groups == 0 and tl % 8 == 0
    return pl.pallas_call(
        functools.partial(_post_kernel, nb=nb, tl=tl, alpha=alpha, fb=256, n_groups=n_groups),
        grid=(b // nb, t // tl),
        in_specs=[row(d), row(hw), row(hw), mem_spec, mem_spec, cst_spec] + [_const_spec(w.shape) for w in weights],
        out_specs=[row(d), cst_spec],
        out_shape=[jax.ShapeDtypeStruct((b, t, d), F32), jax.ShapeDtypeStruct((b, CONV_W - 1, two_ff), F32)],
        scratch_shapes=[pltpu.VMEM((nb, 8, two_ff), F32), pltpu.VMEM((nb * tl, two_ff // 2), BF16)],
        compiler_params=pltpu.CompilerParams(
            dimension_semantics=("parallel", "arbitrary"), vmem_limit_bytes=VMEM_LIMIT),
        name="post",
    )(x, ro, so, mk_b, mv_b, conv_past, *weights)


def _rope_tables(first, n):
    half = RET_D // 2
    inv = 1.0 / (ROPE_BASE ** (np.arange(half, dtype=np.float64) / half))
    ang = np.arange(first, first + n, dtype=np.float64)[:, None] * inv[None, :]
    c, s = np.cos(ang).astype(np.float32), np.sin(ang).astype(np.float32)
    return jnp.asarray(np.concatenate([c, c], axis=-1)), jnp.asarray(np.concatenate([-s, s], axis=-1))


def kernel(x_prompt, x_sample, cache_sb_k, cache_sb_v, state_ret, state_ffn_conv, cache_mem_k, cache_mem_v, mem_prompt, w_in, w_o, ln1_g, ln1_b, w_q_mem, w_k_mem, w_v_mem, w_o_mem, ln2_g, ln2_b, w_up, conv_w, conv_b, w_down, ln3_g, ln3_b):
    depth = w_in.shape[0]
    alpha = (2.0 * depth) ** 0.25
    xp, xs = x_prompt, x_sample
    bp, tp, _ = xp.shape
    bs, ts, _ = xs.shape
    past_len = cache_sb_k.shape[3]
    two_ff = w_up.shape[2]
    cos_p, sin_p = _rope_tables(0, tp)
    cos_s, sin_s = _rope_tables(past_len, ts)
    row2 = lambda a: a.reshape(1, -1)
    swap = lambda a: jnp.swapaxes(a, -1, -2)
    outs = [[] for _ in range(10)]
    for l in range(depth):
        lw = {"ln1_g": row2(ln1_g[l]), "ln1_b": row2(ln1_b[l]), "ln2_g": row2(ln2_g[l]), "ln2_b": row2(ln2_b[l]),
              "conv_w": conv_w[l], "conv_b": row2(conv_b[l]), "ln3_g": row2(ln3_g[l]), "ln3_b": row2(ln3_b[l])}

        mk, mv, mk_b, mv_b, w_in_b = _memkv(mem_prompt, w_k_mem[l], w_v_mem[l], cast=(w_in[l],))
        late = ("w_o", "w_q_mem", "w_o_mem", "w_up", "w_down")
        ro, ps, sq, skb, svb, pk_t, pv_t, *late_b = _proj(
            xp, w_in_b, cos_p, sin_p, nb=1, tl=min(tp, PROJ_ROWS), fused_chunk=RET_CHUNK,
            cast=(w_o[l], w_q_mem[l], w_o_mem[l], w_up[l], w_down[l]))
        lw.update(zip(late, late_b))
        mem_shape = cache_mem_k.shape[1:]
        flat = lambda a: a.reshape(-1, mem_shape[-1])
        so, cmk_b, cmv_b = _sb_prompt(sq, skb, svb, tq=SB_TILE, cast=(flat(cache_mem_k[l]), flat(cache_mem_v[l])))
        xp, pc = _post(xp, ro, so, mk_b, mv_b, jnp.zeros((bp, CONV_W - 1, two_ff), F32), lw,
                       nb=1, tl=min(tp, POST_ROWS), n_groups=POST_GROUPS, alpha=alpha)

        ret, sqh, sk, sv = _proj(xs, w_in_b, cos_s, sin_s, nb=bs, tl=ts, fused_chunk=None)
        ro, ss = _retention(ret, state_ret[l], nb=min(bs, SAMPLE_STREAMS), c=min(REF_CHUNK, ts), n_chunks=1)
        so = _sb_sample(sqh, sk, sv, swap(cache_sb_k[l]), swap(cache_sb_v[l]), tk=SB_TILE,
                        window=min(SB_WINDOW, past_len), ns=min(bs, SAMPLE_STREAMS))
        xs, sc = _post(xs, ro, so, cmk_b.reshape(mem_shape), cmv_b.reshape(mem_shape),
                       state_ffn_conv[l], lw, nb=min(bs, SAMPLE_POST_STREAMS), tl=ts, n_groups=1, alpha=alpha)
        for lst, val in zip(outs, (swap(pk_t), swap(pv_t), ps, pc, mk, mv, sk, sv, ss, sc)):
            lst.append(val)
    return (xp, xs) + tuple(jnp.stack(o) for o in outs)
```
